```python
import jax, jax.numpy as jnp
from jax import lax
import numpy as np

D_MODEL = 1024
BATCH = 8
SEQ = 2048
DEPTH = 2

D_MIX = D_MODEL
M_HEADS = 4
M_DH = 96
M_W = M_HEADS * M_DH
M_CHUNK = 64
M_CONV = 4
A_HEADS = 6
A_KV = 2
A_DH = 64
A_W = A_HEADS * A_DH
A_KVW = A_KV * A_DH
WINDOW = 128
S_HEADS = 4
S_DH = 64
S_W = S_HEADS * S_DH
S_BLOCK = 128
EPS = 1e-6

SPLITS = (M_W, M_W, M_HEADS, M_HEADS, M_W,
          A_W, A_KVW, A_KVW, A_W,
          S_W, S_W, S_W, S_W)
N_IN = int(sum(SPLITS))
OFFSETS = tuple(int(o) for o in np.cumsum(SPLITS)[:-1])

kernel_name = "hymba_style_mlstm_swa_stickbreak"


def rmsnorm(x, g):
    xf = x.astype(jnp.float32)
    y = xf * lax.rsqrt(jnp.mean(xf * xf, axis=-1, keepdims=True) + EPS)
    return (y * g.astype(jnp.float32)).astype(x.dtype)


def alibi_slopes():
    return jnp.asarray(2.0 ** (-8.0 * np.arange(1, A_HEADS + 1) / A_HEADS), dtype=jnp.float32)


def causal_conv(u, w, b):
    C = u.shape[-1]
    out = lax.conv_general_dilated(u, w.reshape(M_CONV, 1, C).astype(u.dtype), window_strides=(1,),
                                   padding=[(M_CONV - 1, 0)], dimension_numbers=('NWC', 'WIO', 'NWC'),
                                   feature_group_count=C)
    return out + b


def mlstm_branch(u, v, i_pre, f_pre, conv_w, conv_b, wq, wk, b_i, b_f, hn_g, skip):
    B, S, _ = u.shape
    f32 = jnp.float32
    cu = jax.nn.silu(causal_conv(u, conv_w, conv_b))
    ch = cu.reshape(B, S, M_HEADS, M_DH)
    q = jnp.einsum('bshd,hde->bhse', ch, wq).astype(f32) * (M_DH ** -0.5)
    k = jnp.einsum('bshd,hde->bhse', ch, wk).astype(f32)
    vh = v.reshape(B, S, M_HEADS, M_DH).transpose(0, 2, 1, 3).astype(f32)
    log_i = (i_pre + b_i).astype(f32).transpose(0, 2, 1)
    log_f = jax.nn.log_sigmoid((f_pre + b_f).astype(f32)).transpose(0, 2, 1)
    nc = S // M_CHUNK

    def chunks(t):
        return jnp.moveaxis(t.reshape(B, M_HEADS, nc, M_CHUNK, *t.shape[3:]), 2, 0)

    causal = jnp.tril(jnp.ones((M_CHUNK, M_CHUNK), dtype=bool))

    def step(carry, inp):
        C, n, m = carry
        qb, kb, vb, li, lf = inp
        b = jnp.cumsum(lf, axis=-1)
        g = b[..., -1]
        dmat = jnp.where(causal, b[..., :, None] - b[..., None, :] + li[..., None, :], -jnp.inf)
        inter = b + m[..., None]
        m_t = jnp.maximum(inter, jnp.max(dmat, axis=-1))
        w_inter = jnp.exp(inter - m_t)
        s = jnp.einsum('bhtd,bhsd->bhts', qb, kb) * jnp.exp(dmat - m_t[..., None])
        num = w_inter[..., None] * jnp.einsum('bhed,bhtd->bhte', C, qb) + jnp.einsum('bhts,bhse->bhte', s, vb)
        den = w_inter * jnp.einsum('bhd,bhtd->bht', n, qb) + jnp.sum(s, axis=-1)
        h = num / jnp.maximum(jnp.abs(den), jnp.exp(-m_t))[..., None]
        decay_s = g[..., None] - b + li
        m_next = jnp.maximum(g + m, jnp.max(decay_s, axis=-1))
        ws = jnp.exp(decay_s - m_next[..., None])
        carry_scale = jnp.exp(g + m - m_next)
        C_next = carry_scale[..., None, None] * C + jnp.einsum('bhs,bhse,bhsd->bhed', ws, vb, kb)
        n_next = carry_scale[..., None] * n + jnp.einsum('bhs,bhsd->bhd', ws, kb)
        return (C_next, n_next, m_next), h

    init = (jnp.zeros((B, M_HEADS, M_DH, M_DH), f32), jnp.zeros((B, M_HEADS, M_DH), f32),
            jnp.zeros((B, M_HEADS), f32))
    _, hs = lax.scan(step, init, (chunks(q), chunks(k), chunks(vh), chunks(log_i), chunks(log_f)))
    h = jnp.moveaxis(hs, 0, 2).reshape(B, M_HEADS, S, M_DH).transpose(0, 2, 1, 3)
    h = h * lax.rsqrt(jnp.mean(h * h, axis=-1, keepdims=True) + EPS)
    h = h.reshape(B, S, M_W) * hn_g.astype(f32)
    return (h + skip.astype(f32) * cu.astype(f32)).astype(u.dtype)


def swa_branch(q, k, v, sinks):
    B, S, _ = q.shape
    nb = S // WINDOW
    G = A_HEADS // A_KV
    qb = q.reshape(B, nb, WINDOW, A_KV, G, A_DH)
    kh = k.reshape(B, nb, WINDOW, A_KV, A_DH)
    vh = v.reshape(B, nb, WINDOW, A_KV, A_DH)
    pad = jnp.zeros_like(kh[:, :1])
    kk = jnp.concatenate([jnp.concatenate([pad, kh[:, :-1]], 1), kh], 2)
    vv = jnp.concatenate([jnp.concatenate([pad, vh[:, :-1]], 1), vh], 2)
    scores = jnp.einsum('bnqkgd,bnskd->bnkgqs', qb, kk).astype(jnp.float32) * (A_DH ** -0.5)
    qi = jnp.arange(WINDOW)[:, None]
    si = jnp.arange(2 * WINDOW)[None, :]
    rel = WINDOW + qi - si
    band = (rel >= 0) & (rel < WINDOW)
    valid = band[None] & ((jnp.arange(nb)[:, None, None] - 1) * WINDOW + si[None] >= 0)
    alibi = -alibi_slopes().reshape(A_KV, G)[:, :, None, None] * rel.astype(jnp.float32)
    scores = jnp.where(valid[None, :, None, None], scores + alibi[None, None], -jnp.inf)
    sink = jnp.broadcast_to(sinks.astype(jnp.float32).reshape(A_KV, G)[None, None, :, :, None, None],
                            scores.shape[:-1] + (1,))
    probs = jax.nn.softmax(jnp.concatenate([scores, sink], axis=-1), axis=-1)[..., :-1]
    out = jnp.einsum('bnkgqs,bnskd->bnqkgd', probs.astype(v.dtype), vv)
    return out.reshape(B, S, A_W)


def stickbreak_branch(q, k, v):
    B, S, _ = q.shape
    nb = S // S_BLOCK
    qh = q.reshape(B, nb, S_BLOCK, S_HEADS, S_DH).transpose(1, 0, 3, 2, 4)
    kh = k.reshape(B, S, S_HEADS, S_DH).transpose(0, 2, 1, 3)
    vh = v.reshape(B, S, S_HEADS, S_DH).transpose(0, 2, 1, 3)
    kpos = jnp.arange(S)

    def block(args):
        qb, n = args
        z = jnp.einsum('bhqd,bhsd->bhqs', qb, kh).astype(jnp.float32) * (S_DH ** -0.5)
        tpos = n * S_BLOCK + jnp.arange(S_BLOCK)
        strict = kpos[None, :] < tpos[:, None]
        log_keep = jnp.where(strict, jax.nn.log_sigmoid(-z), 0.0)
        suffix = lax.cumsum(log_keep, axis=log_keep.ndim - 1, reverse=True) - log_keep
        a = jnp.where(strict, jnp.exp(jax.nn.log_sigmoid(z) + suffix), 0.0)
        return jnp.einsum('bhqs,bhsd->bhqd', a.astype(vh.dtype), vh)

    out = lax.map(block, (qh, jnp.arange(nb)))
    return out.transpose(1, 0, 3, 2, 4).reshape(B, S, S_W)


def hybrid_layer(x, c_act, w_mod, b_mod, g_pre, g_post, w_in, m_conv_w, m_conv_b, m_wq, m_wk,
                 m_b_i, m_b_f, m_norm_g, m_skip, a_sinks, w_out):
    mod = c_act @ w_mod + b_mod
    shift, scale, gate = jnp.split(mod, 3, axis=-1)
    h = rmsnorm(x, g_pre) * (1.0 + scale[:, None]) + shift[:, None]
    proj = h @ w_in
    (mu, mv, mi, mf, mz, aq, ak, av, az, sq, sk, sv, sz) = jnp.split(proj, OFFSETS, axis=-1)
    y_m = mlstm_branch(mu, mv, mi, mf, m_conv_w, m_conv_b, m_wq, m_wk, m_b_i, m_b_f, m_norm_g, m_skip) * jax.nn.silu(mz)
    y_a = swa_branch(aq, ak, av, a_sinks) * jax.nn.silu(az)
    y_s = stickbreak_branch(sq, sk, sv) * jax.nn.silu(sz)
    y = jnp.concatenate([y_m, y_a, y_s], axis=-1) @ w_out
    return x + gate[:, None] * rmsnorm(y, g_post)


def setup_inputs(seed: int = 0) -> dict:
    key = jax.random.key(seed)
    ks = jax.random.split(key, 20)
    nrm = jax.random.normal
    f32 = jnp.float32
    d = D_MODEL
    return {
        "x": nrm(ks[0], (BATCH, SEQ, d), f32),
        "c": nrm(ks[1], (BATCH, d), f32),
        "w_mod": nrm(ks[2], (DEPTH, d, 3 * d), f32) * (0.5 * d ** -0.5),
        "b_mod": nrm(ks[3], (DEPTH, 3 * d), f32) * 0.02,
        "g_pre": 1.0 + 0.05 * nrm(ks[4], (DEPTH, d), f32),
        "g_post": 1.0 + 0.05 * nrm(ks[5], (DEPTH, d), f32),
        "w_in": nrm(ks[6], (DEPTH, d, N_IN), f32) * d ** -0.5,
        "m_conv_w": nrm(ks[7], (DEPTH, M_CONV, M_W), f32) * M_CONV ** -0.5,
        "m_conv_b": nrm(ks[8], (DEPTH, M_W), f32) * 0.02,
        "m_wq": nrm(ks[9], (DEPTH, M_HEADS, M_DH, M_DH), f32) * M_DH ** -0.5,
        "m_wk": nrm(ks[10], (DEPTH, M_HEADS, M_DH, M_DH), f32) * M_DH ** -0.5,
        "m_b_i": nrm(ks[11], (DEPTH, M_HEADS), f32) * 0.1,
        "m_b_f": 3.0 + 0.5 * nrm(ks[12], (DEPTH, M_HEADS), f32),
        "m_norm_g": 1.0 + 0.05 * nrm(ks[13], (DEPTH, M_W), f32),
        "m_skip": 1.0 + 0.05 * nrm(ks[14], (DEPTH, M_W), f32),
        "a_sinks": nrm(ks[15], (DEPTH, A_HEADS), f32) * 0.5,
        "w_out": nrm(ks[16], (DEPTH, D_MIX, d), f32) * D_MIX ** -0.5,
    }


def reference(x, c, w_mod, b_mod, g_pre, g_post, w_in, m_conv_w, m_conv_b, m_wq, m_wk,
              m_b_i, m_b_f, m_norm_g, m_skip, a_sinks, w_out):
    c_act = jax.nn.silu(c)
    for l in range(DEPTH):
        x = hybrid_layer(x, c_act, w_mod[l], b_mod[l], g_pre[l], g_post[l], w_in[l], m_conv_w[l], m_conv_b[l],
                         m_wq[l], m_wk[l], m_b_i[l], m_b_f[l], m_norm_g[l], m_skip[l], a_sinks[l], w_out[l])
    return x
```

```python
import functools

import jax
import jax.numpy as jnp
import numpy as np
from jax import lax
from jax.experimental import pallas as pl
from jax.experimental.pallas import tpu as pltpu

D_MODEL = 1024
DEPTH = 2
M_HEADS = 4
M_DH = 96
M_W = M_HEADS * M_DH
M_CONV = 4
A_HEADS = 6
A_KV = 2
A_G = A_HEADS // A_KV
A_DH = 64
A_W = A_HEADS * A_DH
A_KVW = A_KV * A_DH
WINDOW = 128
S_HEADS = 4
S_DH = 64
S_W = S_HEADS * S_DH
S_BLOCK = 128
EPS = 1e-6

LANES = 128
M_CHUNK = 128
M_WP = M_HEADS * LANES

OFF_MU, OFF_MV, OFF_MZ = 0, M_WP, 2 * M_WP
OFF_AQ = 3 * M_WP
OFF_AZ = OFF_AQ + A_W
OFF_AK = OFF_AZ + A_W
OFF_AV = OFF_AK + A_KVW
OFF_SQ = OFF_AV + A_KVW
OFF_SK = OFF_SQ + S_W
OFF_SV = OFF_SK + S_W
OFF_SZ = OFF_SV + S_W
N_PACK = OFF_SZ + S_W

VMEM_LIMIT = 48 * 1024 * 1024

_NT = (((1,), (1,)), ((), ()))


def _log_sigmoid(x):
    return jnp.minimum(x, 0.0) - jnp.log(1.0 + jnp.exp(-jnp.abs(x)))


def _silu(x):
    return x / (1.0 + jnp.exp(-x))


def _params(n_axes):
    return pltpu.CompilerParams(dimension_semantics=("arbitrary",) * n_axes, vmem_limit_bytes=VMEM_LIMIT)


def _mod_kernel(c_ref, w_ref, b_ref, o_ref):
    c_act = _silu(c_ref[...]).astype(jnp.bfloat16)
    o_ref[0] = jnp.dot(c_act, w_ref[0].astype(jnp.bfloat16), preferred_element_type=jnp.float32) + b_ref[0]


def _modulation(c, w_mod, b_mod):
    depth, d, n = w_mod.shape
    bsz = c.shape[0]
    tn = 1024
    return pl.pallas_call(
        _mod_kernel,
        grid=(depth, n // tn),
        in_specs=[pl.BlockSpec((bsz, d), lambda l, j: (0, 0)),
                  pl.BlockSpec((1, d, tn), lambda l, j: (l, 0, j)),
                  pl.BlockSpec((1, 1, tn), lambda l, j: (l, 0, j))],
        out_specs=pl.BlockSpec((1, bsz, tn), lambda l, j: (l, 0, j)),
        out_shape=jax.ShapeDtypeStruct((depth, bsz, n), jnp.float32),
        compiler_params=_params(2),
        name="modulation",
    )(c, w_mod, b_mod.reshape(depth, 1, n))


def _inproj_kernel(x_ref, shift_ref, scale_ref, g_ref, w_ref, wg_ref, cw_ref, cb_ref,
                   proj_ref, gates_ref, conv_ref, *, tm, nchunk):
    s = pl.program_id(1)
    x = x_ref[0]
    ms = jnp.mean(x * x, axis=-1, keepdims=True)
    h = x * lax.rsqrt(ms + EPS) * g_ref[...]
    h = (h * (1.0 + scale_ref[...]) + shift_ref[...]).astype(jnp.bfloat16)

    gates_ref[0] = lax.dot_general(wg_ref[...], h, _NT, preferred_element_type=jnp.float32)

    @pl.when(s == 0)
    def _():
        conv_ref[0:8, :] = jnp.zeros((8, M_WP), jnp.float32)

    conv_ref[8:8 + tm, :] = jnp.dot(h, w_ref[:, OFF_MU:OFF_MU + M_WP], preferred_element_type=jnp.float32)
    acc = cb_ref[...] + cw_ref[M_CONV - 1:M_CONV, :] * conv_ref[8:8 + tm, :]
    for j in range(M_CONV - 1):
        lag = M_CONV - 1 - j
        acc = acc + cw_ref[j:j + 1, :] * conv_ref[8 - lag:8 - lag + tm, :]
    proj_ref[0, :, OFF_MU:OFF_MU + M_WP] = _silu(acc).astype(jnp.bfloat16)
    conv_ref[0:8, :] = conv_ref[tm:tm + 8, :]

    for c0 in range(OFF_MV, N_PACK, nchunk):
        proj_ref[0, :, c0:c0 + nchunk] = jnp.dot(
            h, w_ref[:, c0:c0 + nchunk], preferred_element_type=jnp.float32).astype(jnp.bfloat16)


def _inproj(x, mod4, layer, g_pre, w_pack, wg_t, conv_w, conv_b):
    bsz, seq, d = x.shape
    tm = 512
    kern = functools.partial(_inproj_kernel, tm=tm, nchunk=512)
    return pl.pallas_call(
        kern,
        grid=(bsz, seq // tm),
        in_specs=[pl.BlockSpec((1, tm, d), lambda b, s: (b, s, 0)),
                  pl.BlockSpec((None, None, 1, d), lambda b, s: (layer, b, 0, 0)),
                  pl.BlockSpec((None, None, 1, d), lambda b, s: (layer, b, 0, 1)),
                  pl.BlockSpec((1, d), lambda b, s: (0, 0)),
                  pl.BlockSpec((d, N_PACK), lambda b, s: (0, 0)),
                  pl.BlockSpec((8, d), lambda b, s: (0, 0)),
                  pl.BlockSpec((M_CONV, M_WP), lambda b, s: (0, 0)),
                  pl.BlockSpec((1, M_WP), lambda b, s: (0, 0))],
        out_specs=[pl.BlockSpec((1, tm, N_PACK), lambda b, s: (b, s, 0)),
                   pl.BlockSpec((1, 8, tm), lambda b, s: (b, 0, s))],
        out_shape=[jax.ShapeDtypeStruct((bsz, seq, N_PACK), jnp.bfloat16),
                   jax.ShapeDtypeStruct((bsz, 8, seq), jnp.float32)],
        scratch_shapes=[pltpu.VMEM((tm + 8, M_WP), jnp.float32)],
        compiler_params=_params(2),
        name="inproj",
    )(x, mod4, mod4, g_pre, w_pack, wg_t, conv_w, conv_b)


def _mlstm_kernel(cu_ref, v_ref, z_ref, gates_ref, gbias_ref, wq_ref, wk_ref, hg_ref, skip_ref,
                  o_ref, ct_ref, m_ref, a_ref, b_ref, *, seq):
    L = M_CHUNK
    nchunks = seq // L
    f32 = jnp.float32
    lane8 = lax.broadcasted_iota(jnp.int32, (8, L), 1)
    row8 = lax.broadcasted_iota(jnp.int32, (8, L), 0)

    for c in range(nchunks):
        g = gates_ref[0, :, c * L:(c + 1) * L] + gbias_ref[...]
        cum = jnp.where(row8 >= M_HEADS, _log_sigmoid(g), 0.0)
        k = 1
        while k < L:
            cum = cum + jnp.where(lane8 >= k, pltpu.roll(cum, k, axis=1), 0.0)
            k *= 2
        a_ref[:, c * L:(c + 1) * L] = g - pltpu.roll(cum, M_HEADS, axis=0)
        b_ref[:, c * L:(c + 1) * L] = cum

    ct_ref[...] = jnp.zeros(ct_ref.shape, f32)
    m_ref[...] = jnp.zeros(m_ref.shape, f32)

    t_idx = lax.broadcasted_iota(jnp.int32, (L, L), 0)
    s_idx = lax.broadcasted_iota(jnp.int32, (L, L), 1)
    tri = s_idx <= t_idx
    eye = s_idx == t_idx
    lane = lax.broadcasted_iota(jnp.int32, (L, LANES), 1)
    scale = M_DH ** -0.5

    def chunk(c, carry):
        r0 = pl.multiple_of(c * L, L)
        for h in range(M_HEADS):
            cols = slice(h * LANES, (h + 1) * LANES)
            cu = cu_ref[0, pl.ds(r0, L), cols]
            q = (jnp.dot(cu, wq_ref[h], preferred_element_type=f32) * scale).astype(jnp.bfloat16)
            kt = jnp.dot(cu, wk_ref[h], preferred_element_type=f32).T
            v = v_ref[0, pl.ds(r0, L), cols]
            v = jnp.where(lane == M_DH, 1.0, v.astype(f32)).astype(jnp.bfloat16)

            a_row = a_ref[h:h + 1, pl.ds(r0, L)]
            b_row = b_ref[M_HEADS + h:M_HEADS + h + 1, pl.ds(r0, L)]
            m_prev = m_ref[h:h + 1, 0:1]
            a_mat = jnp.broadcast_to(a_row, (L, L))
            m1 = jnp.max(jnp.where(tri, a_mat, -jnp.inf), axis=1, keepdims=True)
            b_col = jnp.sum(jnp.where(eye, jnp.broadcast_to(b_row, (L, L)), 0.0), axis=1, keepdims=True)
            mm = jnp.maximum(m_prev, m1)
            p = jnp.where(tri, jnp.exp(a_mat - mm), 0.0)
            w_inter = jnp.exp(m_prev - mm)

            smat = jnp.dot(q, kt.astype(jnp.bfloat16), preferred_element_type=f32) * p
            ct = ct_ref[h]
            nd = (w_inter * jnp.dot(q, ct.astype(jnp.bfloat16), preferred_element_type=f32)
                  + jnp.dot(smat.astype(jnp.bfloat16), v, preferred_element_type=f32))
            den = jnp.sum(jnp.where(lane == M_DH, nd, 0.0), axis=1, keepdims=True)
            hh = jnp.where(lane < M_DH, nd, 0.0) / jnp.maximum(jnp.abs(den), jnp.exp(-(b_col + mm)))
            hh = hh * lax.rsqrt(jnp.sum(hh * hh, axis=1, keepdims=True) * (1.0 / M_DH) + EPS)
            y = hh * hg_ref[:, cols] + skip_ref[:, cols] * cu.astype(f32)
            y = y * _silu(z_ref[0, pl.ds(r0, L), cols].astype(f32))
            o_ref[0, pl.ds(r0, L), cols] = y.astype(o_ref.dtype)

            mm_last = jnp.maximum(m_prev, jnp.max(a_row, axis=1, keepdims=True))
            ws = jnp.exp(a_row - mm_last)
            ct_ref[h] = (jnp.exp(m_prev - mm_last) * ct
                         + jnp.dot((kt * ws).astype(jnp.bfloat16), v, preferred_element_type=f32))
            m_new = jnp.sum(jnp.where(lane8[0:1] == L - 1, b_row, 0.0), axis=1, keepdims=True) + mm_last
            m_ref[h:h + 1, :] = jnp.broadcast_to(m_new, (1, LANES))
        return carry

    lax.fori_loop(0, nchunks, chunk, 0)


def _mlstm(proj, gates, gbias, wq, wk, hn_g, skip):
    bsz, seq, _ = proj.shape
    kern = functools.partial(_mlstm_kernel, seq=seq)
    blk = lambda j: pl.BlockSpec((1, seq, M_WP), lambda b: (b, 0, j))
    full = lambda shape: pl.BlockSpec(shape, lambda b: (0,) * len(shape))
    return pl.pallas_call(
        kern,
        grid=(bsz,),
        in_specs=[blk(OFF_MU // M_WP), blk(OFF_MV // M_WP), blk(OFF_MZ // M_WP),
                  pl.BlockSpec((1, 8, seq), lambda b: (b, 0, 0)),
                  full((8, M_CHUNK)), full((M_HEADS, LANES, LANES)), full((M_HEADS, LANES, LANES)),
                  full((1, M_WP)), full((1, M_WP))],
        out_specs=pl.BlockSpec((1, seq, M_WP), lambda b: (b, 0, 0)),
        out_shape=jax.ShapeDtypeStruct((bsz, seq, M_WP), jnp.bfloat16),
        scratch_shapes=[pltpu.VMEM((M_HEADS, LANES, LANES), jnp.float32),
                        pltpu.VMEM((8, LANES), jnp.float32),
                        pltpu.VMEM((8, seq), jnp.float32),
                        pltpu.VMEM((8, seq), jnp.float32)],
        compiler_params=_params(1),
        name="mlstm",
    )(proj, proj, proj, gates, gbias, wq, wk, hn_g, skip)


def _swa_kernel(q_ref, z_ref, kc_ref, kp_ref, vc_ref, vp_ref, slope_ref, sink_ref, o_ref):
    n = pl.program_id(1)
    f32 = jnp.float32
    W = WINDOW
    qi = lax.broadcasted_iota(jnp.int32, (W, 2 * W), 0)
    si = lax.broadcasted_iota(jnp.int32, (W, 2 * W), 1)
    rel = W + qi - si
    valid = (rel >= 0) & (rel < W) & ((n - 1) * W + si >= 0)
    relf = rel.astype(f32)
    lane = lax.broadcasted_iota(jnp.int32, (W, LANES), 1)
    kk = jnp.concatenate([kp_ref[0], kc_ref[0]], axis=0)
    vv = jnp.concatenate([vp_ref[0], vc_ref[0]], axis=0)
    for g in range(A_G):
        cols = slice(g * LANES, (g + 1) * LANES)
        q = q_ref[0, :, cols]
        outs = []
        for kv in range(A_KV):
            head = kv * A_G + g
            half = (lane >= A_DH) if kv else (lane < A_DH)
            qm = jnp.where(half, q, jnp.zeros_like(q))
            sc = lax.dot_general(qm, kk, _NT, preferred_element_type=f32) * (A_DH ** -0.5)
            sc = jnp.where(valid, sc - slope_ref[head] * relf, -jnp.inf)
            sink = sink_ref[head]
            mx = jnp.maximum(jnp.max(sc, axis=1, keepdims=True), sink)
            e = jnp.exp(sc - mx)
            probs = e / (jnp.sum(e, axis=1, keepdims=True) + jnp.exp(sink - mx))
            outs.append(jnp.dot(probs.astype(jnp.bfloat16), vv, preferred_element_type=f32))
        o = jnp.where(lane < A_DH, outs[0], outs[1])
        o_ref[0, :, cols] = (o * _silu(z_ref[0, :, cols].astype(f32))).astype(o_ref.dtype)


def _swa(proj, slopes, sinks):
    bsz, seq, _ = proj.shape
    nb = seq // WINDOW
    smem = pl.BlockSpec(memory_space=pltpu.SMEM)
    cur = lambda off, w: pl.BlockSpec((1, WINDOW, w), lambda b, n: (b, n, off // w))
    prev = lambda off, w: pl.BlockSpec((1, WINDOW, w), lambda b, n: (b, jnp.maximum(n - 1, 0), off // w))
    return pl.pallas_call(
        _swa_kernel,
        grid=(bsz, nb),
        in_specs=[cur(OFF_AQ, A_W), cur(OFF_AZ, A_W),
                  cur(OFF_AK, A_KVW), prev(OFF_AK, A_KVW), cur(OFF_AV, A_KVW), prev(OFF_AV, A_KVW),
                  smem, smem],
        out_specs=pl.BlockSpec((1, WINDOW, A_W), lambda b, n: (b, n, 0)),
        out_shape=jax.ShapeDtypeStruct((bsz, seq, A_W), jnp.bfloat16),
        compiler_params=_params(2),
        name="swa",
    )(proj, proj, proj, proj, proj, proj, slopes, sinks)


def _sb_kernel(q_ref, z_ref, k_ref, v_ref, o_ref):
    n = pl.program_id(2)
    f32 = jnp.float32
    T = S_BLOCK
    lane = lax.broadcasted_iota(jnp.int32, (T, LANES), 1)
    ti = lax.broadcasted_iota(jnp.int32, (T, T), 0)
    si = lax.broadcasted_iota(jnp.int32, (T, T), 1)
    strict = si < ti
    jj = lax.broadcasted_iota(jnp.int32, (T, 2 * T), 0)
    ss = lax.broadcasted_iota(jnp.int32, (T, 2 * T), 1)
    usum = jnp.where((jj > ss) | (ss >= T), 1.0, 0.0).astype(jnp.bfloat16)

    q = q_ref[0]
    qms = [jnp.where((lane >= S_DH) if hh else (lane < S_DH), q, jnp.zeros_like(q)) for hh in range(2)]

    def tile(j, carry, diag):
        r0 = pl.multiple_of(j * T, T)
        kb = k_ref[0, pl.ds(r0, T), :]
        vb = v_ref[0, pl.ds(r0, T), :]
        new = []
        for hh in range(2):
            acc, run = carry[hh]
            zz = lax.dot_general(qms[hh], kb, _NT, preferred_element_type=f32) * (S_DH ** -0.5)
            ls = _log_sigmoid(zz)
            lk = ls - zz
            if diag:
                lk = jnp.where(strict, lk, 0.0)
            hi = lk.astype(jnp.bfloat16)
            lo = (lk - hi.astype(f32)).astype(jnp.bfloat16)
            cs = (jnp.dot(hi, usum, preferred_element_type=f32)
                  + jnp.dot(lo, usum, preferred_element_type=f32))
            a = jnp.exp(ls + cs[:, :T] + run)
            if diag:
                a = jnp.where(strict, a, 0.0)
            acc = acc + jnp.dot(a.astype(jnp.bfloat16), vb, preferred_element_type=f32)
            new.append((acc, run + cs[:, T:]))
        return tuple(new)

    zero = jnp.zeros((T, LANES), f32)
    carry = tile(n, ((zero, zero), (zero, zero)), True)
    carry = lax.fori_loop(0, n, lambda i, c: tile(n - 1 - i, c, False), carry)
    o = jnp.where(lane < S_DH, carry[0][0], carry[1][0])
    o_ref[0] = (o * _silu(z_ref[0].astype(f32))).astype(o_ref.dtype)


def _stickbreak(proj):
    bsz, seq, _ = proj.shape
    nb = seq // S_BLOCK
    npair = S_W // LANES
    qblk = lambda off: pl.BlockSpec((1, S_BLOCK, LANES), lambda b, p, n: (b, n, off // LANES + p))
    full = lambda off: pl.BlockSpec((1, seq, LANES), lambda b, p, n: (b, 0, off // LANES + p))
    return pl.pallas_call(
        _sb_kernel,
        grid=(bsz, npair, nb),
        in_specs=[qblk(OFF_SQ), qblk(OFF_SZ), full(OFF_SK), full(OFF_SV)],
        out_specs=pl.BlockSpec((1, S_BLOCK, LANES), lambda b, p, n: (b, n, p)),
        out_shape=jax.ShapeDtypeStruct((bsz, seq, S_W), jnp.bfloat16),
        compiler_params=_params(3),
        name="stickbreak",
    )(proj, proj, proj, proj)


def _outproj_kernel(x_ref, ym_ref, ya_ref, ys_ref, wm_ref, wa_ref, ws_ref, g_ref, gate_ref, o_ref):
    f32 = jnp.float32
    y = (jnp.dot(ym_ref[0], wm_ref[...], preferred_element_type=f32)
         + jnp.dot(ya_ref[0], wa_ref[...], preferred_element_type=f32)
         + jnp.dot(ys_ref[0], ws_ref[...], preferred_element_type=f32))
    yn = y * lax.rsqrt(jnp.mean(y * y, axis=-1, keepdims=True) + EPS) * g_ref[...]
    o_ref[0] = x_ref[0] + gate_ref[...] * yn


def _outproj(x, ym, ya, ys, wm, wa, ws, g_post, mod4, layer):
    bsz, seq, d = x.shape
    tm = 512
    row = lambda w: pl.BlockSpec((1, tm, w), lambda b, s: (b, s, 0))
    full = lambda shape: pl.BlockSpec(shape, lambda b, s: (0,) * len(shape))
    return pl.pallas_call(
        _outproj_kernel,
        grid=(bsz, seq // tm),
        in_specs=[row(d), row(M_WP), row(A_W), row(S_W),
                  full((M_WP, d)), full((A_W, d)), full((S_W, d)), full((1, d)),
                  pl.BlockSpec((None, None, 1, d), lambda b, s: (layer, b, 0, 2))],
        out_specs=row(d),
        out_shape=jax.ShapeDtypeStruct((bsz, seq, d), jnp.float32),
        compiler_params=_params(2),
        name="outproj",
    )(x, ym, ya, ys, wm, wa, ws, g_post, mod4)


def _pad_heads(w, axis):
    shape = list(w.shape)
    shape[axis:axis + 1] = [M_HEADS, M_DH]
    w = w.reshape(shape)
    pad = [(0, 0)] * w.ndim
    pad[axis + 1] = (0, LANES - M_DH)
    w = jnp.pad(w, pad)
    shape[axis:axis + 2] = [M_WP]
    return w.reshape(shape)


def _pair_heads(w, axis):
    shape = list(w.shape)
    shape[axis:axis + 1] = [A_KV, A_G, A_DH]
    w = jnp.swapaxes(w.reshape(shape), axis, axis + 1)
    shape[axis:axis + 3] = [A_W]
    return w.reshape(shape)


def _pack_w_in(w):
    o = np.cumsum([0, M_W, M_W, M_HEADS, M_HEADS, M_W, A_W, A_KVW, A_KVW, A_W, S_W, S_W, S_W, S_W])
    seg = lambda i: w[:, int(o[i]):int(o[i + 1])]
    packed = jnp.concatenate(
        [_pad_heads(seg(0), 1), _pad_heads(seg(1), 1), _pad_heads(seg(4), 1),
         _pair_heads(seg(5), 1), _pair_heads(seg(8), 1), seg(6), seg(7),
         seg(9), seg(10), seg(11), seg(12)], axis=1).astype(jnp.bfloat16)
    gates_t = jnp.concatenate([seg(2), seg(3)], axis=1).T.astype(jnp.bfloat16)
    return packed, gates_t


def _pack_w_out(w):
    wm = _pad_heads(w[:M_W], 0)
    wa = _pair_heads(w[M_W:M_W + A_W], 0)
    ws = w[M_W + A_W:]
    return wm.astype(jnp.bfloat16), wa.astype(jnp.bfloat16), ws.astype(jnp.bfloat16)


def _pad_qk(w):
    return jnp.pad(w, ((0, 0), (0, LANES - M_DH), (0, LANES - M_DH))).astype(jnp.bfloat16)


def kernel(x, c, w_mod, b_mod, g_pre, g_post, w_in, m_conv_w, m_conv_b, m_wq, m_wk, m_b_i, m_b_f,
           m_norm_g, m_skip, a_sinks, w_out):
    bsz = x.shape[0]
    mod4 = _modulation(c, w_mod, b_mod).reshape(DEPTH, bsz, 1, 3 * D_MODEL)
    slopes = jnp.asarray(2.0 ** (-8.0 * np.arange(1, A_HEADS + 1) / A_HEADS), dtype=jnp.float32)
    for l in range(DEPTH):
        w_pack, wg_t = _pack_w_in(w_in[l])
        wm, wa, ws = _pack_w_out(w_out[l])
        gbias = jnp.broadcast_to(jnp.concatenate([m_b_i[l], m_b_f[l]])[:, None], (2 * M_HEADS, M_CHUNK))
        proj, gates = _inproj(x, mod4, l, g_pre[l][None], w_pack, wg_t,
                              _pad_heads(m_conv_w[l], 1), _pad_heads(m_conv_b[l][None], 1))
        ym = _mlstm(proj, gates, gbias, _pad_qk(m_wq[l]), _pad_qk(m_wk[l]),
                    _pad_heads(m_norm_g[l][None], 1), _pad_heads(m_skip[l][None], 1))
        ya = _swa(proj, slopes, a_sinks[l])
        ys = _stickbreak(proj)
        x = _outproj(x, ym, ya, ys, wm, wa, ws, g_post[l][None], mod4, l)
    return x
```

```python
import functools

import jax
import jax.numpy as jnp
import numpy as np
from jax import lax
from jax.experimental import pallas as pl
from jax.experimental.pallas import tpu as pltpu

D_MODEL = 1024
DEPTH = 2
M_HEADS = 4
M_DH = 96
M_W = M_HEADS * M_DH
M_CONV = 4
A_HEADS = 6
A_KV = 2
A_G = A_HEADS // A_KV
A_DH = 64
A_W = A_HEADS * A_DH
A_KVW = A_KV * A_DH
WINDOW = 128
S_HEADS = 4
S_DH = 64
S_W = S_HEADS * S_DH
S_BLOCK = 128
EPS = 1e-6

LANES = 128
M_CHUNK = 128
M_WP = M_HEADS * LANES
SB_TILE = 256
S_PAIRS = S_W // LANES

OFF_MU, OFF_MV, OFF_MZ = 0, M_WP, 2 * M_WP
OFF_AQ = 3 * M_WP
OFF_AZ = OFF_AQ + A_W
OFF_AK = OFF_AZ + A_W
OFF_AV = OFF_AK + A_KVW
OFF_SQ = OFF_AV + A_KVW
OFF_SK = OFF_SQ + S_W
OFF_SV = OFF_SK + S_W
OFF_SZ = OFF_SV + S_W
N_PACK = OFF_SZ + S_W

VMEM_LIMIT = 48 * 1024 * 1024

_NT = (((1,), (1,)), ((), ()))
LOG2E = 1.4426950408889634


def _log_sigmoid(x):
    return jnp.minimum(x, 0.0) - jnp.log(1.0 + jnp.exp(-jnp.abs(x)))


def _silu(x):
    return x / (1.0 + jnp.exp(-x))


def _params(n_axes):
    return pltpu.CompilerParams(dimension_semantics=("arbitrary",) * n_axes, vmem_limit_bytes=VMEM_LIMIT)


def _mod_kernel(c_ref, w_ref, b_ref, o_ref):
    c_act = _silu(c_ref[...]).astype(jnp.bfloat16)
    o_ref[0] = jnp.dot(c_act, w_ref[0].astype(jnp.bfloat16), preferred_element_type=jnp.float32) + b_ref[0]


def _modulation(c, w_mod, b_mod):
    depth, d, n = w_mod.shape
    bsz = c.shape[0]
    tn = 1024
    return pl.pallas_call(
        _mod_kernel,
        grid=(depth, n // tn),
        in_specs=[pl.BlockSpec((bsz, d), lambda l, j: (0, 0)),
                  pl.BlockSpec((1, d, tn), lambda l, j: (l, 0, j)),
                  pl.BlockSpec((1, 1, tn), lambda l, j: (l, 0, j))],
        out_specs=pl.BlockSpec((1, bsz, tn), lambda l, j: (l, 0, j)),
        out_shape=jax.ShapeDtypeStruct((depth, bsz, n), jnp.float32),
        compiler_params=_params(2),
        name="modulation",
    )(c, w_mod, b_mod.reshape(depth, 1, n))


def _inproj_kernel(x_ref, shift_ref, scale_ref, g_ref, w_ref, wg_ref, cw_ref, cb_ref,
                   proj_ref, gates_ref, conv_ref, *, tm, nchunk):
    s = pl.program_id(1)
    x = x_ref[0]
    ms = jnp.mean(x * x, axis=-1, keepdims=True)
    h = x * lax.rsqrt(ms + EPS) * g_ref[...]
    h = (h * (1.0 + scale_ref[...]) + shift_ref[...]).astype(jnp.bfloat16)

    gates_ref[0] = lax.dot_general(wg_ref[...], h, _NT, preferred_element_type=jnp.float32)

    @pl.when(s == 0)
    def _():
        conv_ref[0:8, :] = jnp.zeros((8, M_WP), jnp.float32)

    conv_ref[8:8 + tm, :] = jnp.dot(h, w_ref[:, OFF_MU:OFF_MU + M_WP], preferred_element_type=jnp.float32)
    acc = cb_ref[...] + cw_ref[M_CONV - 1:M_CONV, :] * conv_ref[8:8 + tm, :]
    for j in range(M_CONV - 1):
        lag = M_CONV - 1 - j
        acc = acc + cw_ref[j:j + 1, :] * conv_ref[8 - lag:8 - lag + tm, :]
    proj_ref[0, :, OFF_MU:OFF_MU + M_WP] = _silu(acc).astype(jnp.bfloat16)
    conv_ref[0:8, :] = conv_ref[tm:tm + 8, :]

    for c0 in range(OFF_MV, N_PACK, nchunk):
        proj_ref[0, :, c0:c0 + nchunk] = jnp.dot(
            h, w_ref[:, c0:c0 + nchunk], preferred_element_type=jnp.float32).astype(jnp.bfloat16)


def _inproj(x, mod4, layer, g_pre, w_pack, wg_t, conv_w, conv_b):
    bsz, seq, d = x.shape
    tm = 512
    kern = functools.partial(_inproj_kernel, tm=tm, nchunk=512)
    return pl.pallas_call(
        kern,
        grid=(bsz, seq // tm),
        in_specs=[pl.BlockSpec((1, tm, d), lambda b, s: (b, s, 0)),
                  pl.BlockSpec((None, None, 1, d), lambda b, s: (layer, b, 0, 0)),
                  pl.BlockSpec((None, None, 1, d), lambda b, s: (layer, b, 0, 1)),
                  pl.BlockSpec((1, d), lambda b, s: (0, 0)),
                  pl.BlockSpec((d, N_PACK), lambda b, s: (0, 0)),
                  pl.BlockSpec((8, d), lambda b, s: (0, 0)),
                  pl.BlockSpec((M_CONV, M_WP), lambda b, s: (0, 0)),
                  pl.BlockSpec((1, M_WP), lambda b, s: (0, 0))],
        out_specs=[pl.BlockSpec((1, tm, N_PACK), lambda b, s: (b, s, 0)),
                   pl.BlockSpec((1, 8, tm), lambda b, s: (b, 0, s))],
        out_shape=[jax.ShapeDtypeStruct((bsz, seq, N_PACK), jnp.bfloat16),
                   jax.ShapeDtypeStruct((bsz, 8, seq), jnp.float32)],
        scratch_shapes=[pltpu.VMEM((tm + 8, M_WP), jnp.float32)],
        compiler_params=_params(2),
        name="inproj",
    )(x, mod4, mod4, g_pre, w_pack, wg_t, conv_w, conv_b)


def _mlstm_kernel(cu_ref, v_ref, z_ref, gates_ref, gbias_ref, wq_ref, wk_ref, hg_ref, skip_ref,
                  o_ref, ct_ref, m_ref, a_ref, b_ref, *, seq):
    L = M_CHUNK
    nchunks = seq // L
    f32 = jnp.float32
    lane8 = lax.broadcasted_iota(jnp.int32, (8, L), 1)
    row8 = lax.broadcasted_iota(jnp.int32, (8, L), 0)

    for c in range(nchunks):
        g = gates_ref[0, :, c * L:(c + 1) * L] + gbias_ref[...]
        cum = jnp.where(row8 >= M_HEADS, _log_sigmoid(g), 0.0)
        k = 1
        while k < L:
            cum = cum + jnp.where(lane8 >= k, pltpu.roll(cum, k, axis=1), 0.0)
            k *= 2
        a_ref[:, c * L:(c + 1) * L] = g - pltpu.roll(cum, M_HEADS, axis=0)
        b_ref[:, c * L:(c + 1) * L] = cum

    ct_ref[...] = jnp.zeros(ct_ref.shape, f32)
    m_ref[...] = jnp.zeros(m_ref.shape, f32)

    t_idx = lax.broadcasted_iota(jnp.int32, (L, L), 0)
    s_idx = lax.broadcasted_iota(jnp.int32, (L, L), 1)
    tri = s_idx <= t_idx
    eye = s_idx == t_idx
    lane = lax.broadcasted_iota(jnp.int32, (L, LANES), 1)
    scale = M_DH ** -0.5

    def chunk(c, carry):
        r0 = pl.multiple_of(c * L, L)
        for h in range(M_HEADS):
            cols = slice(h * LANES, (h + 1) * LANES)
            cu = cu_ref[0, pl.ds(r0, L), cols]
            q = (jnp.dot(cu, wq_ref[h], preferred_element_type=f32) * scale).astype(jnp.bfloat16)
            kt = jnp.dot(cu, wk_ref[h], preferred_element_type=f32).T
            v = v_ref[0, pl.ds(r0, L), cols]
            v = jnp.where(lane == M_DH, 1.0, v.astype(f32)).astype(jnp.bfloat16)

            a_row = a_ref[h:h + 1, pl.ds(r0, L)]
            b_row = b_ref[M_HEADS + h:M_HEADS + h + 1, pl.ds(r0, L)]
            m_prev = m_ref[h:h + 1, 0:1]
            a_mat = jnp.broadcast_to(a_row, (L, L))
            m1 = jnp.max(jnp.where(tri, a_mat, -jnp.inf), axis=1, keepdims=True)
            b_col = jnp.sum(jnp.where(eye, jnp.broadcast_to(b_row, (L, L)), 0.0), axis=1, keepdims=True)
            mm = jnp.maximum(m_prev, m1)
            p = jnp.where(tri, jnp.exp(a_mat - mm), 0.0)
            w_inter = jnp.exp(m_prev - mm)

            smat = jnp.dot(q, kt.astype(jnp.bfloat16), preferred_element_type=f32) * p
            ct = ct_ref[h]
            nd = (w_inter * jnp.dot(q, ct.astype(jnp.bfloat16), preferred_element_type=f32)
                  + jnp.dot(smat.astype(jnp.bfloat16), v, preferred_element_type=f32))
            den = jnp.sum(jnp.where(lane == M_DH, nd, 0.0), axis=1, keepdims=True)
            hh = jnp.where(lane < M_DH, nd, 0.0) / jnp.maximum(jnp.abs(den), jnp.exp(-(b_col + mm)))
            hh = hh * lax.rsqrt(jnp.sum(hh * hh, axis=1, keepdims=True) * (1.0 / M_DH) + EPS)
            y = hh * hg_ref[:, cols] + skip_ref[:, cols] * cu.astype(f32)
            y = y * _silu(z_ref[0, pl.ds(r0, L), cols].astype(f32))
            o_ref[0, pl.ds(r0, L), cols] = y.astype(o_ref.dtype)

            mm_last = jnp.maximum(m_prev, jnp.max(a_row, axis=1, keepdims=True))
            ws = jnp.exp(a_row - mm_last)
            ct_ref[h] = (jnp.exp(m_prev - mm_last) * ct
                         + jnp.dot((kt * ws).astype(jnp.bfloat16), v, preferred_element_type=f32))
            m_new = jnp.sum(jnp.where(lane8[0:1] == L - 1, b_row, 0.0), axis=1, keepdims=True) + mm_last
            m_ref[h:h + 1, :] = jnp.broadcast_to(m_new, (1, LANES))
        return carry

    lax.fori_loop(0, nchunks, chunk, 0)


def _mlstm(proj, gates, gbias, wq, wk, hn_g, skip):
    bsz, seq, _ = proj.shape
    kern = functools.partial(_mlstm_kernel, seq=seq)
    blk = lambda j: pl.BlockSpec((1, seq, M_WP), lambda b: (b, 0, j))
    full = lambda shape: pl.BlockSpec(shape, lambda b: (0,) * len(shape))
    return pl.pallas_call(
        kern,
        grid=(bsz,),
        in_specs=[blk(OFF_MU // M_WP), blk(OFF_MV // M_WP), blk(OFF_MZ // M_WP),
                  pl.BlockSpec((1, 8, seq), lambda b: (b, 0, 0)),
                  full((8, M_CHUNK)), full((M_HEADS, LANES, LANES)), full((M_HEADS, LANES, LANES)),
                  full((1, M_WP)), full((1, M_WP))],
        out_specs=pl.BlockSpec((1, seq, M_WP), lambda b: (b, 0, 0)),
        out_shape=jax.ShapeDtypeStruct((bsz, seq, M_WP), jnp.bfloat16),
        scratch_shapes=[pltpu.VMEM((M_HEADS, LANES, LANES), jnp.float32),
                        pltpu.VMEM((8, LANES), jnp.float32),
                        pltpu.VMEM((8, seq), jnp.float32),
                        pltpu.VMEM((8, seq), jnp.float32)],
        compiler_params=_params(1),
        name="mlstm",
    )(proj, proj, proj, gates, gbias, wq, wk, hn_g, skip)


def _swa_kernel(q_ref, z_ref, kc_ref, kp_ref, vc_ref, vp_ref, slope_ref, sink_ref, o_ref):
    n = pl.program_id(1)
    f32 = jnp.float32
    W = WINDOW
    qi = lax.broadcasted_iota(jnp.int32, (W, 2 * W), 0)
    si = lax.broadcasted_iota(jnp.int32, (W, 2 * W), 1)
    rel = W + qi - si
    valid = (rel >= 0) & (rel < W) & ((n - 1) * W + si >= 0)
    relf = rel.astype(f32)
    lane = lax.broadcasted_iota(jnp.int32, (W, LANES), 1)
    kk = jnp.concatenate([kp_ref[0], kc_ref[0]], axis=0)
    vv = jnp.concatenate([vp_ref[0], vc_ref[0]], axis=0)
    for g in range(A_G):
        cols = slice(g * LANES, (g + 1) * LANES)
        q = q_ref[0, :, cols]
        outs = []
        for kv in range(A_KV):
            head = kv * A_G + g
            half = (lane >= A_DH) if kv else (lane < A_DH)
            qm = jnp.where(half, q, jnp.zeros_like(q))
            sc = lax.dot_general(qm, kk, _NT, preferred_element_type=f32) * (A_DH ** -0.5)
            sc = jnp.where(valid, sc - slope_ref[head] * relf, -jnp.inf)
            sink = sink_ref[head]
            mx = jnp.maximum(jnp.max(sc, axis=1, keepdims=True), sink)
            e = jnp.exp(sc - mx)
            probs = e / (jnp.sum(e, axis=1, keepdims=True) + jnp.exp(sink - mx))
            outs.append(jnp.dot(probs.astype(jnp.bfloat16), vv, preferred_element_type=f32))
        o = jnp.where(lane < A_DH, outs[0], outs[1])
        o_ref[0, :, cols] = (o * _silu(z_ref[0, :, cols].astype(f32))).astype(o_ref.dtype)


def _swa(proj, slopes, sinks):
    bsz, seq, _ = proj.shape
    nb = seq // WINDOW
    smem = pl.BlockSpec(memory_space=pltpu.SMEM)
    cur = lambda off, w: pl.BlockSpec((1, WINDOW, w), lambda b, n: (b, n, off // w))
    prev = lambda off, w: pl.BlockSpec((1, WINDOW, w), lambda b, n: (b, jnp.maximum(n - 1, 0), off // w))
    return pl.pallas_call(
        _swa_kernel,
        grid=(bsz, nb),
        in_specs=[cur(OFF_AQ, A_W), cur(OFF_AZ, A_W),
                  cur(OFF_AK, A_KVW), prev(OFF_AK, A_KVW), cur(OFF_AV, A_KVW), prev(OFF_AV, A_KVW),
                  smem, smem],
        out_specs=pl.BlockSpec((1, WINDOW, A_W), lambda b, n: (b, n, 0)),
        out_shape=jax.ShapeDtypeStruct((bsz, seq, A_W), jnp.bfloat16),
        compiler_params=_params(2),
        name="swa",
    )(proj, proj, proj, proj, proj, proj, slopes, sinks)


def _sb_kernel(q_ref, z_ref, k_ref, v_ref, o_ref, acc_ref, run_ref, *, seq):
    f32 = jnp.float32
    T = SB_TILE
    lane = lax.broadcasted_iota(jnp.int32, (T, LANES), 1)
    halves = (lane < S_DH, lane >= S_DH)
    ti = lax.broadcasted_iota(jnp.int32, (T, T), 0)
    si = lax.broadcasted_iota(jnp.int32, (T, T), 1)
    strict = si < ti
    usum = jnp.where(ti > si, 1.0, 0.0).astype(jnp.bfloat16)

    def qblock(n, carry):
        r0 = pl.multiple_of(n * T, T)
        q = q_ref[0, pl.ds(r0, T), :]
        q = q * jnp.asarray(S_DH ** -0.5, q.dtype)
        acc_ref[...] = jnp.zeros(acc_ref.shape, f32)
        run_ref[...] = jnp.zeros(run_ref.shape, f32)

        def tile(j, diag):
            k0 = pl.multiple_of(j * T, T)
            kb = k_ref[0, pl.ds(k0, T), :]
            vb = v_ref[0, pl.ds(k0, T), :]
            heads = range(S_HEADS)
            cols = [slice((h // 2) * LANES, (h // 2 + 1) * LANES) for h in heads]
            zz = [lax.dot_general(jnp.where(halves[h % 2], q[:, cols[h]], jnp.zeros((T, LANES), q.dtype)),
                                  kb[:, cols[h]], _NT, preferred_element_type=f32) for h in heads]
            ls, lk = [], []
            for h in heads:
                z2 = zz[h] * LOG2E
                soft = jnp.log2(1.0 + jnp.exp2(-jnp.abs(z2)))
                ls.append(jnp.minimum(z2, 0.0) - soft)
                lkh = ls[h] - z2
                lk.append(jnp.where(strict, lkh, 0.0) if diag else lkh)
            suf = [jnp.dot(lk[h].astype(jnp.bfloat16), usum, preferred_element_type=f32) for h in heads]
            a = []
            for h in heads:
                run = run_ref[h]
                ah = jnp.exp2(ls[h] + suf[h] + jnp.concatenate([run] * (T // LANES), axis=1))
                a.append((jnp.where(strict, ah, 0.0) if diag else ah).astype(jnp.bfloat16))
                run_ref[h] = run + jnp.sum(lk[h], axis=1, keepdims=True)
            for p in range(S_PAIRS):
                pv = None
                for h in (2 * p, 2 * p + 1):
                    vm = jnp.where(halves[h % 2], vb[:, cols[h]], jnp.zeros((T, LANES), vb.dtype))
                    d = jnp.dot(a[h], vm, preferred_element_type=f32)
                    pv = d if pv is None else pv + d
                acc_ref[:, cols[2 * p]] += pv

        tile(n, True)

        def body(i, c):
            tile(n - 1 - i, False)
            return c

        lax.fori_loop(0, n, body, 0)
        zg = z_ref[0, pl.ds(r0, T), :].astype(f32)
        o_ref[0, pl.ds(r0, T), :] = (acc_ref[...] * _silu(zg)).astype(o_ref.dtype)
        return carry

    lax.fori_loop(0, seq // T, qblock, 0)


def _stickbreak(proj):
    bsz, seq, _ = proj.shape
    kern = functools.partial(_sb_kernel, seq=seq)
    full = lambda off: pl.BlockSpec((1, seq, S_W), lambda b: (b, 0, off // S_W))
    return pl.pallas_call(
        kern,
        grid=(bsz,),
        in_specs=[full(OFF_SQ), full(OFF_SZ), full(OFF_SK), full(OFF_SV)],
        out_specs=pl.BlockSpec((1, seq, S_W), lambda b: (b, 0, 0)),
        out_shape=jax.ShapeDtypeStruct((bsz, seq, S_W), jnp.bfloat16),
        scratch_shapes=[pltpu.VMEM((SB_TILE, S_W), jnp.float32),
                        pltpu.VMEM((S_HEADS, SB_TILE, LANES), jnp.float32)],
        compiler_params=_params(1),
        name="stickbreak",
    )(proj, proj, proj, proj)


def _outproj_kernel(x_ref, ym_ref, ya_ref, ys_ref, wm_ref, wa_ref, ws_ref, g_ref, gate_ref, o_ref):
    f32 = jnp.float32
    y = (jnp.dot(ym_ref[0], wm_ref[...], preferred_element_type=f32)
         + jnp.dot(ya_ref[0], wa_ref[...], preferred_element_type=f32)
         + jnp.dot(ys_ref[0], ws_ref[...], preferred_element_type=f32))
    yn = y * lax.rsqrt(jnp.mean(y * y, axis=-1, keepdims=True) + EPS) * g_ref[...]
    o_ref[0] = x_ref[0] + gate_ref[...] * yn


def _outproj(x, ym, ya, ys, wm, wa, ws, g_post, mod4, layer):
    bsz, seq, d = x.shape
    tm = 512
    row = lambda w: pl.BlockSpec((1, tm, w), lambda b, s: (b, s, 0))
    full = lambda shape: pl.BlockSpec(shape, lambda b, s: (0,) * len(shape))
    return pl.pallas_call(
        _outproj_kernel,
        grid=(bsz, seq // tm),
        in_specs=[row(d), row(M_WP), row(A_W), row(S_W),
                  full((M_WP, d)), full((A_W, d)), full((S_W, d)), full((1, d)),
                  pl.BlockSpec((None, None, 1, d), lambda b, s: (layer, b, 0, 2))],
        out_specs=row(d),
        out_shape=jax.ShapeDtypeStruct((bsz, seq, d), jnp.float32),
        compiler_params=_params(2),
        name="outproj",
    )(x, ym, ya, ys, wm, wa, ws, g_post, mod4)


def _pad_heads(w, axis):
    shape = list(w.shape)
    shape[axis:axis + 1] = [M_HEADS, M_DH]
    w = w.reshape(shape)
    pad = [(0, 0)] * w.ndim
    pad[axis + 1] = (0, LANES - M_DH)
    w = jnp.pad(w, pad)
    shape[axis:axis + 2] = [M_WP]
    return w.reshape(shape)


def _pair_heads(w, axis):
    shape = list(w.shape)
    shape[axis:axis + 1] = [A_KV, A_G, A_DH]
    w = jnp.swapaxes(w.reshape(shape), axis, axis + 1)
    shape[axis:axis + 3] = [A_W]
    return w.reshape(shape)


def _pack_w_in(w):
    o = np.cumsum([0, M_W, M_W, M_HEADS, M_HEADS, M_W, A_W, A_KVW, A_KVW, A_W, S_W, S_W, S_W, S_W])
    seg = lambda i: w[:, int(o[i]):int(o[i + 1])]
    packed = jnp.concatenate(
        [_pad_heads(seg(0), 1), _pad_heads(seg(1), 1), _pad_heads(seg(4), 1),
         _pair_heads(seg(5), 1), _pair_heads(seg(8), 1), seg(6), seg(7),
         seg(9), seg(10), seg(11), seg(12)], axis=1).astype(jnp.bfloat16)
    gates_t = jnp.concatenate([seg(2), seg(3)], axis=1).T.astype(jnp.bfloat16)
    return packed, gates_t


def _pack_w_out(w):
    wm = _pad_heads(w[:M_W], 0)
    wa = _pair_heads(w[M_W:M_W + A_W], 0)
    ws = w[M_W + A_W:]
    return wm.astype(jnp.bfloat16), wa.astype(jnp.bfloat16), ws.astype(jnp.bfloat16)


def _pad_qk(w):
    return jnp.pad(w, ((0, 0), (0, LANES - M_DH), (0, LANES - M_DH))).astype(jnp.bfloat16)


def kernel(x, c, w_mod, b_mod, g_pre, g_post, w_in, m_conv_w, m_conv_b, m_wq, m_wk, m_b_i, m_b_f,
           m_norm_g, m_skip, a_sinks, w_out):
    bsz = x.shape[0]
    mod4 = _modulation(c, w_mod, b_mod).reshape(DEPTH, bsz, 1, 3 * D_MODEL)
    slopes = jnp.asarray(2.0 ** (-8.0 * np.arange(1, A_HEADS + 1) / A_HEADS), dtype=jnp.float32)
    for l in range(DEPTH):
        w_pack, wg_t = _pack_w_in(w_in[l])
        wm, wa, ws = _pack_w_out(w_out[l])
        gbias = jnp.broadcast_to(jnp.concatenate([m_b_i[l], m_b_f[l]])[:, None], (2 * M_HEADS, M_CHUNK))
        proj, gates = _inproj(x, mod4, l, g_pre[l][None], w_pack, wg_t,
                              _pad_heads(m_conv_w[l], 1), _pad_heads(m_conv_b[l][None], 1))
        ym = _mlstm(proj, gates, gbias, _pad_qk(m_wq[l]), _pad_qk(m_wk[l]),
                    _pad_heads(m_norm_g[l][None], 1), _pad_heads(m_skip[l][None], 1))
        ya = _swa(proj, slopes, a_sinks[l])
        ys = _stickbreak(proj)
        x = _outproj(x, ym, ya, ys, wm, wa, ws, g_post[l][None], mod4, l)
    return x
```

```python
import functools

import jax
import jax.numpy as jnp
import numpy as np
from jax import lax
from jax.experimental import pallas as pl
from jax.experimental.pallas import tpu as pltpu

D_MODEL = 1024
DEPTH = 2
M_HEADS = 4
M_DH = 96
M_W = M_HEADS * M_DH
M_CONV = 4
A_HEADS = 6
A_KV = 2
A_G = A_HEADS // A_KV
A_DH = 64
A_W = A_HEADS * A_DH
A_KVW = A_KV * A_DH
WINDOW = 128
S_HEADS = 4
S_DH = 64
S_W = S_HEADS * S_DH
S_BLOCK = 128
EPS = 1e-6

LANES = 128
M_CHUNK = 128
M_WP = M_HEADS * LANES
SB_TILE = 256
S_PAIRS = S_W // LANES

OFF_MU, OFF_MV, OFF_MZ = 0, M_WP, 2 * M_WP
OFF_AQ = 3 * M_WP
OFF_AZ = OFF_AQ + A_W
OFF_AK = OFF_AZ + A_W
OFF_AV = OFF_AK + A_KVW
OFF_SQ = OFF_AV + A_KVW
OFF_SK = OFF_SQ + S_W
OFF_SV = OFF_SK + S_W
OFF_SZ = OFF_SV + S_W
N_PACK = OFF_SZ + S_W

VMEM_LIMIT = 48 * 1024 * 1024

_NT = (((1,), (1,)), ((), ()))
LOG2E = 1.4426950408889634


def _log_sigmoid(x):
    return jnp.minimum(x, 0.0) - jnp.log(1.0 + jnp.exp(-jnp.abs(x)))


def _silu(x):
    return x / (1.0 + jnp.exp(-x))


def _params(n_axes):
    return pltpu.CompilerParams(dimension_semantics=("arbitrary",) * n_axes, vmem_limit_bytes=VMEM_LIMIT)


def _mod_kernel(c_ref, w_ref, b_ref, o_ref):
    c_act = _silu(c_ref[...]).astype(jnp.bfloat16)
    o_ref[0] = jnp.dot(c_act, w_ref[0].astype(jnp.bfloat16), preferred_element_type=jnp.float32) + b_ref[0]


def _modulation(c, w_mod, b_mod):
    depth, d, n = w_mod.shape
    bsz = c.shape[0]
    tn = 1024
    return pl.pallas_call(
        _mod_kernel,
        grid=(depth, n // tn),
        in_specs=[pl.BlockSpec((bsz, d), lambda l, j: (0, 0)),
                  pl.BlockSpec((1, d, tn), lambda l, j: (l, 0, j)),
                  pl.BlockSpec((1, 1, tn), lambda l, j: (l, 0, j))],
        out_specs=pl.BlockSpec((1, bsz, tn), lambda l, j: (l, 0, j)),
        out_shape=jax.ShapeDtypeStruct((depth, bsz, n), jnp.float32),
        compiler_params=_params(2),
        name="modulation",
    )(c, w_mod, b_mod.reshape(depth, 1, n))


def _inproj_kernel(x_ref, shift_ref, scale_ref, g_ref, w_ref, wg_ref, cw_ref, cb_ref,
                   proj_ref, gates_ref, conv_ref, *, tm, sub, nchunk):
    s = pl.program_id(1)
    f32 = jnp.float32

    @pl.when(s == 0)
    def _():
        conv_ref[0:8, :] = jnp.zeros((8, M_WP), f32)

    subs = [slice(i * sub, (i + 1) * sub) for i in range(tm // sub)]
    hs = []
    for r in subs:
        x = x_ref[0, r, :]
        ms = jnp.mean(x * x, axis=-1, keepdims=True)
        h = x * lax.rsqrt(ms + EPS) * g_ref[...]
        hs.append((h * (1.0 + scale_ref[...]) + shift_ref[...]).astype(jnp.bfloat16))

    for i, (r, h) in enumerate(zip(subs, hs)):
        gates_ref[0, :, r] = lax.dot_general(wg_ref[...], h, _NT, preferred_element_type=f32)

        base = 8 + i * sub
        conv_ref[base:base + sub, :] = jnp.dot(h, w_ref[:, OFF_MU:OFF_MU + M_WP], preferred_element_type=f32)
        acc = cb_ref[...] + cw_ref[M_CONV - 1:M_CONV, :] * conv_ref[base:base + sub, :]
        for j in range(M_CONV - 1):
            lag = M_CONV - 1 - j
            acc = acc + cw_ref[j:j + 1, :] * conv_ref[base - lag:base - lag + sub, :]
        proj_ref[0, r, OFF_MU:OFF_MU + M_WP] = _silu(acc).astype(jnp.bfloat16)

        for c0 in range(OFF_MV, N_PACK, nchunk):
            proj_ref[0, r, c0:c0 + nchunk] = jnp.dot(
                h, w_ref[:, c0:c0 + nchunk], preferred_element_type=f32).astype(jnp.bfloat16)

    conv_ref[0:8, :] = conv_ref[tm:tm + 8, :]


def _inproj(x, mod4, layer, g_pre, w_pack, wg_t, conv_w, conv_b):
    bsz, seq, d = x.shape
    tm = 1024
    kern = functools.partial(_inproj_kernel, tm=tm, sub=512, nchunk=512)
    const = lambda shape: pl.BlockSpec(shape, lambda b, s: (0,) * len(shape), pipeline_mode=pl.Buffered(1))
    return pl.pallas_call(
        kern,
        grid=(bsz, seq // tm),
        in_specs=[pl.BlockSpec((1, tm, d), lambda b, s: (b, s, 0)),
                  pl.BlockSpec((None, None, 1, d), lambda b, s: (layer, b, 0, 0)),
                  pl.BlockSpec((None, None, 1, d), lambda b, s: (layer, b, 0, 1)),
                  const((1, d)), const((d, N_PACK)), const((8, d)), const((M_CONV, M_WP)), const((1, M_WP))],
        out_specs=[pl.BlockSpec((1, tm, N_PACK), lambda b, s: (b, s, 0)),
                   pl.BlockSpec((1, 8, tm), lambda b, s: (b, 0, s))],
        out_shape=[jax.ShapeDtypeStruct((bsz, seq, N_PACK), jnp.bfloat16),
                   jax.ShapeDtypeStruct((bsz, 8, seq), jnp.float32)],
        scratch_shapes=[pltpu.VMEM((tm + 8, M_WP), jnp.float32)],
        compiler_params=_params(2),
        name="inproj",
    )(x, mod4, mod4, g_pre, w_pack, wg_t, conv_w, conv_b)


def _mlstm_kernel(cu_ref, v_ref, z_ref, gates_ref, gbias_ref, wq_ref, wkt_ref, hg_ref, skip_ref,
                  o_ref, st_ref, a_ref, b_ref, mp_ref, ml_ref, *, seq):
    L = M_CHUNK
    nchunks = seq // L
    f32 = jnp.float32
    bf16 = jnp.bfloat16
    lane8 = lax.broadcasted_iota(jnp.int32, (8, L), 1)
    row8 = lax.broadcasted_iota(jnp.int32, (8, L), 0)

    amax, gsum = [], []
    for c in range(nchunks):
        g = gates_ref[0, :, c * L:(c + 1) * L] + gbias_ref[...]
        cum = jnp.where(row8 >= M_HEADS, _log_sigmoid(g), 0.0)
        k = 1
        while k < L:
            cum = cum + jnp.where(lane8 >= k, pltpu.roll(cum, k, axis=1), 0.0)
            k *= 2
        b8 = pltpu.roll(cum, M_HEADS, axis=0)
        a8 = g - b8
        a_ref[:, c * L:(c + 1) * L] = a8
        b_ref[:, c * L:(c + 1) * L] = b8
        amax.append(jnp.broadcast_to(jnp.max(a8, axis=1, keepdims=True), (8, L)))
        gsum.append(jnp.broadcast_to(jnp.max(jnp.where(lane8 == L - 1, b8, -jnp.inf), axis=1, keepdims=True), (8, L)))
    m_prev = jnp.zeros((8, L), f32)
    for c in range(nchunks):
        m_last = jnp.maximum(m_prev, amax[c])
        mp_ref[:, c * L:(c + 1) * L] = m_prev
        ml_ref[:, c * L:(c + 1) * L] = m_last
        m_prev = gsum[c] + m_last

    st_ref[...] = jnp.zeros(st_ref.shape, f32)

    t_idx = lax.broadcasted_iota(jnp.int32, (L, L), 0)
    s_idx = lax.broadcasted_iota(jnp.int32, (L, L), 1)
    tri = s_idx <= t_idx
    eye = s_idx == t_idx
    ones = jnp.ones((L, LANES), bf16)
    scale = M_DH ** -0.5
    heads = range(M_HEADS)
    cols = [slice(h * LANES, (h + 1) * LANES) for h in heads]

    def chunk(c, carry):
        r0 = pl.multiple_of(c * L, L)
        rows = pl.ds(r0, L)
        cu = [cu_ref[0, rows, cols[h]] for h in heads]
        q = [(jnp.dot(cu[h], wq_ref[h], preferred_element_type=f32) * scale).astype(bf16) for h in heads]
        kt = [lax.dot_general(wkt_ref[h], cu[h], _NT, preferred_element_type=f32) for h in heads]
        st = [st_ref[h] for h in heads]
        qk = [jnp.dot(q[h], kt[h].astype(bf16), preferred_element_type=f32) for h in heads]
        qs = [jnp.dot(q[h], st[h].astype(bf16), preferred_element_type=f32) for h in heads]
        smat, mmb, bb = [], [], []
        for h in heads:
            a_mat = jnp.broadcast_to(a_ref[h:h + 1, rows], (L, L))
            m1 = jnp.max(jnp.where(tri, a_mat, -jnp.inf), axis=1, keepdims=True)
            mm = jnp.maximum(mp_ref[h:h + 1, rows], m1)
            p = jnp.where(tri, jnp.exp(a_mat - mm), 0.0)
            smat.append((qk[h] * p).astype(bf16))
            mmb.append(mm)
            b_mat = jnp.broadcast_to(b_ref[h:h + 1, rows], (L, L))
            bb.append(jnp.broadcast_to(jnp.sum(jnp.where(eye, b_mat, 0.0), axis=1, keepdims=True), (L, LANES)))
        v2 = [jnp.concatenate([v_ref[0, rows, cols[h]], ones], axis=1) for h in heads]
        sv = [jnp.dot(smat[h], v2[h], preferred_element_type=f32) for h in heads]
        upd = []
        for h in heads:
            ws = jnp.exp(a_ref[h:h + 1, rows] - ml_ref[h:h + 1, rows])
            upd.append(jnp.dot((kt[h] * ws).astype(bf16), v2[h], preferred_element_type=f32))
        for h in heads:
            mp_row = mp_ref[h:h + 1, rows]
            w_inter = jnp.exp(mp_row - mmb[h])
            num = w_inter * qs[h][:, :LANES] + sv[h][:, :LANES]
            den = w_inter * qs[h][:, LANES:] + sv[h][:, LANES:]
            hh = num / jnp.maximum(jnp.abs(den), jnp.exp(-(bb[h] + mmb[h])))
            hh = hh * lax.rsqrt(jnp.sum(hh * hh, axis=1, keepdims=True) * (1.0 / M_DH) + EPS)
            y = hh * hg_ref[:, cols[h]] + skip_ref[:, cols[h]] * cu[h].astype(f32)
            y = y * _silu(z_ref[0, rows, cols[h]].astype(f32))
            o_ref[0, rows, cols[h]] = y.astype(o_ref.dtype)
            cs = jnp.exp(mp_row - ml_ref[h:h + 1, rows])
            st_ref[h] = jnp.concatenate([cs, cs], axis=1) * st[h] + upd[h]
        return carry

    lax.fori_loop(0, nchunks, chunk, 0)


def _mlstm(proj, gates, gbias, wq, wkt, hn_g, skip):
    bsz, seq, _ = proj.shape
    kern = functools.partial(_mlstm_kernel, seq=seq)
    blk = lambda j: pl.BlockSpec((1, seq, M_WP), lambda b: (b, 0, j))
    full = lambda shape: pl.BlockSpec(shape, lambda b: (0,) * len(shape))
    rows = pltpu.VMEM((8, seq), jnp.float32)
    return pl.pallas_call(
        kern,
        grid=(bsz,),
        in_specs=[blk(OFF_MU // M_WP), blk(OFF_MV // M_WP), blk(OFF_MZ // M_WP),
                  pl.BlockSpec((1, 8, seq), lambda b: (b, 0, 0)),
                  full((8, M_CHUNK)), full((M_HEADS, LANES, LANES)), full((M_HEADS, LANES, LANES)),
                  full((1, M_WP)), full((1, M_WP))],
        out_specs=pl.BlockSpec((1, seq, M_WP), lambda b: (b, 0, 0)),
        out_shape=jax.ShapeDtypeStruct((bsz, seq, M_WP), jnp.bfloat16),
        scratch_shapes=[pltpu.VMEM((M_HEADS, LANES, 2 * LANES), jnp.float32), rows, rows, rows, rows],
        compiler_params=_params(1),
        name="mlstm",
    )(proj, proj, proj, gates, gbias, wq, wkt, hn_g, skip)


def _swa_kernel(q_ref, z_ref, kc_ref, kp_ref, vc_ref, vp_ref, slope_ref, sink_ref, o_ref):
    n = pl.program_id(1)
    f32 = jnp.float32
    bf16 = jnp.bfloat16
    W = WINDOW
    ti = lax.broadcasted_iota(jnp.int32, (W, W), 0)
    si = lax.broadcasted_iota(jnp.int32, (W, W), 1)
    cur = si <= ti
    neg_rel = -jnp.where(cur, ti - si, W + ti - si).astype(f32)
    valid = (cur.astype(jnp.int32) + n) > 0
    lane = lax.broadcasted_iota(jnp.int32, (W, LANES), 1)
    lane2 = lax.broadcasted_iota(jnp.int32, (2 * W, LANES), 1)
    halves = (lane < A_DH, lane >= A_DH)
    kk = jnp.concatenate([kc_ref[0], kp_ref[0]], axis=0)
    vv = jnp.concatenate([vc_ref[0], vp_ref[0]], axis=0)
    vms = [jnp.where(lane2 < A_DH, vv, jnp.zeros_like(vv)), jnp.where(lane2 >= A_DH, vv, jnp.zeros_like(vv))]
    q_all = q_ref[0]
    q_all = q_all * jnp.asarray(A_DH ** -0.5, q_all.dtype)
    pairs = [(g, kv) for g in range(A_G) for kv in range(A_KV)]
    sc = []
    for g, kv in pairs:
        qg = q_all[:, g * LANES:(g + 1) * LANES]
        qm = jnp.where(halves[kv], qg, jnp.zeros_like(qg))
        sc.append(lax.dot_general(qm, kk, _NT, preferred_element_type=f32))
    pc = []
    for i, (g, kv) in enumerate(pairs):
        head = kv * A_G + g
        s2 = jnp.where(cur, sc[i][:, :W], sc[i][:, W:]) + slope_ref[head] * neg_rel
        s2 = jnp.where(valid, s2, -jnp.inf)
        sink = sink_ref[head]
        mx = jnp.maximum(jnp.max(s2, axis=1, keepdims=True), sink)
        e = jnp.exp(s2 - mx)
        den = jnp.sum(e, axis=1, keepdims=True) + jnp.exp(sink - mx)
        probs = (e * (1.0 / den)).astype(bf16)
        zero = jnp.zeros_like(probs)
        pc.append(jnp.concatenate([jnp.where(cur, probs, zero), jnp.where(cur, zero, probs)], axis=1))
    for g in range(A_G):
        cols = slice(g * LANES, (g + 1) * LANES)
        o = (jnp.dot(pc[2 * g], vms[0], preferred_element_type=f32)
             + jnp.dot(pc[2 * g + 1], vms[1], preferred_element_type=f32))
        o_ref[0, :, cols] = (o * _silu(z_ref[0, :, cols].astype(f32))).astype(o_ref.dtype)


def _swa(proj, slopes, sinks):
    bsz, seq, _ = proj.shape
    nb = seq // WINDOW
    smem = pl.BlockSpec(memory_space=pltpu.SMEM)
    cur = lambda off, w: pl.BlockSpec((1, WINDOW, w), lambda b, n: (b, n, off // w))
    prev = lambda off, w: pl.BlockSpec((1, WINDOW, w), lambda b, n: (b, jnp.maximum(n - 1, 0), off // w))
    return pl.pallas_call(
        _swa_kernel,
        grid=(bsz, nb),
        in_specs=[cur(OFF_AQ, A_W), cur(OFF_AZ, A_W),
                  cur(OFF_AK, A_KVW), prev(OFF_AK, A_KVW), cur(OFF_AV, A_KVW), prev(OFF_AV, A_KVW),
                  smem, smem],
        out_specs=pl.BlockSpec((1, WINDOW, A_W), lambda b, n: (b, n, 0)),
        out_shape=jax.ShapeDtypeStruct((bsz, seq, A_W), jnp.bfloat16),
        compiler_params=_params(2),
        name="swa",
    )(proj, proj, proj, proj, proj, proj, slopes, sinks)


def _sb_kernel(q_ref, z_ref, k_ref, v_ref, o_ref, acc_ref, run_ref, *, seq):
    f32 = jnp.float32
    T = SB_TILE
    lane = lax.broadcasted_iota(jnp.int32, (T, LANES), 1)
    halves = (lane < S_DH, lane >= S_DH)
    ti = lax.broadcasted_iota(jnp.int32, (T, T), 0)
    si = lax.broadcasted_iota(jnp.int32, (T, T), 1)
    strict = si < ti
    usum = jnp.where(ti > si, 1.0, 0.0).astype(jnp.bfloat16)

    def qblock(n, carry):
        r0 = pl.multiple_of(n * T, T)
        q = q_ref[0, pl.ds(r0, T), :]
        q = q * jnp.asarray(S_DH ** -0.5, q.dtype)
        acc_ref[...] = jnp.zeros(acc_ref.shape, f32)
        run_ref[...] = jnp.zeros(run_ref.shape, f32)

        def tile(j, diag):
            k0 = pl.multiple_of(j * T, T)
            kb = k_ref[0, pl.ds(k0, T), :]
            vb = v_ref[0, pl.ds(k0, T), :]
            heads = range(S_HEADS)
            cols = [slice((h // 2) * LANES, (h // 2 + 1) * LANES) for h in heads]
            zz = [lax.dot_general(jnp.where(halves[h % 2], q[:, cols[h]], jnp.zeros((T, LANES), q.dtype)),
                                  kb[:, cols[h]], _NT, preferred_element_type=f32) for h in heads]
            ls, lk = [], []
            for h in heads:
                z2 = zz[h] * LOG2E
                soft = jnp.log2(1.0 + jnp.exp2(-jnp.abs(z2)))
                ls.append(jnp.minimum(z2, 0.0) - soft)
                lkh = ls[h] - z2
                lk.append(jnp.where(strict, lkh, 0.0) if diag else lkh)
            suf = [jnp.dot(lk[h].astype(jnp.bfloat16), usum, preferred_element_type=f32) for h in heads]
            a = []
            for h in heads:
                run = run_ref[h]
                ah = jnp.exp2(ls[h] + suf[h] + jnp.concatenate([run] * (T // LANES), axis=1))
                a.append((jnp.where(strict, ah, 0.0) if diag else ah).astype(jnp.bfloat16))
                run_ref[h] = run + jnp.sum(lk[h], axis=1, keepdims=True)
            for p in range(S_PAIRS):
                pv = None
                for h in (2 * p, 2 * p + 1):
                    vm = jnp.where(halves[h % 2], vb[:, cols[h]], jnp.zeros((T, LANES), vb.dtype))
                    d = jnp.dot(a[h], vm, preferred_element_type=f32)
                    pv = d if pv is None else pv + d
                acc_ref[:, cols[2 * p]] += pv

        tile(n, True)

        def body(i, c):
            tile(n - 1 - i, False)
            return c

        lax.fori_loop(0, n, body, 0)
        zg = z_ref[0, pl.ds(r0, T), :].astype(f32)
        o_ref[0, pl.ds(r0, T), :] = (acc_ref[...] * _silu(zg)).astype(o_ref.dtype)
        return carry

    lax.fori_loop(0, seq // T, qblock, 0)


def _stickbreak(proj):
    bsz, seq, _ = proj.shape
    kern = functools.partial(_sb_kernel, seq=seq)
    full = lambda off: pl.BlockSpec((1, seq, S_W), lambda b: (b, 0, off // S_W))
    return pl.pallas_call(
        kern,
        grid=(bsz,),
        in_specs=[full(OFF_SQ), full(OFF_SZ), full(OFF_SK), full(OFF_SV)],
        out_specs=pl.BlockSpec((1, seq, S_W), lambda b: (b, 0, 0)),
        out_shape=jax.ShapeDtypeStruct((bsz, seq, S_W), jnp.bfloat16),
        scratch_shapes=[pltpu.VMEM((SB_TILE, S_W), jnp.float32),
                        pltpu.VMEM((S_HEADS, SB_TILE, LANES), jnp.float32)],
        compiler_params=_params(1),
        name="stickbreak",
    )(proj, proj, proj, proj)


def _outproj_kernel(x_ref, ym_ref, ya_ref, ys_ref, wm_ref, wa_ref, ws_ref, g_ref, gate_ref, o_ref, *, tm, sub):
    f32 = jnp.float32
    for i in range(tm // sub):
        r = slice(i * sub, (i + 1) * sub)
        y = (jnp.dot(ym_ref[0, r, :], wm_ref[...], preferred_element_type=f32)
             + jnp.dot(ya_ref[0, r, :], wa_ref[...], preferred_element_type=f32)
             + jnp.dot(ys_ref[0, r, :], ws_ref[...], preferred_element_type=f32))
        yn = y * lax.rsqrt(jnp.mean(y * y, axis=-1, keepdims=True) + EPS) * g_ref[...]
        o_ref[0, r, :] = x_ref[0, r, :] + gate_ref[...] * yn


def _outproj(x, ym, ya, ys, wm, wa, ws, g_post, mod4, layer):
    bsz, seq, d = x.shape
    tm = 1024
    kern = functools.partial(_outproj_kernel, tm=tm, sub=256)
    row = lambda w: pl.BlockSpec((1, tm, w), lambda b, s: (b, s, 0))
    const = lambda shape: pl.BlockSpec(shape, lambda b, s: (0,) * len(shape), pipeline_mode=pl.Buffered(1))
    return pl.pallas_call(
        kern,
        grid=(bsz, seq // tm),
        in_specs=[row(d), row(M_WP), row(A_W), row(S_W),
                  const((M_WP, d)), const((A_W, d)), const((S_W, d)), const((1, d)),
                  pl.BlockSpec((None, None, 1, d), lambda b, s: (layer, b, 0, 2))],
        out_specs=row(d),
        out_shape=jax.ShapeDtypeStruct((bsz, seq, d), jnp.float32),
        compiler_params=_params(2),
        name="outproj",
    )(x, ym, ya, ys, wm, wa, ws, g_post, mod4)


def _pad_heads(w, axis):
    shape = list(w.shape)
    shape[axis:axis + 1] = [M_HEADS, M_DH]
    w = w.reshape(shape)
    pad = [(0, 0)] * w.ndim
    pad[axis + 1] = (0, LANES - M_DH)
    w = jnp.pad(w, pad)
    shape[axis:axis + 2] = [M_WP]
    return w.reshape(shape)


def _pair_heads(w, axis):
    shape = list(w.shape)
    shape[axis:axis + 1] = [A_KV, A_G, A_DH]
    w = jnp.swapaxes(w.reshape(shape), axis, axis + 1)
    shape[axis:axis + 3] = [A_W]
    return w.reshape(shape)


def _pack_w_in(w):
    o = np.cumsum([0, M_W, M_W, M_HEADS, M_HEADS, M_W, A_W, A_KVW, A_KVW, A_W, S_W, S_W, S_W, S_W])
    seg = lambda i: w[:, int(o[i]):int(o[i + 1])]
    packed = jnp.concatenate(
        [_pad_heads(seg(0), 1), _pad_heads(seg(1), 1), _pad_heads(seg(4), 1),
         _pair_heads(seg(5), 1), _pair_heads(seg(8), 1), seg(6), seg(7),
         seg(9), seg(10), seg(11), seg(12)], axis=1).astype(jnp.bfloat16)
    gates_t = jnp.concatenate([seg(2), seg(3)], axis=1).T.astype(jnp.bfloat16)
    return packed, gates_t


def _pack_w_out(w):
    wm = _pad_heads(w[:M_W], 0)
    wa = _pair_heads(w[M_W:M_W + A_W], 0)
    ws = w[M_W + A_W:]
    return wm.astype(jnp.bfloat16), wa.astype(jnp.bfloat16), ws.astype(jnp.bfloat16)


def _pad_qk(w):
    return jnp.pad(w, ((0, 0), (0, LANES - M_DH), (0, LANES - M_DH))).astype(jnp.bfloat16)


def kernel(x, c, w_mod, b_mod, g_pre, g_post, w_in, m_conv_w, m_conv_b, m_wq, m_wk, m_b_i, m_b_f,
           m_norm_g, m_skip, a_sinks, w_out):
    bsz = x.shape[0]
    mod4 = _modulation(c, w_mod, b_mod).reshape(DEPTH, bsz, 1, 3 * D_MODEL)
    slopes = jnp.asarray(2.0 ** (-8.0 * np.arange(1, A_HEADS + 1) / A_HEADS), dtype=jnp.float32)
    for l in range(DEPTH):
        w_pack, wg_t = _pack_w_in(w_in[l])
        wm, wa, ws = _pack_w_out(w_out[l])
        gbias = jnp.broadcast_to(jnp.concatenate([m_b_i[l], m_b_f[l]])[:, None], (2 * M_HEADS, M_CHUNK))
        proj, gates = _inproj(x, mod4, l, g_pre[l][None], w_pack, wg_t,
                              _pad_heads(m_conv_w[l], 1), _pad_heads(m_conv_b[l][None], 1))
        ym = _mlstm(proj, gates, gbias, _pad_qk(m_wq[l]), _pad_qk(jnp.swapaxes(m_wk[l], 1, 2)),
                    _pad_heads(m_norm_g[l][None], 1), _pad_heads(m_skip[l][None], 1))
        ya = _swa(proj, slopes, a_sinks[l])
        ys = _stickbreak(proj)
        x = _outproj(x, ym, ya, ys, wm, wa, ws, g_post[l][None], mod4, l)
    return x
```

```python
import functools

import jax
import jax.numpy as jnp
import numpy as np
from jax import lax
from jax.experimental import pallas as pl
from jax.experimental.pallas import tpu as pltpu

D_MODEL = 1024
DEPTH = 2
M_HEADS = 4
M_DH = 96
M_W = M_HEADS * M_DH
M_CONV = 4
A_HEADS = 6
A_KV = 2
A_G = A_HEADS // A_KV
A_DH = 64
A_W = A_HEADS * A_DH
A_KVW = A_KV * A_DH
WINDOW = 128
S_HEADS = 4
S_DH = 64
S_W = S_HEADS * S_DH
S_BLOCK = 128
EPS = 1e-6

LANES = 128
M_CHUNK = 128
M_WP = M_HEADS * LANES
SB_TILE = 256
S_PAIRS = S_W // LANES

OFF_MU, OFF_MV, OFF_MZ = 0, M_WP, 2 * M_WP
OFF_AQ = 3 * M_WP
OFF_AZ = OFF_AQ + A_W
OFF_AK = OFF_AZ + A_W
OFF_AV = OFF_AK + A_KVW
OFF_SQ = OFF_AV + A_KVW
OFF_SK = OFF_SQ + S_W
OFF_SV = OFF_SK + S_W
OFF_SZ = OFF_SV + S_W
N_PACK = OFF_SZ + S_W

VMEM_LIMIT = 48 * 1024 * 1024

_NT = (((1,), (1,)), ((), ()))
LOG2E = 1.4426950408889634


def _log_sigmoid(x):
    return jnp.minimum(x, 0.0) - jnp.log(1.0 + jnp.exp(-jnp.abs(x)))


def _silu(x):
    return x / (1.0 + jnp.exp(-x))


def _params(n_axes):
    return pltpu.CompilerParams(dimension_semantics=("arbitrary",) * n_axes, vmem_limit_bytes=VMEM_LIMIT)


def _mod_kernel(c_ref, w_ref, b_ref, o_ref):
    c_act = _silu(c_ref[...]).astype(jnp.bfloat16)
    o_ref[0] = jnp.dot(c_act, w_ref[0].astype(jnp.bfloat16), preferred_element_type=jnp.float32) + b_ref[0]


def _modulation(c, w_mod, b_mod):
    depth, d, n = w_mod.shape
    bsz = c.shape[0]
    tn = 1024
    return pl.pallas_call(
        _mod_kernel,
        grid=(depth, n // tn),
        in_specs=[pl.BlockSpec((bsz, d), lambda l, j: (0, 0)),
                  pl.BlockSpec((1, d, tn), lambda l, j: (l, 0, j)),
                  pl.BlockSpec((1, 1, tn), lambda l, j: (l, 0, j))],
        out_specs=pl.BlockSpec((1, bsz, tn), lambda l, j: (l, 0, j)),
        out_shape=jax.ShapeDtypeStruct((depth, bsz, n), jnp.float32),
        compiler_params=_params(2),
        name="modulation",
    )(c, w_mod, b_mod.reshape(depth, 1, n))


def _inproj_kernel(x_ref, shift_ref, scale_ref, g_ref, w_ref, wg_ref, cw_ref, cb_ref,
                   proj_ref, gates_ref, conv_ref, *, tm, sub, nchunk):
    s = pl.program_id(1)
    f32 = jnp.float32

    @pl.when(s == 0)
    def _():
        conv_ref[0:8, :] = jnp.zeros((8, M_WP), f32)

    subs = [slice(i * sub, (i + 1) * sub) for i in range(tm // sub)]
    hs = []
    for r in subs:
        x = x_ref[0, r, :]
        ms = jnp.mean(x * x, axis=-1, keepdims=True)
        h = x * lax.rsqrt(ms + EPS) * g_ref[...]
        hs.append((h * (1.0 + scale_ref[...]) + shift_ref[...]).astype(jnp.bfloat16))

    for i, (r, h) in enumerate(zip(subs, hs)):
        gates_ref[0, :, r] = lax.dot_general(wg_ref[...], h, _NT, preferred_element_type=f32)

        base = 8 + i * sub
        conv_ref[base:base + sub, :] = jnp.dot(h, w_ref[:, OFF_MU:OFF_MU + M_WP], preferred_element_type=f32)
        acc = cb_ref[...] + cw_ref[M_CONV - 1:M_CONV, :] * conv_ref[base:base + sub, :]
        for j in range(M_CONV - 1):
            lag = M_CONV - 1 - j
            acc = acc + cw_ref[j:j + 1, :] * conv_ref[base - lag:base - lag + sub, :]
        proj_ref[0, r, OFF_MU:OFF_MU + M_WP] = _silu(acc).astype(jnp.bfloat16)

        for c0 in range(OFF_MV, N_PACK, nchunk):
            proj_ref[0, r, c0:c0 + nchunk] = jnp.dot(
                h, w_ref[:, c0:c0 + nchunk], preferred_element_type=f32).astype(jnp.bfloat16)

    conv_ref[0:8, :] = conv_ref[tm:tm + 8, :]


def _inproj(x, mod4, layer, g_pre, w_pack, wg_t, conv_w, conv_b):
    bsz, seq, d = x.shape
    tm = 1024
    kern = functools.partial(_inproj_kernel, tm=tm, sub=512, nchunk=512)
    const = lambda shape: pl.BlockSpec(shape, lambda b, s: (0,) * len(shape), pipeline_mode=pl.Buffered(1))
    return pl.pallas_call(
        kern,
        grid=(bsz, seq // tm),
        in_specs=[pl.BlockSpec((1, tm, d), lambda b, s: (b, s, 0)),
                  pl.BlockSpec((None, None, 1, d), lambda b, s: (layer, b, 0, 0)),
                  pl.BlockSpec((None, None, 1, d), lambda b, s: (layer, b, 0, 1)),
                  const((1, d)), const((d, N_PACK)), const((8, d)), const((M_CONV, M_WP)), const((1, M_WP))],
        out_specs=[pl.BlockSpec((1, tm, N_PACK), lambda b, s: (b, s, 0)),
                   pl.BlockSpec((1, 8, tm), lambda b, s: (b, 0, s))],
        out_shape=[jax.ShapeDtypeStruct((bsz, seq, N_PACK), jnp.bfloat16),
                   jax.ShapeDtypeStruct((bsz, 8, seq), jnp.float32)],
        scratch_shapes=[pltpu.VMEM((tm + 8, M_WP), jnp.float32)],
        compiler_params=_params(2),
        name="inproj",
    )(x, mod4, mod4, g_pre, w_pack, wg_t, conv_w, conv_b)


def _mlstm_kernel(cu_ref, v_ref, z_ref, gates_ref, gbias_ref, wq_ref, wkt_ref, hg_ref, skip_ref,
                  o_ref, st_ref, a_ref, b_ref, mp_ref, ml_ref, *, seq):
    L = M_CHUNK
    nchunks = seq // L
    f32 = jnp.float32
    bf16 = jnp.bfloat16
    lane8 = lax.broadcasted_iota(jnp.int32, (8, L), 1)
    row8 = lax.broadcasted_iota(jnp.int32, (8, L), 0)

    amax, gsum = [], []
    for c in range(nchunks):
        g = gates_ref[0, :, c * L:(c + 1) * L] + gbias_ref[...]
        cum = jnp.where(row8 >= M_HEADS, _log_sigmoid(g), 0.0)
        k = 1
        while k < L:
            cum = cum + jnp.where(lane8 >= k, pltpu.roll(cum, k, axis=1), 0.0)
            k *= 2
        b8 = pltpu.roll(cum, M_HEADS, axis=0)
        a8 = g - b8
        a_ref[:, c * L:(c + 1) * L] = a8
        b_ref[:, c * L:(c + 1) * L] = b8
        amax.append(jnp.broadcast_to(jnp.max(a8, axis=1, keepdims=True), (8, L)))
        gsum.append(jnp.broadcast_to(jnp.max(jnp.where(lane8 == L - 1, b8, -jnp.inf), axis=1, keepdims=True), (8, L)))
    m_prev = jnp.zeros((8, L), f32)
    for c in range(nchunks):
        m_last = jnp.maximum(m_prev, amax[c])
        mp_ref[:, c * L:(c + 1) * L] = m_prev
        ml_ref[:, c * L:(c + 1) * L] = m_last
        m_prev = gsum[c] + m_last

    st_ref[...] = jnp.zeros(st_ref.shape, f32)

    t_idx = lax.broadcasted_iota(jnp.int32, (L, L), 0)
    s_idx = lax.broadcasted_iota(jnp.int32, (L, L), 1)
    tri = s_idx <= t_idx
    eye = s_idx == t_idx
    ones = jnp.ones((L, LANES), bf16)
    scale = M_DH ** -0.5
    heads = range(M_HEADS)
    cols = [slice(h * LANES, (h + 1) * LANES) for h in heads]

    def chunk(c, carry):
        r0 = pl.multiple_of(c * L, L)
        rows = pl.ds(r0, L)
        cu = [cu_ref[0, rows, cols[h]] for h in heads]
        q = [(jnp.dot(cu[h], wq_ref[h], preferred_element_type=f32) * scale).astype(bf16) for h in heads]
        kt = [lax.dot_general(wkt_ref[h], cu[h], _NT, preferred_element_type=f32) for h in heads]
        st = [st_ref[h] for h in heads]
        qk = [jnp.dot(q[h], kt[h].astype(bf16), preferred_element_type=f32) for h in heads]
        qs = [jnp.dot(q[h], st[h].astype(bf16), preferred_element_type=f32) for h in heads]
        smat, mmb, bb = [], [], []
        for h in heads:
            a_mat = jnp.broadcast_to(a_ref[h:h + 1, rows], (L, L))
            m1 = jnp.max(jnp.where(tri, a_mat, -jnp.inf), axis=1, keepdims=True)
            mm = jnp.maximum(mp_ref[h:h + 1, rows], m1)
            p = jnp.where(tri, jnp.exp(a_mat - mm), 0.0)
            smat.append((qk[h] * p).astype(bf16))
            mmb.append(mm)
            b_mat = jnp.broadcast_to(b_ref[h:h + 1, rows], (L, L))
            bb.append(jnp.broadcast_to(jnp.sum(jnp.where(eye, b_mat, 0.0), axis=1, keepdims=True), (L, LANES)))
        v2 = [jnp.concatenate([v_ref[0, rows, cols[h]], ones], axis=1) for h in heads]
        sv = [jnp.dot(smat[h], v2[h], preferred_element_type=f32) for h in heads]
        upd = []
        for h in heads:
            ws = jnp.exp(a_ref[h:h + 1, rows] - ml_ref[h:h + 1, rows])
            upd.append(jnp.dot((kt[h] * ws).astype(bf16), v2[h], preferred_element_type=f32))
        for h in heads:
            mp_row = mp_ref[h:h + 1, rows]
            w_inter = jnp.exp(mp_row - mmb[h])
            num = w_inter * qs[h][:, :LANES] + sv[h][:, :LANES]
            den = w_inter * qs[h][:, LANES:] + sv[h][:, LANES:]
            hh = num / jnp.maximum(jnp.abs(den), jnp.exp(-(bb[h] + mmb[h])))
            hh = hh * lax.rsqrt(jnp.sum(hh * hh, axis=1, keepdims=True) * (1.0 / M_DH) + EPS)
            y = hh * hg_ref[:, cols[h]] + skip_ref[:, cols[h]] * cu[h].astype(f32)
            y = y * _silu(z_ref[0, rows, cols[h]].astype(f32))
            o_ref[0, rows, cols[h]] = y.astype(o_ref.dtype)
            cs = jnp.exp(mp_row - ml_ref[h:h + 1, rows])
            st_ref[h] = jnp.concatenate([cs, cs], axis=1) * st[h] + upd[h]
        return carry

    lax.fori_loop(0, nchunks, chunk, 0)


def _mlstm(proj, gates, gbias, wq, wkt, hn_g, skip):
    bsz, seq, _ = proj.shape
    kern = functools.partial(_mlstm_kernel, seq=seq)
    blk = lambda j: pl.BlockSpec((1, seq, M_WP), lambda b: (b, 0, j))
    full = lambda shape: pl.BlockSpec(shape, lambda b: (0,) * len(shape))
    rows = pltpu.VMEM((8, seq), jnp.float32)
    return pl.pallas_call(
        kern,
        grid=(bsz,),
        in_specs=[blk(OFF_MU // M_WP), blk(OFF_MV // M_WP), blk(OFF_MZ // M_WP),
                  pl.BlockSpec((1, 8, seq), lambda b: (b, 0, 0)),
                  full((8, M_CHUNK)), full((M_HEADS, LANES, LANES)), full((M_HEADS, LANES, LANES)),
                  full((1, M_WP)), full((1, M_WP))],
        out_specs=pl.BlockSpec((1, seq, M_WP), lambda b: (b, 0, 0)),
        out_shape=jax.ShapeDtypeStruct((bsz, seq, M_WP), jnp.bfloat16),
        scratch_shapes=[pltpu.VMEM((M_HEADS, LANES, 2 * LANES), jnp.float32), rows, rows, rows, rows],
        compiler_params=_params(1),
        name="mlstm",
    )(proj, proj, proj, gates, gbias, wq, wkt, hn_g, skip)


def _swa_kernel(q_ref, z_ref, k_ref, v_ref, slope_ref, sink_ref, o_ref, *, seq):
    f32 = jnp.float32
    bf16 = jnp.bfloat16
    W = WINDOW
    ti = lax.broadcasted_iota(jnp.int32, (W, W), 0)
    si = lax.broadcasted_iota(jnp.int32, (W, W), 1)
    cur = si <= ti
    neg_rel = -jnp.where(cur, ti - si, W + ti - si).astype(f32)
    lane = lax.broadcasted_iota(jnp.int32, (W, LANES), 1)
    lane2 = lax.broadcasted_iota(jnp.int32, (2 * W, LANES), 1)
    halves = (lane < A_DH, lane >= A_DH)
    pairs = [(g, kv) for g in range(A_G) for kv in range(A_KV)]

    def block(n, carry):
        rows = pl.ds(pl.multiple_of(n * W, W), W)
        prev = pl.ds(pl.multiple_of(jnp.maximum(n - 1, 0) * W, W), W)
        valid = (cur.astype(jnp.int32) + n) > 0
        kk = jnp.concatenate([k_ref[0, rows, :], k_ref[0, prev, :]], axis=0)
        vv = jnp.concatenate([v_ref[0, rows, :], v_ref[0, prev, :]], axis=0)
        vms = [jnp.where(lane2 < A_DH, vv, jnp.zeros_like(vv)), jnp.where(lane2 >= A_DH, vv, jnp.zeros_like(vv))]
        q_all = q_ref[0, rows, :]
        q_all = q_all * jnp.asarray(A_DH ** -0.5, q_all.dtype)
        sc = []
        for g, kv in pairs:
            qg = q_all[:, g * LANES:(g + 1) * LANES]
            qm = jnp.where(halves[kv], qg, jnp.zeros_like(qg))
            sc.append(lax.dot_general(qm, kk, _NT, preferred_element_type=f32))
        pc = []
        for i, (g, kv) in enumerate(pairs):
            head = kv * A_G + g
            s2 = jnp.where(cur, sc[i][:, :W], sc[i][:, W:]) + slope_ref[head] * neg_rel
            s2 = jnp.where(valid, s2, -jnp.inf)
            sink = sink_ref[head]
            mx = jnp.maximum(jnp.max(s2, axis=1, keepdims=True), sink)
            e = jnp.exp(s2 - mx)
            den = jnp.sum(e, axis=1, keepdims=True) + jnp.exp(sink - mx)
            probs = (e * (1.0 / den)).astype(bf16)
            zero = jnp.zeros_like(probs)
            pc.append(jnp.concatenate([jnp.where(cur, probs, zero), jnp.where(cur, zero, probs)], axis=1))
        for g in range(A_G):
            cols = slice(g * LANES, (g + 1) * LANES)
            o = (jnp.dot(pc[2 * g], vms[0], preferred_element_type=f32)
                 + jnp.dot(pc[2 * g + 1], vms[1], preferred_element_type=f32))
            o_ref[0, rows, cols] = (o * _silu(z_ref[0, rows, cols].astype(f32))).astype(o_ref.dtype)
        return carry

    lax.fori_loop(0, seq // W, block, 0, unroll=4)


def _swa(proj, slopes, sinks):
    bsz, seq, _ = proj.shape
    kern = functools.partial(_swa_kernel, seq=seq)
    smem = pl.BlockSpec(memory_space=pltpu.SMEM)
    full = lambda off, w: pl.BlockSpec((1, seq, w), lambda b: (b, 0, off // w))
    return pl.pallas_call(
        kern,
        grid=(bsz,),
        in_specs=[full(OFF_AQ, A_W), full(OFF_AZ, A_W), full(OFF_AK, A_KVW), full(OFF_AV, A_KVW), smem, smem],
        out_specs=pl.BlockSpec((1, seq, A_W), lambda b: (b, 0, 0)),
        out_shape=jax.ShapeDtypeStruct((bsz, seq, A_W), jnp.bfloat16),
        compiler_params=_params(1),
        name="swa",
    )(proj, proj, proj, proj, slopes, sinks)


def _sb_kernel(q_ref, z_ref, k_ref, v_ref, o_ref, acc_ref, run_ref, zz_ref, a_ref, *, seq):
    f32 = jnp.float32
    bf16 = jnp.bfloat16
    T = SB_TILE
    lane = lax.broadcasted_iota(jnp.int32, (T, LANES), 1)
    halves = (lane < S_DH, lane >= S_DH)
    ti = lax.broadcasted_iota(jnp.int32, (T, T), 0)
    si = lax.broadcasted_iota(jnp.int32, (T, T), 1)
    strict = si < ti
    usum = jnp.where(ti > si, 1.0, 0.0).astype(bf16)
    heads = range(S_HEADS)
    cols = [slice((h // 2) * LANES, (h // 2 + 1) * LANES) for h in heads]

    def qblock(n, carry):
        r0 = pl.multiple_of(n * T, T)
        q = q_ref[0, pl.ds(r0, T), :]
        q = q * jnp.asarray(S_DH ** -0.5, q.dtype)
        qm = [jnp.where(halves[h % 2], q[:, cols[h]], jnp.zeros((T, LANES), q.dtype)) for h in heads]
        acc_ref[...] = jnp.zeros(acc_ref.shape, f32)
        run_ref[...] = jnp.zeros(run_ref.shape, f32)

        def score(j):
            k0 = pl.multiple_of(j * T, T)
            kb = k_ref[0, pl.ds(k0, T), :]
            for h in heads:
                zz_ref[h] = lax.dot_general(qm[h], kb[:, cols[h]], _NT, preferred_element_type=f32)

        def apply(j):
            k0 = pl.multiple_of(j * T, T)
            vb = v_ref[0, pl.ds(k0, T), :]
            for p in range(S_PAIRS):
                pv = None
                for h in (2 * p, 2 * p + 1):
                    vm = jnp.where(halves[h % 2], vb[:, cols[h]], jnp.zeros((T, LANES), vb.dtype))
                    d = jnp.dot(a_ref[h], vm, preferred_element_type=f32)
                    pv = d if pv is None else pv + d
                acc_ref[:, cols[2 * p]] += pv

        def weights(j_next, diag):
            ls, lk = [], []
            for h in heads:
                zz = zz_ref[h]
                soft = jnp.log(1.0 + jnp.exp2(jnp.abs(zz) * (-LOG2E)))
                ls.append(jnp.minimum(zz, 0.0) - soft)
                lkh = ls[h] - zz
                lk.append(jnp.where(strict, lkh, 0.0) if diag else lkh)
            suf = [jnp.dot(lk[h].astype(bf16), usum, preferred_element_type=f32) for h in heads]
            score(j_next)
            for h in heads:
                run = run_ref[h]
                ah = jnp.exp2((ls[h] + suf[h] + jnp.concatenate([run] * (T // LANES), axis=1)) * LOG2E)
                a_ref[h] = (jnp.where(strict, ah, 0.0) if diag else ah).astype(bf16)
                run_ref[h] = run + jnp.sum(lk[h], axis=1, keepdims=True)

        score(n)
        weights(jnp.maximum(n - 1, 0), True)

        def body(i, c):
            apply(n - i + 1)
            weights(jnp.maximum(n - i - 1, 0), False)
            return c

        lax.fori_loop(1, n + 1, body, 0)
        apply(0)
        zg = z_ref[0, pl.ds(r0, T), :].astype(f32)
        o_ref[0, pl.ds(r0, T), :] = (acc_ref[...] * _silu(zg)).astype(o_ref.dtype)
        return carry

    lax.fori_loop(0, seq // T, qblock, 0)


def _stickbreak(proj):
    bsz, seq, _ = proj.shape
    kern = functools.partial(_sb_kernel, seq=seq)
    full = lambda off: pl.BlockSpec((1, seq, S_W), lambda b: (b, 0, off // S_W))
    return pl.pallas_call(
        kern,
        grid=(bsz,),
        in_specs=[full(OFF_SQ), full(OFF_SZ), full(OFF_SK), full(OFF_SV)],
        out_specs=pl.BlockSpec((1, seq, S_W), lambda b: (b, 0, 0)),
        out_shape=jax.ShapeDtypeStruct((bsz, seq, S_W), jnp.bfloat16),
        scratch_shapes=[pltpu.VMEM((SB_TILE, S_W), jnp.float32),
                        pltpu.VMEM((S_HEADS, SB_TILE, LANES), jnp.float32),
                        pltpu.VMEM((S_HEADS, SB_TILE, SB_TILE), jnp.float32),
                        pltpu.VMEM((S_HEADS, SB_TILE, SB_TILE), jnp.bfloat16)],
        compiler_params=_params(1),
        name="stickbreak",
    )(proj, proj, proj, proj)


def _outproj_kernel(x_ref, ym_ref, ya_ref, ys_ref, wm_ref, wa_ref, ws_ref, g_ref, gate_ref, o_ref, *, tm, sub):
    f32 = jnp.float32
    for i in range(tm // sub):
        r = slice(i * sub, (i + 1) * sub)
        y = (jnp.dot(ym_ref[0, r, :], wm_ref[...], preferred_element_type=f32)
             + jnp.dot(ya_ref[0, r, :], wa_ref[...], preferred_element_type=f32)
             + jnp.dot(ys_ref[0, r, :], ws_ref[...], preferred_element_type=f32))
        yn = y * lax.rsqrt(jnp.mean(y * y, axis=-1, keepdims=True) + EPS) * g_ref[...]
        o_ref[0, r, :] = x_ref[0, r, :] + gate_ref[...] * yn


def _outproj(x, ym, ya, ys, wm, wa, ws, g_post, mod4, layer):
    bsz, seq, d = x.shape
    tm = 1024
    kern = functools.partial(_outproj_kernel, tm=tm, sub=256)
    row = lambda w: pl.BlockSpec((1, tm, w), lambda b, s: (b, s, 0))
    const = lambda shape: pl.BlockSpec(shape, lambda b, s: (0,) * len(shape), pipeline_mode=pl.Buffered(1))
    return pl.pallas_call(
        kern,
        grid=(bsz, seq // tm),
        in_specs=[row(d), row(M_WP), row(A_W), row(S_W),
                  const((M_WP, d)), const((A_W, d)), const((S_W, d)), const((1, d)),
                  pl.BlockSpec((None, None, 1, d), lambda b, s: (layer, b, 0, 2))],
        out_specs=row(d),
        out_shape=jax.ShapeDtypeStruct((bsz, seq, d), jnp.float32),
        compiler_params=_params(2),
        name="outproj",
    )(x, ym, ya, ys, wm, wa, ws, g_post, mod4)


def _pad_heads(w, axis):
    shape = list(w.shape)
    shape[axis:axis + 1] = [M_HEADS, M_DH]
    w = w.reshape(shape)
    pad = [(0, 0)] * w.ndim
    pad[axis + 1] = (0, LANES - M_DH)
    w = jnp.pad(w, pad)
    shape[axis:axis + 2] = [M_WP]
    return w.reshape(shape)


def _pair_heads(w, axis):
    shape = list(w.shape)
    shape[axis:axis + 1] = [A_KV, A_G, A_DH]
    w = jnp.swapaxes(w.reshape(shape), axis, axis + 1)
    shape[axis:axis + 3] = [A_W]
    return w.reshape(shape)


def _pack_w_in(w):
    o = np.cumsum([0, M_W, M_W, M_HEADS, M_HEADS, M_W, A_W, A_KVW, A_KVW, A_W, S_W, S_W, S_W, S_W])
    seg = lambda i: w[:, int(o[i]):int(o[i + 1])]
    packed = jnp.concatenate(
        [_pad_heads(seg(0), 1), _pad_heads(seg(1), 1), _pad_heads(seg(4), 1),
         _pair_heads(seg(5), 1), _pair_heads(seg(8), 1), seg(6), seg(7),
         seg(9), seg(10), seg(11), seg(12)], axis=1).astype(jnp.bfloat16)
    gates_t = jnp.concatenate([seg(2), seg(3)], axis=1).T.astype(jnp.bfloat16)
    return packed, gates_t


def _pack_w_out(w):
    wm = _pad_heads(w[:M_W], 0)
    wa = _pair_heads(w[M_W:M_W + A_W], 0)
    ws = w[M_W + A_W:]
    return wm.astype(jnp.bfloat16), wa.astype(jnp.bfloat16), ws.astype(jnp.bfloat16)


def _pad_qk(w):
    return jnp.pad(w, ((0, 0), (0, LANES - M_DH), (0, LANES - M_DH))).astype(jnp.bfloat16)


def kernel(x, c, w_mod, b_mod, g_pre, g_post, w_in, m_conv_w, m_conv_b, m_wq, m_wk, m_b_i, m_b_f,
           m_norm_g, m_skip, a_sinks, w_out):
    bsz = x.shape[0]
    mod4 = _modulation(c, w_mod, b_mod).reshape(DEPTH, bsz, 1, 3 * D_MODEL)
    slopes = jnp.asarray(2.0 ** (-8.0 * np.arange(1, A_HEADS + 1) / A_HEADS), dtype=jnp.float32)
    for l in range(DEPTH):
        w_pack, wg_t = _pack_w_in(w_in[l])
        wm, wa, ws = _pack_w_out(w_out[l])
        gbias = jnp.broadcast_to(jnp.concatenate([m_b_i[l], m_b_f[l]])[:, None], (2 * M_HEADS, M_CHUNK))
        proj, gates = _inproj(x, mod4, l, g_pre[l][None], w_pack, wg_t,
                              _pad_heads(m_conv_w[l], 1), _pad_heads(m_conv_b[l][None], 1))
        ym = _mlstm(proj, gates, gbias, _pad_qk(m_wq[l]), _pad_qk(jnp.swapaxes(m_wk[l], 1, 2)),
                    _pad_heads(m_norm_g[l][None], 1), _pad_heads(m_skip[l][None], 1))
        ya = _swa(proj, slopes, a_sinks[l])
        ys = _stickbreak(proj)
        x = _outproj(x, ym, ya, ys, wm, wa, ws, g_post[l][None], mod4, l)
    return x
```

```python
import functools

import jax
import jax.numpy as jnp
import numpy as np
from jax import lax
from jax.experimental import pallas as pl
from jax.experimental.pallas import tpu as pltpu

D_MODEL = 1024
DEPTH = 2
M_HEADS = 4
M_DH = 96
M_W = M_HEADS * M_DH
M_CONV = 4
A_HEADS = 6
A_KV = 2
A_G = A_HEADS // A_KV
A_DH = 64
A_W = A_HEADS * A_DH
A_KVW = A_KV * A_DH
WINDOW = 128
S_HEADS = 4
S_DH = 64
S_W = S_HEADS * S_DH
S_BLOCK = 128
EPS = 1e-6

LANES = 128
M_CHUNK = 128
M_WP = M_HEADS * LANES
SB_TILE = 256
S_PAIRS = S_W // LANES

OFF_MU, OFF_MV, OFF_MZ = 0, M_WP, 2 * M_WP
OFF_AQ = 3 * M_WP
OFF_AZ = OFF_AQ + A_W
OFF_AK = OFF_AZ + A_W
OFF_AV = OFF_AK + A_KVW
OFF_SQ = OFF_AV + A_KVW
OFF_SK = OFF_SQ + S_W
OFF_SV = OFF_SK + S_W
OFF_SZ = OFF_SV + S_W
N_PACK = OFF_SZ + S_W

VMEM_LIMIT = 48 * 1024 * 1024

_NT = (((1,), (1,)), ((), ()))
LOG2E = 1.4426950408889634


def _log_sigmoid(x):
    return jnp.minimum(x, 0.0) - jnp.log(1.0 + jnp.exp(-jnp.abs(x)))


def _silu(x):
    return x / (1.0 + jnp.exp(-x))


def _params(n_axes):
    return pltpu.CompilerParams(dimension_semantics=("arbitrary",) * n_axes, vmem_limit_bytes=VMEM_LIMIT)


def _mod_kernel(c_ref, w_ref, b_ref, o_ref):
    c_act = _silu(c_ref[...]).astype(jnp.bfloat16)
    o_ref[0] = jnp.dot(c_act, w_ref[0].astype(jnp.bfloat16), preferred_element_type=jnp.float32) + b_ref[0]


def _modulation(c, w_mod, b_mod):
    depth, d, n = w_mod.shape
    bsz = c.shape[0]
    tn = 1024
    return pl.pallas_call(
        _mod_kernel,
        grid=(depth, n // tn),
        in_specs=[pl.BlockSpec((bsz, d), lambda l, j: (0, 0)),
                  pl.BlockSpec((1, d, tn), lambda l, j: (l, 0, j)),
                  pl.BlockSpec((1, 1, tn), lambda l, j: (l, 0, j))],
        out_specs=pl.BlockSpec((1, bsz, tn), lambda l, j: (l, 0, j)),
        out_shape=jax.ShapeDtypeStruct((depth, bsz, n), jnp.float32),
        compiler_params=_params(2),
        name="modulation",
    )(c, w_mod, b_mod.reshape(depth, 1, n))


def _inproj_kernel(x_ref, shift_ref, scale_ref, g_ref, w_ref, wg_ref, cw_ref, cb_ref,
                   proj_ref, gates_ref, conv_ref, *, tm, sub, nchunk):
    s = pl.program_id(1)
    f32 = jnp.float32

    @pl.when(s == 0)
    def _():
        conv_ref[0:8, :] = jnp.zeros((8, M_WP), f32)

    subs = [slice(i * sub, (i + 1) * sub) for i in range(tm // sub)]
    hs = []
    for r in subs:
        x = x_ref[0, r, :]
        ms = jnp.mean(x * x, axis=-1, keepdims=True)
        h = x * lax.rsqrt(ms + EPS) * g_ref[...]
        hs.append((h * (1.0 + scale_ref[...]) + shift_ref[...]).astype(jnp.bfloat16))

    for i, (r, h) in enumerate(zip(subs, hs)):
        gates_ref[0, :, r] = lax.dot_general(wg_ref[...], h, _NT, preferred_element_type=f32)

        base = 8 + i * sub
        conv_ref[base:base + sub, :] = jnp.dot(h, w_ref[:, OFF_MU:OFF_MU + M_WP], preferred_element_type=f32)
        acc = cb_ref[...] + cw_ref[M_CONV - 1:M_CONV, :] * conv_ref[base:base + sub, :]
        for j in range(M_CONV - 1):
            lag = M_CONV - 1 - j
            acc = acc + cw_ref[j:j + 1, :] * conv_ref[base - lag:base - lag + sub, :]
        proj_ref[0, r, OFF_MU:OFF_MU + M_WP] = _silu(acc).astype(jnp.bfloat16)

        for c0 in range(OFF_MV, N_PACK, nchunk):
            proj_ref[0, r, c0:c0 + nchunk] = jnp.dot(
                h, w_ref[:, c0:c0 + nchunk], preferred_element_type=f32).astype(jnp.bfloat16)

    conv_ref[0:8, :] = conv_ref[tm:tm + 8, :]


def _inproj(x, mod4, layer, g_pre, w_pack, wg_t, conv_w, conv_b):
    bsz, seq, d = x.shape
    tm = 1024
    kern = functools.partial(_inproj_kernel, tm=tm, sub=512, nchunk=512)
    const = lambda shape: pl.BlockSpec(shape, lambda b, s: (0,) * len(shape), pipeline_mode=pl.Buffered(1))
    return pl.pallas_call(
        kern,
        grid=(bsz, seq // tm),
        in_specs=[pl.BlockSpec((1, tm, d), lambda b, s: (b, s, 0)),
                  pl.BlockSpec((None, None, 1, d), lambda b, s: (layer, b, 0, 0)),
                  pl.BlockSpec((None, None, 1, d), lambda b, s: (layer, b, 0, 1)),
                  const((1, d)), const((d, N_PACK)), const((8, d)), const((M_CONV, M_WP)), const((1, M_WP))],
        out_specs=[pl.BlockSpec((1, tm, N_PACK), lambda b, s: (b, s, 0)),
                   pl.BlockSpec((1, 8, tm), lambda b, s: (b, 0, s))],
        out_shape=[jax.ShapeDtypeStruct((bsz, seq, N_PACK), jnp.bfloat16),
                   jax.ShapeDtypeStruct((bsz, 8, seq), jnp.float32)],
        scratch_shapes=[pltpu.VMEM((tm + 8, M_WP), jnp.float32)],
        compiler_params=_params(2),
        name="inproj",
    )(x, mod4, mod4, g_pre, w_pack, wg_t, conv_w, conv_b)


def _mlstm_kernel(cu_ref, v_ref, z_ref, gates_ref, gbias_ref, wq_ref, wkt_ref, hg_ref, skip_ref,
                  o_ref, st_ref, a_ref, b_ref, mp_ref, ml_ref, *, seq):
    L = M_CHUNK
    nchunks = seq // L
    f32 = jnp.float32
    bf16 = jnp.bfloat16
    lane8 = lax.broadcasted_iota(jnp.int32, (8, L), 1)
    row8 = lax.broadcasted_iota(jnp.int32, (8, L), 0)

    amax, gsum = [], []
    for c in range(nchunks):
        g = gates_ref[0, :, c * L:(c + 1) * L] + gbias_ref[...]
        cum = jnp.where(row8 >= M_HEADS, _log_sigmoid(g), 0.0)
        k = 1
        while k < L:
            cum = cum + jnp.where(lane8 >= k, pltpu.roll(cum, k, axis=1), 0.0)
            k *= 2
        b8 = pltpu.roll(cum, M_HEADS, axis=0)
        a8 = g - b8
        a_ref[:, c * L:(c + 1) * L] = a8
        b_ref[:, c * L:(c + 1) * L] = b8
        amax.append(jnp.broadcast_to(jnp.max(a8, axis=1, keepdims=True), (8, L)))
        gsum.append(jnp.broadcast_to(jnp.max(jnp.where(lane8 == L - 1, b8, -jnp.inf), axis=1, keepdims=True), (8, L)))
    m_prev = jnp.zeros((8, L), f32)
    for c in range(nchunks):
        m_last = jnp.maximum(m_prev, amax[c])
        mp_ref[:, c * L:(c + 1) * L] = m_prev
        ml_ref[:, c * L:(c + 1) * L] = m_last
        m_prev = gsum[c] + m_last

    st_ref[...] = jnp.zeros(st_ref.shape, f32)

    t_idx = lax.broadcasted_iota(jnp.int32, (L, L), 0)
    s_idx = lax.broadcasted_iota(jnp.int32, (L, L), 1)
    tri = s_idx <= t_idx
    eye = s_idx == t_idx
    ones = jnp.ones((L, LANES), bf16)
    scale = M_DH ** -0.5
    heads = range(M_HEADS)
    cols = [slice(h * LANES, (h + 1) * LANES) for h in heads]

    def chunk(c, carry):
        r0 = pl.multiple_of(c * L, L)
        rows = pl.ds(r0, L)
        cu = [cu_ref[0, rows, cols[h]] for h in heads]
        q = [(jnp.dot(cu[h], wq_ref[h], preferred_element_type=f32) * scale).astype(bf16) for h in heads]
        kt = [lax.dot_general(wkt_ref[h], cu[h], _NT, preferred_element_type=f32) for h in heads]
        st = [st_ref[h] for h in heads]
        qk = [jnp.dot(q[h], kt[h].astype(bf16), preferred_element_type=f32) for h in heads]
        qs = [jnp.dot(q[h], st[h].astype(bf16), preferred_element_type=f32) for h in heads]
        smat, mmb, bb = [], [], []
        for h in heads:
            a_mat = jnp.broadcast_to(a_ref[h:h + 1, rows], (L, L))
            m1 = jnp.max(jnp.where(tri, a_mat, -jnp.inf), axis=1, keepdims=True)
            mm = jnp.maximum(mp_ref[h:h + 1, rows], m1)
            p = jnp.where(tri, jnp.exp(a_mat - mm), 0.0)
            smat.append((qk[h] * p).astype(bf16))
            mmb.append(mm)
            b_mat = jnp.broadcast_to(b_ref[h:h + 1, rows], (L, L))
            bb.append(jnp.broadcast_to(jnp.sum(jnp.where(eye, b_mat, 0.0), axis=1, keepdims=True), (L, LANES)))
        v2 = [jnp.concatenate([v_ref[0, rows, cols[h]], ones], axis=1) for h in heads]
        sv = [jnp.dot(smat[h], v2[h], preferred_element_type=f32) for h in heads]
        upd = []
        for h in heads:
            ws = jnp.exp(a_ref[h:h + 1, rows] - ml_ref[h:h + 1, rows])
            upd.append(jnp.dot((kt[h] * ws).astype(bf16), v2[h], preferred_element_type=f32))
        for h in heads:
            mp_row = mp_ref[h:h + 1, rows]
            w_inter = jnp.exp(mp_row - mmb[h])
            num = w_inter * qs[h][:, :LANES] + sv[h][:, :LANES]
            den = w_inter * qs[h][:, LANES:] + sv[h][:, LANES:]
            hh = num / jnp.maximum(jnp.abs(den), jnp.exp(-(bb[h] + mmb[h])))
            hh = hh * lax.rsqrt(jnp.sum(hh * hh, axis=1, keepdims=True) * (1.0 / M_DH) + EPS)
            y = hh * hg_ref[:, cols[h]] + skip_ref[:, cols[h]] * cu[h].astype(f32)
            y = y * _silu(z_ref[0, rows, cols[h]].astype(f32))
            o_ref[0, rows, cols[h]] = y.astype(o_ref.dtype)
            cs = jnp.exp(mp_row - ml_ref[h:h + 1, rows])
            st_ref[h] = jnp.concatenate([cs, cs], axis=1) * st[h] + upd[h]
        return carry

    lax.fori_loop(0, nchunks, chunk, 0, unroll=4)


def _mlstm(proj, gates, gbias, wq, wkt, hn_g, skip):
    bsz, seq, _ = proj.shape
    kern = functools.partial(_mlstm_kernel, seq=seq)
    blk = lambda j: pl.BlockSpec((1, seq, M_WP), lambda b: (b, 0, j))
    full = lambda shape: pl.BlockSpec(shape, lambda b: (0,) * len(shape))
    rows = pltpu.VMEM((8, seq), jnp.float32)
    return pl.pallas_call(
        kern,
        grid=(bsz,),
        in_specs=[blk(OFF_MU // M_WP), blk(OFF_MV // M_WP), blk(OFF_MZ // M_WP),
                  pl.BlockSpec((1, 8, seq), lambda b: (b, 0, 0)),
                  full((8, M_CHUNK)), full((M_HEADS, LANES, LANES)), full((M_HEADS, LANES, LANES)),
                  full((1, M_WP)), full((1, M_WP))],
        out_specs=pl.BlockSpec((1, seq, M_WP), lambda b: (b, 0, 0)),
        out_shape=jax.ShapeDtypeStruct((bsz, seq, M_WP), jnp.bfloat16),
        scratch_shapes=[pltpu.VMEM((M_HEADS, LANES, 2 * LANES), jnp.float32), rows, rows, rows, rows],
        compiler_params=_params(1),
        name="mlstm",
    )(proj, proj, proj, gates, gbias, wq, wkt, hn_g, skip)


def _swa_kernel(q_ref, z_ref, k_ref, v_ref, slope_ref, sink_ref, o_ref, *, seq):
    f32 = jnp.float32
    bf16 = jnp.bfloat16
    W = WINDOW
    ti = lax.broadcasted_iota(jnp.int32, (W, W), 0)
    si = lax.broadcasted_iota(jnp.int32, (W, W), 1)
    cur = si <= ti
    neg_rel = -jnp.where(cur, ti - si, W + ti - si).astype(f32)
    lane = lax.broadcasted_iota(jnp.int32, (W, LANES), 1)
    lane2 = lax.broadcasted_iota(jnp.int32, (2 * W, LANES), 1)
    halves = (lane < A_DH, lane >= A_DH)
    pairs = [(g, kv) for g in range(A_G) for kv in range(A_KV)]

    def block(n, carry):
        rows = pl.ds(pl.multiple_of(n * W, W), W)
        prev = pl.ds(pl.multiple_of(jnp.maximum(n - 1, 0) * W, W), W)
        bias = jnp.where((cur.astype(jnp.int32) + n) > 0, neg_rel, -jnp.inf)
        kk = jnp.concatenate([k_ref[0, rows, :], k_ref[0, prev, :]], axis=0)
        vv = jnp.concatenate([v_ref[0, rows, :], v_ref[0, prev, :]], axis=0)
        vms = [jnp.where(lane2 < A_DH, vv, jnp.zeros_like(vv)), jnp.where(lane2 >= A_DH, vv, jnp.zeros_like(vv))]
        q_all = q_ref[0, rows, :]
        q_all = q_all * jnp.asarray(A_DH ** -0.5, q_all.dtype)
        sc = []
        for g, kv in pairs:
            qg = q_all[:, g * LANES:(g + 1) * LANES]
            qm = jnp.where(halves[kv], qg, jnp.zeros_like(qg))
            sc.append(lax.dot_general(qm, kk, _NT, preferred_element_type=f32))
        pc = []
        for i, (g, kv) in enumerate(pairs):
            head = kv * A_G + g
            s2 = jnp.where(cur, sc[i][:, :W], sc[i][:, W:]) + slope_ref[head] * bias
            sink = sink_ref[head]
            mx = jnp.maximum(jnp.max(s2, axis=1, keepdims=True), sink)
            e = jnp.exp(s2 - mx)
            den = jnp.sum(e, axis=1, keepdims=True) + jnp.exp(sink - mx)
            probs = (e * (1.0 / den)).astype(bf16)
            zero = jnp.zeros_like(probs)
            pc.append(jnp.concatenate([jnp.where(cur, probs, zero), jnp.where(cur, zero, probs)], axis=1))
        for g in range(A_G):
            cols = slice(g * LANES, (g + 1) * LANES)
            o = (jnp.dot(pc[2 * g], vms[0], preferred_element_type=f32)
                 + jnp.dot(pc[2 * g + 1], vms[1], preferred_element_type=f32))
            o_ref[0, rows, cols] = (o * _silu(z_ref[0, rows, cols].astype(f32))).astype(o_ref.dtype)
        return carry

    lax.fori_loop(0, seq // W, block, 0, unroll=4)


def _swa(proj, slopes, sinks):
    bsz, seq, _ = proj.shape
    kern = functools.partial(_swa_kernel, seq=seq)
    smem = pl.BlockSpec(memory_space=pltpu.SMEM)
    full = lambda off, w: pl.BlockSpec((1, seq, w), lambda b: (b, 0, off // w))
    return pl.pallas_call(
        kern,
        grid=(bsz,),
        in_specs=[full(OFF_AQ, A_W), full(OFF_AZ, A_W), full(OFF_AK, A_KVW), full(OFF_AV, A_KVW), smem, smem],
        out_specs=pl.BlockSpec((1, seq, A_W), lambda b: (b, 0, 0)),
        out_shape=jax.ShapeDtypeStruct((bsz, seq, A_W), jnp.bfloat16),
        compiler_params=_params(1),
        name="swa",
    )(proj, proj, proj, proj, slopes, sinks)


def _sb_kernel(q_ref, z_ref, k_ref, v_ref, o_ref, acc_ref, run_ref, zz_ref, a_ref, *, seq):
    f32 = jnp.float32
    bf16 = jnp.bfloat16
    T = SB_TILE
    lane = lax.broadcasted_iota(jnp.int32, (T, LANES), 1)
    halves = (lane < S_DH, lane >= S_DH)
    ti = lax.broadcasted_iota(jnp.int32, (T, T), 0)
    si = lax.broadcasted_iota(jnp.int32, (T, T), 1)
    strict = si < ti
    usum = jnp.where(ti > si, 1.0, 0.0).astype(bf16)
    heads = range(S_HEADS)
    cols = [slice((h // 2) * LANES, (h // 2 + 1) * LANES) for h in heads]

    def qblock(n, carry):
        r0 = pl.multiple_of(n * T, T)
        q = q_ref[0, pl.ds(r0, T), :]
        q = q * jnp.asarray(S_DH ** -0.5, q.dtype)
        qm = [jnp.where(halves[h % 2], q[:, cols[h]], jnp.zeros((T, LANES), q.dtype)) for h in heads]
        acc_ref[...] = jnp.zeros(acc_ref.shape, f32)
        run_ref[...] = jnp.zeros(run_ref.shape, f32)

        def score(j):
            k0 = pl.multiple_of(j * T, T)
            kb = k_ref[0, pl.ds(k0, T), :]
            for h in heads:
                zz_ref[h] = lax.dot_general(qm[h], kb[:, cols[h]], _NT, preferred_element_type=f32)

        def apply(j):
            k0 = pl.multiple_of(j * T, T)
            vb = v_ref[0, pl.ds(k0, T), :]
            for p in range(S_PAIRS):
                pv = None
                for h in (2 * p, 2 * p + 1):
                    vm = jnp.where(halves[h % 2], vb[:, cols[h]], jnp.zeros((T, LANES), vb.dtype))
                    d = jnp.dot(a_ref[h], vm, preferred_element_type=f32)
                    pv = d if pv is None else pv + d
                acc_ref[:, cols[2 * p]] += pv

        def weights(j_next, diag):
            ls, lk = [], []
            for h in heads:
                zz = zz_ref[h]
                soft = jnp.log(1.0 + jnp.exp2(jnp.abs(zz) * (-LOG2E)))
                ls.append(jnp.minimum(zz, 0.0) - soft)
                lkh = ls[h] - zz
                lk.append(jnp.where(strict, lkh, 0.0) if diag else lkh)
            suf = [jnp.dot(lk[h].astype(bf16), usum, preferred_element_type=f32) for h in heads]
            score(j_next)
            for h in heads:
                run = run_ref[h]
                ah = jnp.exp2((ls[h] + suf[h] + jnp.concatenate([run] * (T // LANES), axis=1)) * LOG2E)
                a_ref[h] = (jnp.where(strict, ah, 0.0) if diag else ah).astype(bf16)
                run_ref[h] = run + jnp.sum(lk[h], axis=1, keepdims=True)

        score(n)
        weights(jnp.maximum(n - 1, 0), True)

        def body(i, c):
            apply(n - i + 1)
            weights(jnp.maximum(n - i - 1, 0), False)
            return c

        lax.fori_loop(1, n + 1, body, 0)
        apply(0)
        zg = z_ref[0, pl.ds(r0, T), :].astype(f32)
        o_ref[0, pl.ds(r0, T), :] = (acc_ref[...] * _silu(zg)).astype(o_ref.dtype)
        return carry

    for n in range(seq // T):
        qblock(n, 0)


def _stickbreak(proj):
    bsz, seq, _ = proj.shape
    kern = functools.partial(_sb_kernel, seq=seq)
    full = lambda off: pl.BlockSpec((1, seq, S_W), lambda b: (b, 0, off // S_W))
    return pl.pallas_call(
        kern,
        grid=(bsz,),
        in_specs=[full(OFF_SQ), full(OFF_SZ), full(OFF_SK), full(OFF_SV)],
        out_specs=pl.BlockSpec((1, seq, S_W), lambda b: (b, 0, 0)),
        out_shape=jax.ShapeDtypeStruct((bsz, seq, S_W), jnp.bfloat16),
        scratch_shapes=[pltpu.VMEM((SB_TILE, S_W), jnp.float32),
                        pltpu.VMEM((S_HEADS, SB_TILE, LANES), jnp.float32),
                        pltpu.VMEM((S_HEADS, SB_TILE, SB_TILE), jnp.float32),
                        pltpu.VMEM((S_HEADS, SB_TILE, SB_TILE), jnp.bfloat16)],
        compiler_params=_params(1),
        name="stickbreak",
    )(proj, proj, proj, proj)


def _outproj_kernel(x_ref, ym_ref, ya_ref, ys_ref, wm_ref, wa_ref, ws_ref, g_ref, gate_ref, o_ref, *, tm, sub):
    f32 = jnp.float32
    for i in range(tm // sub):
        r = slice(i * sub, (i + 1) * sub)
        y = (jnp.dot(ym_ref[0, r, :], wm_ref[...], preferred_element_type=f32)
             + jnp.dot(ya_ref[0, r, :], wa_ref[...], preferred_element_type=f32)
             + jnp.dot(ys_ref[0, r, :], ws_ref[...], preferred_element_type=f32))
        yn = y * lax.rsqrt(jnp.mean(y * y, axis=-1, keepdims=True) + EPS) * g_ref[...]
        o_ref[0, r, :] = x_ref[0, r, :] + gate_ref[...] * yn


def _outproj(x, ym, ya, ys, wm, wa, ws, g_post, mod4, layer):
    bsz, seq, d = x.shape
    tm = 1024
    kern = functools.partial(_outproj_kernel, tm=tm, sub=256)
    row = lambda w: pl.BlockSpec((1, tm, w), lambda b, s: (b, s, 0))
    const = lambda shape: pl.BlockSpec(shape, lambda b, s: (0,) * len(shape), pipeline_mode=pl.Buffered(1))
    return pl.pallas_call(
        kern,
        grid=(bsz, seq // tm),
        in_specs=[row(d), row(M_WP), row(A_W), row(S_W),
                  const((M_WP, d)), const((A_W, d)), const((S_W, d)), const((1, d)),
                  pl.BlockSpec((None, None, 1, d), lambda b, s: (layer, b, 0, 2))],
        out_specs=row(d),
        out_shape=jax.ShapeDtypeStruct((bsz, seq, d), jnp.float32),
        compiler_params=_params(2),
        name="outproj",
    )(x, ym, ya, ys, wm, wa, ws, g_post, mod4)


def _pad_heads(w, axis):
    shape = list(w.shape)
    shape[axis:axis + 1] = [M_HEADS, M_DH]
    w = w.reshape(shape)
    pad = [(0, 0)] * w.ndim
    pad[axis + 1] = (0, LANES - M_DH)
    w = jnp.pad(w, pad)
    shape[axis:axis + 2] = [M_WP]
    return w.reshape(shape)


def _pair_heads(w, axis):
    shape = list(w.shape)
    shape[axis:axis + 1] = [A_KV, A_G, A_DH]
    w = jnp.swapaxes(w.reshape(shape), axis, axis + 1)
    shape[axis:axis + 3] = [A_W]
    return w.reshape(shape)


def _pack_w_in(w):
    o = np.cumsum([0, M_W, M_W, M_HEADS, M_HEADS, M_W, A_W, A_KVW, A_KVW, A_W, S_W, S_W, S_W, S_W])
    seg = lambda i: w[:, int(o[i]):int(o[i + 1])]
    packed = jnp.concatenate(
        [_pad_heads(seg(0), 1), _pad_heads(seg(1), 1), _pad_heads(seg(4), 1),
         _pair_heads(seg(5), 1), _pair_heads(seg(8), 1), seg(6), seg(7),
         seg(9), seg(10), seg(11), seg(12)], axis=1).astype(jnp.bfloat16)
    gates_t = jnp.concatenate([seg(2), seg(3)], axis=1).T.astype(jnp.bfloat16)
    return packed, gates_t


def _pack_w_out(w):
    wm = _pad_heads(w[:M_W], 0)
    wa = _pair_heads(w[M_W:M_W + A_W], 0)
    ws = w[M_W + A_W:]
    return wm.astype(jnp.bfloat16), wa.astype(jnp.bfloat16), ws.astype(jnp.bfloat16)


def _pad_qk(w):
    return jnp.pad(w, ((0, 0), (0, LANES - M_DH), (0, LANES - M_DH))).astype(jnp.bfloat16)


def kernel(x, c, w_mod, b_mod, g_pre, g_post, w_in, m_conv_w, m_conv_b, m_wq, m_wk, m_b_i, m_b_f,
           m_norm_g, m_skip, a_sinks, w_out):
    bsz = x.shape[0]
    mod4 = _modulation(c, w_mod, b_mod).reshape(DEPTH, bsz, 1, 3 * D_MODEL)
    slopes = jnp.asarray(2.0 ** (-8.0 * np.arange(1, A_HEADS + 1) / A_HEADS), dtype=jnp.float32)
    for l in range(DEPTH):
        w_pack, wg_t = _pack_w_in(w_in[l])
        wm, wa, ws = _pack_w_out(w_out[l])
        gbias = jnp.broadcast_to(jnp.concatenate([m_b_i[l], m_b_f[l]])[:, None], (2 * M_HEADS, M_CHUNK))
        proj, gates = _inproj(x, mod4, l, g_pre[l][None], w_pack, wg_t,
                              _pad_heads(m_conv_w[l], 1), _pad_heads(m_conv_b[l][None], 1))
        ym = _mlstm(proj, gates, gbias, _pad_qk(m_wq[l]), _pad_qk(jnp.swapaxes(m_wk[l], 1, 2)),
                    _pad_heads(m_norm_g[l][None], 1), _pad_heads(m_skip[l][None], 1))
        ya = _swa(proj, slopes, a_sinks[l])
        ys = _stickbreak(proj)
        x = _outproj(x, ym, ya, ys, wm, wa, ws, g_post[l][None], mod4, l)
    return x
```

```python
import functools

import jax
import jax.numpy as jnp
import numpy as np
from jax import lax
from jax.experimental import pallas as pl
from jax.experimental.pallas import tpu as pltpu

D_MODEL = 1024
DEPTH = 2
M_HEADS = 4
M_DH = 96
M_W = M_HEADS * M_DH
M_CONV = 4
A_HEADS = 6
A_KV = 2
A_G = A_HEADS // A_KV
A_DH = 64
A_W = A_HEADS * A_DH
A_KVW = A_KV * A_DH
WINDOW = 128
S_HEADS = 4
S_DH = 64
S_W = S_HEADS * S_DH
S_BLOCK = 128
EPS = 1e-6

LANES = 128
M_CHUNK = 128
M_WP = M_HEADS * LANES
SB_TILE = 256
S_PAIRS = S_W // LANES
SB_DEAD = -110.0

OFF_MU, OFF_MV, OFF_MZ = 0, M_WP, 2 * M_WP
OFF_AQ = 3 * M_WP
OFF_AZ = OFF_AQ + A_W
OFF_AK = OFF_AZ + A_W
OFF_AV = OFF_AK + A_KVW
OFF_SQ = OFF_AV + A_KVW
OFF_SK = OFF_SQ + S_W
OFF_SV = OFF_SK + S_W
OFF_SZ = OFF_SV + S_W
N_PACK = OFF_SZ + S_W

VMEM_LIMIT = 48 * 1024 * 1024

_NT = (((1,), (1,)), ((), ()))
LOG2E = 1.4426950408889634


def _log_sigmoid(x):
    return jnp.minimum(x, 0.0) - jnp.log(1.0 + jnp.exp(-jnp.abs(x)))


def _silu(x):
    return x / (1.0 + jnp.exp(-x))


def _params(n_axes):
    return pltpu.CompilerParams(dimension_semantics=("arbitrary",) * n_axes, vmem_limit_bytes=VMEM_LIMIT)


def _mod_kernel(c_ref, w_ref, b_ref, o_ref):
    c_act = _silu(c_ref[...]).astype(jnp.bfloat16)
    o_ref[0] = jnp.dot(c_act, w_ref[0].astype(jnp.bfloat16), preferred_element_type=jnp.float32) + b_ref[0]


def _modulation(c, w_mod, b_mod):
    depth, d, n = w_mod.shape
    bsz = c.shape[0]
    tn = 1024
    return pl.pallas_call(
        _mod_kernel,
        grid=(depth, n // tn),
        in_specs=[pl.BlockSpec((bsz, d), lambda l, j: (0, 0)),
                  pl.BlockSpec((1, d, tn), lambda l, j: (l, 0, j)),
                  pl.BlockSpec((1, 1, tn), lambda l, j: (l, 0, j))],
        out_specs=pl.BlockSpec((1, bsz, tn), lambda l, j: (l, 0, j)),
        out_shape=jax.ShapeDtypeStruct((depth, bsz, n), jnp.float32),
        compiler_params=_params(2),
        name="modulation",
    )(c, w_mod, b_mod.reshape(depth, 1, n))


def _inproj_kernel(x_ref, shift_ref, scale_ref, g_ref, w_ref, wg_ref, cw_ref, cb_ref,
                   proj_ref, gates_ref, conv_ref, *, tm, sub, nchunk):
    s = pl.program_id(1)
    f32 = jnp.float32

    @pl.when(s == 0)
    def _():
        conv_ref[0:8, :] = jnp.zeros((8, M_WP), f32)

    subs = [slice(i * sub, (i + 1) * sub) for i in range(tm // sub)]
    hs = []
    for r in subs:
        x = x_ref[0, r, :]
        ms = jnp.mean(x * x, axis=-1, keepdims=True)
        h = x * lax.rsqrt(ms + EPS) * g_ref[...]
        hs.append((h * (1.0 + scale_ref[...]) + shift_ref[...]).astype(jnp.bfloat16))

    for i, (r, h) in enumerate(zip(subs, hs)):
        gates_ref[0, :, r] = lax.dot_general(wg_ref[...], h, _NT, preferred_element_type=f32)

        base = 8 + i * sub
        conv_ref[base:base + sub, :] = jnp.dot(h, w_ref[:, OFF_MU:OFF_MU + M_WP], preferred_element_type=f32)
        acc = cb_ref[...] + cw_ref[M_CONV - 1:M_CONV, :] * conv_ref[base:base + sub, :]
        for j in range(M_CONV - 1):
            lag = M_CONV - 1 - j
            acc = acc + cw_ref[j:j + 1, :] * conv_ref[base - lag:base - lag + sub, :]
        proj_ref[0, r, OFF_MU:OFF_MU + M_WP] = _silu(acc).astype(jnp.bfloat16)

        for c0 in range(OFF_MV, N_PACK, nchunk):
            proj_ref[0, r, c0:c0 + nchunk] = jnp.dot(
                h, w_ref[:, c0:c0 + nchunk], preferred_element_type=f32).astype(jnp.bfloat16)

    conv_ref[0:8, :] = conv_ref[tm:tm + 8, :]


def _inproj(x, mod4, layer, g_pre, w_pack, wg_t, conv_w, conv_b):
    bsz, seq, d = x.shape
    tm = 1024
    kern = functools.partial(_inproj_kernel, tm=tm, sub=512, nchunk=512)
    const = lambda shape: pl.BlockSpec(shape, lambda b, s: (0,) * len(shape), pipeline_mode=pl.Buffered(1))
    return pl.pallas_call(
        kern,
        grid=(bsz, seq // tm),
        in_specs=[pl.BlockSpec((1, tm, d), lambda b, s: (b, s, 0)),
                  pl.BlockSpec((None, None, 1, d), lambda b, s: (layer, b, 0, 0)),
                  pl.BlockSpec((None, None, 1, d), lambda b, s: (layer, b, 0, 1)),
                  const((1, d)), const((d, N_PACK)), const((8, d)), const((M_CONV, M_WP)), const((1, M_WP))],
        out_specs=[pl.BlockSpec((1, tm, N_PACK), lambda b, s: (b, s, 0)),
                   pl.BlockSpec((1, 8, tm), lambda b, s: (b, 0, s))],
        out_shape=[jax.ShapeDtypeStruct((bsz, seq, N_PACK), jnp.bfloat16),
                   jax.ShapeDtypeStruct((bsz, 8, seq), jnp.float32)],
        scratch_shapes=[pltpu.VMEM((tm + 8, M_WP), jnp.float32)],
        compiler_params=_params(2),
        name="inproj",
    )(x, mod4, mod4, g_pre, w_pack, wg_t, conv_w, conv_b)


def _mlstm_kernel(cu_ref, v_ref, z_ref, gates_ref, gbias_ref, wq_ref, wkt_ref, hg_ref, skip_ref,
                  o_ref, st_ref, a_ref, b_ref, mp_ref, ml_ref, *, seq):
    L = M_CHUNK
    nchunks = seq // L
    f32 = jnp.float32
    bf16 = jnp.bfloat16
    lane8 = lax.broadcasted_iota(jnp.int32, (8, L), 1)
    row8 = lax.broadcasted_iota(jnp.int32, (8, L), 0)

    amax, gsum = [], []
    for c in range(nchunks):
        g = gates_ref[0, :, c * L:(c + 1) * L] + gbias_ref[...]
        cum = jnp.where(row8 >= M_HEADS, _log_sigmoid(g), 0.0)
        k = 1
        while k < L:
            cum = cum + jnp.where(lane8 >= k, pltpu.roll(cum, k, axis=1), 0.0)
            k *= 2
        b8 = pltpu.roll(cum, M_HEADS, axis=0)
        a8 = g - b8
        a_ref[:, c * L:(c + 1) * L] = a8
        b_ref[:, c * L:(c + 1) * L] = b8
        amax.append(jnp.broadcast_to(jnp.max(a8, axis=1, keepdims=True), (8, L)))
        gsum.append(jnp.broadcast_to(jnp.max(jnp.where(lane8 == L - 1, b8, -jnp.inf), axis=1, keepdims=True), (8, L)))
    m_prev = jnp.zeros((8, L), f32)
    for c in range(nchunks):
        m_last = jnp.maximum(m_prev, amax[c])
        mp_ref[:, c * L:(c + 1) * L] = m_prev
        ml_ref[:, c * L:(c + 1) * L] = m_last
        m_prev = gsum[c] + m_last

    st_ref[...] = jnp.zeros(st_ref.shape, f32)

    t_idx = lax.broadcasted_iota(jnp.int32, (L, L), 0)
    s_idx = lax.broadcasted_iota(jnp.int32, (L, L), 1)
    tri = s_idx <= t_idx
    eye = s_idx == t_idx
    ones = jnp.ones((L, LANES), bf16)
    scale = M_DH ** -0.5
    heads = range(M_HEADS)
    cols = [slice(h * LANES, (h + 1) * LANES) for h in heads]

    def chunk(c, carry):
        r0 = pl.multiple_of(c * L, L)
        rows = pl.ds(r0, L)
        cu = [cu_ref[0, rows, cols[h]] for h in heads]
        q = [(jnp.dot(cu[h], wq_ref[h], preferred_element_type=f32) * scale).astype(bf16) for h in heads]
        kt = [lax.dot_general(wkt_ref[h], cu[h], _NT, preferred_element_type=f32) for h in heads]
        st = [st_ref[h] for h in heads]
        qk = [jnp.dot(q[h], kt[h].astype(bf16), preferred_element_type=f32) for h in heads]
        qs = [jnp.dot(q[h], st[h].astype(bf16), preferred_element_type=f32) for h in heads]
        smat, mmb, bb = [], [], []
        for h in heads:
            a_mat = jnp.broadcast_to(a_ref[h:h + 1, rows], (L, L))
            m1 = jnp.max(jnp.where(tri, a_mat, -jnp.inf), axis=1, keepdims=True)
            mm = jnp.maximum(mp_ref[h:h + 1, rows], m1)
            p = jnp.where(tri, jnp.exp(a_mat - mm), 0.0)
            smat.append((qk[h] * p).astype(bf16))
            mmb.append(mm)
            b_mat = jnp.broadcast_to(b_ref[h:h + 1, rows], (L, L))
            bb.append(jnp.broadcast_to(jnp.sum(jnp.where(eye, b_mat, 0.0), axis=1, keepdims=True), (L, LANES)))
        v2 = [jnp.concatenate([v_ref[0, rows, cols[h]], ones], axis=1) for h in heads]
        sv = [jnp.dot(smat[h], v2[h], preferred_element_type=f32) for h in heads]
        upd = []
        for h in heads:
            ws = jnp.exp(a_ref[h:h + 1, rows] - ml_ref[h:h + 1, rows])
            upd.append(jnp.dot((kt[h] * ws).astype(bf16), v2[h], preferred_element_type=f32))
        for h in heads:
            mp_row = mp_ref[h:h + 1, rows]
            w_inter = jnp.exp(mp_row - mmb[h])
            num = w_inter * qs[h][:, :LANES] + sv[h][:, :LANES]
            den = w_inter * qs[h][:, LANES:] + sv[h][:, LANES:]
            hh = num / jnp.maximum(jnp.abs(den), jnp.exp(-(bb[h] + mmb[h])))
            hh = hh * lax.rsqrt(jnp.sum(hh * hh, axis=1, keepdims=True) * (1.0 / M_DH) + EPS)
            y = hh * hg_ref[:, cols[h]] + skip_ref[:, cols[h]] * cu[h].astype(f32)
            y = y * _silu(z_ref[0, rows, cols[h]].astype(f32))
            o_ref[0, rows, cols[h]] = y.astype(o_ref.dtype)
            cs = jnp.exp(mp_row - ml_ref[h:h + 1, rows])
            st_ref[h] = jnp.concatenate([cs, cs], axis=1) * st[h] + upd[h]
        return carry

    lax.fori_loop(0, nchunks, chunk, 0, unroll=4)


def _mlstm(proj, gates, gbias, wq, wkt, hn_g, skip):
    bsz, seq, _ = proj.shape
    kern = functools.partial(_mlstm_kernel, seq=seq)
    blk = lambda j: pl.BlockSpec((1, seq, M_WP), lambda b: (b, 0, j))
    full = lambda shape: pl.BlockSpec(shape, lambda b: (0,) * len(shape))
    rows = pltpu.VMEM((8, seq), jnp.float32)
    return pl.pallas_call(
        kern,
        grid=(bsz,),
        in_specs=[blk(OFF_MU // M_WP), blk(OFF_MV // M_WP), blk(OFF_MZ // M_WP),
                  pl.BlockSpec((1, 8, seq), lambda b: (b, 0, 0)),
                  full((8, M_CHUNK)), full((M_HEADS, LANES, LANES)), full((M_HEADS, LANES, LANES)),
                  full((1, M_WP)), full((1, M_WP))],
        out_specs=pl.BlockSpec((1, seq, M_WP), lambda b: (b, 0, 0)),
        out_shape=jax.ShapeDtypeStruct((bsz, seq, M_WP), jnp.bfloat16),
        scratch_shapes=[pltpu.VMEM((M_HEADS, LANES, 2 * LANES), jnp.float32), rows, rows, rows, rows],
        compiler_params=_params(1),
        name="mlstm",
    )(proj, proj, proj, gates, gbias, wq, wkt, hn_g, skip)


def _swa_kernel(q_ref, z_ref, k_ref, v_ref, slope_ref, sink_ref, o_ref, *, seq):
    f32 = jnp.float32
    bf16 = jnp.bfloat16
    W = WINDOW
    ti = lax.broadcasted_iota(jnp.int32, (W, W), 0)
    si = lax.broadcasted_iota(jnp.int32, (W, W), 1)
    cur = si <= ti
    neg_rel = -jnp.where(cur, ti - si, W + ti - si).astype(f32)
    lane = lax.broadcasted_iota(jnp.int32, (W, LANES), 1)
    lane2 = lax.broadcasted_iota(jnp.int32, (2 * W, LANES), 1)
    halves = (lane < A_DH, lane >= A_DH)
    pairs = [(g, kv) for g in range(A_G) for kv in range(A_KV)]

    def block(n, carry):
        rows = pl.ds(pl.multiple_of(n * W, W), W)
        prev = pl.ds(pl.multiple_of(jnp.maximum(n - 1, 0) * W, W), W)
        bias = jnp.where((cur.astype(jnp.int32) + n) > 0, neg_rel, -jnp.inf)
        kk = jnp.concatenate([k_ref[0, rows, :], k_ref[0, prev, :]], axis=0)
        vv = jnp.concatenate([v_ref[0, rows, :], v_ref[0, prev, :]], axis=0)
        vms = [jnp.where(lane2 < A_DH, vv, jnp.zeros_like(vv)), jnp.where(lane2 >= A_DH, vv, jnp.zeros_like(vv))]
        q_all = q_ref[0, rows, :]
        q_all = q_all * jnp.asarray(A_DH ** -0.5, q_all.dtype)
        sc = []
        for g, kv in pairs:
            qg = q_all[:, g * LANES:(g + 1) * LANES]
            qm = jnp.where(halves[kv], qg, jnp.zeros_like(qg))
            sc.append(lax.dot_general(qm, kk, _NT, preferred_element_type=f32))
        pc = []
        for i, (g, kv) in enumerate(pairs):
            head = kv * A_G + g
            s2 = jnp.where(cur, sc[i][:, :W], sc[i][:, W:]) + slope_ref[head] * bias
            sink = sink_ref[head]
            mx = jnp.maximum(jnp.max(s2, axis=1, keepdims=True), sink)
            e = jnp.exp(s2 - mx)
            den = jnp.sum(e, axis=1, keepdims=True) + jnp.exp(sink - mx)
            probs = (e * (1.0 / den)).astype(bf16)
            zero = jnp.zeros_like(probs)
            pc.append(jnp.concatenate([jnp.where(cur, probs, zero), jnp.where(cur, zero, probs)], axis=1))
        for g in range(A_G):
            cols = slice(g * LANES, (g + 1) * LANES)
            o = (jnp.dot(pc[2 * g], vms[0], preferred_element_type=f32)
                 + jnp.dot(pc[2 * g + 1], vms[1], preferred_element_type=f32))
            o_ref[0, rows, cols] = (o * _silu(z_ref[0, rows, cols].astype(f32))).astype(o_ref.dtype)
        return carry

    lax.fori_loop(0, seq // W, block, 0, unroll=4)


def _swa(proj, slopes, sinks):
    bsz, seq, _ = proj.shape
    kern = functools.partial(_swa_kernel, seq=seq)
    smem = pl.BlockSpec(memory_space=pltpu.SMEM)
    full = lambda off, w: pl.BlockSpec((1, seq, w), lambda b: (b, 0, off // w))
    return pl.pallas_call(
        kern,
        grid=(bsz,),
        in_specs=[full(OFF_AQ, A_W), full(OFF_AZ, A_W), full(OFF_AK, A_KVW), full(OFF_AV, A_KVW), smem, smem],
        out_specs=pl.BlockSpec((1, seq, A_W), lambda b: (b, 0, 0)),
        out_shape=jax.ShapeDtypeStruct((bsz, seq, A_W), jnp.bfloat16),
        compiler_params=_params(1),
        name="swa",
    )(proj, proj, proj, proj, slopes, sinks)


def _sb_kernel(q_ref, z_ref, k_ref, v_ref, o_ref, acc_ref, run_ref, zz_ref, a_ref, *, seq):
    f32 = jnp.float32
    bf16 = jnp.bfloat16
    T = SB_TILE
    lane = lax.broadcasted_iota(jnp.int32, (T, LANES), 1)
    halves = (lane < S_DH, lane >= S_DH)
    ti = lax.broadcasted_iota(jnp.int32, (T, T), 0)
    si = lax.broadcasted_iota(jnp.int32, (T, T), 1)
    strict = si < ti
    usum = jnp.where(ti > si, 1.0, 0.0).astype(bf16)
    heads = range(S_HEADS)
    cols = [slice((h // 2) * LANES, (h // 2 + 1) * LANES) for h in heads]

    def qblock(n, carry):
        r0 = pl.multiple_of(n * T, T)
        q = q_ref[0, pl.ds(r0, T), :]
        q = q * jnp.asarray(S_DH ** -0.5, q.dtype)
        qm = [jnp.where(halves[h % 2], q[:, cols[h]], jnp.zeros((T, LANES), q.dtype)) for h in heads]
        acc_ref[...] = jnp.zeros(acc_ref.shape, f32)
        run_ref[...] = jnp.zeros(run_ref.shape, f32)

        def score(j):
            k0 = pl.multiple_of(j * T, T)
            kb = k_ref[0, pl.ds(k0, T), :]
            for h in heads:
                zz_ref[h] = lax.dot_general(qm[h], kb[:, cols[h]], _NT, preferred_element_type=f32)

        def apply(j):
            k0 = pl.multiple_of(j * T, T)
            vb = v_ref[0, pl.ds(k0, T), :]
            for p in range(S_PAIRS):
                pv = None
                for h in (2 * p, 2 * p + 1):
                    vm = jnp.where(halves[h % 2], vb[:, cols[h]], jnp.zeros((T, LANES), vb.dtype))
                    d = jnp.dot(a_ref[h], vm, preferred_element_type=f32)
                    pv = d if pv is None else pv + d
                acc_ref[:, cols[2 * p]] += pv

        def weights(j_next, diag):
            ls, lk = [], []
            for h in heads:
                zz = zz_ref[h]
                soft = jnp.log(1.0 + jnp.exp2(jnp.abs(zz) * (-LOG2E)))
                ls.append(jnp.minimum(zz, 0.0) - soft)
                lkh = ls[h] - zz
                lk.append(jnp.where(strict, lkh, 0.0) if diag else lkh)
            suf = [jnp.dot(lk[h].astype(bf16), usum, preferred_element_type=f32) for h in heads]
            runs = [run_ref[h] for h in heads]
            new_runs = [runs[h] + jnp.sum(lk[h], axis=1, keepdims=True) for h in heads]
            live = jnp.max(functools.reduce(jnp.maximum, new_runs)) > SB_DEAD
            score(j_next)
            for h in heads:
                ah = jnp.exp2((ls[h] + suf[h] + jnp.concatenate([runs[h]] * (T // LANES), axis=1)) * LOG2E)
                a_ref[h] = (jnp.where(strict, ah, 0.0) if diag else ah).astype(bf16)
                run_ref[h] = new_runs[h]
            return live.astype(jnp.int32)

        score(n)
        live0 = weights(max(n - 1, 0), True)

        def cond(c):
            i, live = c
            return jnp.logical_and(i <= n, live > 0)

        def body(c):
            i, _ = c
            apply(n - i + 1)
            return i + 1, weights(jnp.maximum(n - i - 1, 0), False)

        i_end, _ = lax.while_loop(cond, body, (jnp.int32(1), live0))
        apply(n - i_end + 1)
        zg = z_ref[0, pl.ds(r0, T), :].astype(f32)
        o_ref[0, pl.ds(r0, T), :] = (acc_ref[...] * _silu(zg)).astype(o_ref.dtype)
        return carry

    for n in range(seq // T):
        qblock(n, 0)


def _stickbreak(proj):
    bsz, seq, _ = proj.shape
    kern = functools.partial(_sb_kernel, seq=seq)
    full = lambda off: pl.BlockSpec((1, seq, S_W), lambda b: (b, 0, off // S_W))
    return pl.pallas_call(
        kern,
        grid=(bsz,),
        in_specs=[full(OFF_SQ), full(OFF_SZ), full(OFF_SK), full(OFF_SV)],
        out_specs=pl.BlockSpec((1, seq, S_W), lambda b: (b, 0, 0)),
        out_shape=jax.ShapeDtypeStruct((bsz, seq, S_W), jnp.bfloat16),
        scratch_shapes=[pltpu.VMEM((SB_TILE, S_W), jnp.float32),
                        pltpu.VMEM((S_HEADS, SB_TILE, LANES), jnp.float32),
                        pltpu.VMEM((S_HEADS, SB_TILE, SB_TILE), jnp.float32),
                        pltpu.VMEM((S_HEADS, SB_TILE, SB_TILE), jnp.bfloat16)],
        compiler_params=_params(1),
        name="stickbreak",
    )(proj, proj, proj, proj)


def _outproj_kernel(x_ref, ym_ref, ya_ref, ys_ref, wm_ref, wa_ref, ws_ref, g_ref, gate_ref, o_ref, *, tm, sub):
    f32 = jnp.float32
    for i in range(tm // sub):
        r = slice(i * sub, (i + 1) * sub)
        y = (jnp.dot(ym_ref[0, r, :], wm_ref[...], preferred_element_type=f32)
             + jnp.dot(ya_ref[0, r, :], wa_ref[...], preferred_element_type=f32)
             + jnp.dot(ys_ref[0, r, :], ws_ref[...], preferred_element_type=f32))
        yn = y * lax.rsqrt(jnp.mean(y * y, axis=-1, keepdims=True) + EPS) * g_ref[...]
        o_ref[0, r, :] = x_ref[0, r, :] + gate_ref[...] * yn


def _outproj(x, ym, ya, ys, wm, wa, ws, g_post, mod4, layer):
    bsz, seq, d = x.shape
    tm = 1024
    kern = functools.partial(_outproj_kernel, tm=tm, sub=256)
    row = lambda w: pl.BlockSpec((1, tm, w), lambda b, s: (b, s, 0))
    const = lambda shape: pl.BlockSpec(shape, lambda b, s: (0,) * len(shape), pipeline_mode=pl.Buffered(1))
    return pl.pallas_call(
        kern,
        grid=(bsz, seq // tm),
        in_specs=[row(d), row(M_WP), row(A_W), row(S_W),
                  const((M_WP, d)), const((A_W, d)), const((S_W, d)), const((1, d)),
                  pl.BlockSpec((None, None, 1, d), lambda b, s: (layer, b, 0, 2))],
        out_specs=row(d),
        out_shape=jax.ShapeDtypeStruct((bsz, seq, d), jnp.float32),
        compiler_params=_params(2),
        name="outproj",
    )(x, ym, ya, ys, wm, wa, ws, g_post, mod4)


def _pad_heads(w, axis):
    shape = list(w.shape)
    shape[axis:axis + 1] = [M_HEADS, M_DH]
    w = w.reshape(shape)
    pad = [(0, 0)] * w.ndim
    pad[axis + 1] = (0, LANES - M_DH)
    w = jnp.pad(w, pad)
    shape[axis:axis + 2] = [M_WP]
    return w.reshape(shape)


def _pair_heads(w, axis):
    shape = list(w.shape)
    shape[axis:axis + 1] = [A_KV, A_G, A_DH]
    w = jnp.swapaxes(w.reshape(shape), axis, axis + 1)
    shape[axis:axis + 3] = [A_W]
    return w.reshape(shape)


def _pack_w_in(w):
    o = np.cumsum([0, M_W, M_W, M_HEADS, M_HEADS, M_W, A_W, A_KVW, A_KVW, A_W, S_W, S_W, S_W, S_W])
    seg = lambda i: w[:, int(o[i]):int(o[i + 1])]
    packed = jnp.concatenate(
        [_pad_heads(seg(0), 1), _pad_heads(seg(1), 1), _pad_heads(seg(4), 1),
         _pair_heads(seg(5), 1), _pair_heads(seg(8), 1), seg(6), seg(7),
         seg(9), seg(10), seg(11), seg(12)], axis=1).astype(jnp.bfloat16)
    gates_t = jnp.concatenate([seg(2), seg(3)], axis=1).T.astype(jnp.bfloat16)
    return packed, gates_t


def _pack_w_out(w):
    wm = _pad_heads(w[:M_W], 0)
    wa = _pair_heads(w[M_W:M_W + A_W], 0)
    ws = w[M_W + A_W:]
    return wm.astype(jnp.bfloat16), wa.astype(jnp.bfloat16), ws.astype(jnp.bfloat16)


def _pad_qk(w):
    return jnp.pad(w, ((0, 0), (0, LANES - M_DH), (0, LANES - M_DH))).astype(jnp.bfloat16)


def kernel(x, c, w_mod, b_mod, g_pre, g_post, w_in, m_conv_w, m_conv_b, m_wq, m_wk, m_b_i, m_b_f,
           m_norm_g, m_skip, a_sinks, w_out):
    bsz = x.shape[0]
    mod4 = _modulation(c, w_mod, b_mod).reshape(DEPTH, bsz, 1, 3 * D_MODEL)
    slopes = jnp.asarray(2.0 ** (-8.0 * np.arange(1, A_HEADS + 1) / A_HEADS), dtype=jnp.float32)
    for l in range(DEPTH):
        w_pack, wg_t = _pack_w_in(w_in[l])
        wm, wa, ws = _pack_w_out(w_out[l])
        gbias = jnp.broadcast_to(jnp.concatenate([m_b_i[l], m_b_f[l]])[:, None], (2 * M_HEADS, M_CHUNK))
        proj, gates = _inproj(x, mod4, l, g_pre[l][None], w_pack, wg_t,
                              _pad_heads(m_conv_w[l], 1), _pad_heads(m_conv_b[l][None], 1))
        ym = _mlstm(proj, gates, gbias, _pad_qk(m_wq[l]), _pad_qk(jnp.swapaxes(m_wk[l], 1, 2)),
                    _pad_heads(m_norm_g[l][None], 1), _pad_heads(m_skip[l][None], 1))
        ya = _swa(proj, slopes, a_sinks[l])
        ys = _stickbreak(proj)
        x = _outproj(x, ym, ya, ys, wm, wa, ws, g_post[l][None], mod4, l)
    return x
```

```python
import functools

import jax
import jax.numpy as jnp
import numpy as np
from jax import lax
from jax.experimental import pallas as pl
from jax.experimental.pallas import tpu as pltpu

D_MODEL = 1024
DEPTH = 2
M_HEADS = 4
M_DH = 96
M_W = M_HEADS * M_DH
M_CONV = 4
A_HEADS = 6
A_KV = 2
A_G = A_HEADS // A_KV
A_DH = 64
A_W = A_HEADS * A_DH
A_KVW = A_KV * A_DH
WINDOW = 128
S_HEADS = 4
S_DH = 64
S_W = S_HEADS * S_DH
S_BLOCK = 128
EPS = 1e-6

LANES = 128
M_CHUNK = 128
M_WP = M_HEADS * LANES
SB_TILE = 256
S_PAIRS = S_W // LANES
SB_DEAD = -110.0

OFF_MU, OFF_MV, OFF_MZ = 0, M_WP, 2 * M_WP
OFF_AQ = 3 * M_WP
OFF_AZ = OFF_AQ + A_W
OFF_AK = OFF_AZ + A_W
OFF_AV = OFF_AK + A_KVW
OFF_SQ = OFF_AV + A_KVW
OFF_SK = OFF_SQ + S_W
OFF_SV = OFF_SK + S_W
OFF_SZ = OFF_SV + S_W
N_PACK = OFF_SZ + S_W

VMEM_LIMIT = 48 * 1024 * 1024

_NT = (((1,), (1,)), ((), ()))
LOG2E = 1.4426950408889634


def _log_sigmoid(x):
    return jnp.minimum(x, 0.0) - jnp.log(1.0 + jnp.exp(-jnp.abs(x)))


def _silu(x):
    return x / (1.0 + jnp.exp(-x))


def _params(n_axes):
    return pltpu.CompilerParams(dimension_semantics=("arbitrary",) * n_axes, vmem_limit_bytes=VMEM_LIMIT)


def _mod_kernel(c_ref, w_ref, b_ref, o_ref):
    c_act = _silu(c_ref[...]).astype(jnp.bfloat16)
    o_ref[0] = jnp.dot(c_act, w_ref[0].astype(jnp.bfloat16), preferred_element_type=jnp.float32) + b_ref[0]


def _modulation(c, w_mod, b_mod):
    depth, d, n = w_mod.shape
    bsz = c.shape[0]
    tn = 1024
    return pl.pallas_call(
        _mod_kernel,
        grid=(depth, n // tn),
        in_specs=[pl.BlockSpec((bsz, d), lambda l, j: (0, 0)),
                  pl.BlockSpec((1, d, tn), lambda l, j: (l, 0, j)),
                  pl.BlockSpec((1, 1, tn), lambda l, j: (l, 0, j))],
        out_specs=pl.BlockSpec((1, bsz, tn), lambda l, j: (l, 0, j)),
        out_shape=jax.ShapeDtypeStruct((depth, bsz, n), jnp.float32),
        compiler_params=_params(2),
        name="modulation",
    )(c, w_mod, b_mod.reshape(depth, 1, n))


def _inproj_kernel(x_ref, shift_ref, scale_ref, g_ref, w_ref, wg_ref, cw_ref, cb_ref,
                   proj_ref, gates_ref, conv_ref, *, tm, sub, nchunk):
    s = pl.program_id(1)
    f32 = jnp.float32

    @pl.when(s == 0)
    def _():
        conv_ref[0:8, :] = jnp.zeros((8, M_WP), f32)

    subs = [slice(i * sub, (i + 1) * sub) for i in range(tm // sub)]
    hs = []
    for r in subs:
        x = x_ref[0, r, :]
        ms = jnp.mean(x * x, axis=-1, keepdims=True)
        h = x * lax.rsqrt(ms + EPS) * g_ref[...]
        hs.append((h * (1.0 + scale_ref[...]) + shift_ref[...]).astype(jnp.bfloat16))

    for i, (r, h) in enumerate(zip(subs, hs)):
        gates_ref[0, :, r] = lax.dot_general(wg_ref[...], h, _NT, preferred_element_type=f32)

        base = 8 + i * sub
        conv_ref[base:base + sub, :] = jnp.dot(h, w_ref[:, OFF_MU:OFF_MU + M_WP], preferred_element_type=f32)
        acc = cb_ref[...] + cw_ref[M_CONV - 1:M_CONV, :] * conv_ref[base:base + sub, :]
        for j in range(M_CONV - 1):
            lag = M_CONV - 1 - j
            acc = acc + cw_ref[j:j + 1, :] * conv_ref[base - lag:base - lag + sub, :]
        proj_ref[0, r, OFF_MU:OFF_MU + M_WP] = _silu(acc).astype(jnp.bfloat16)

        for c0 in range(OFF_MV, N_PACK, nchunk):
            proj_ref[0, r, c0:c0 + nchunk] = jnp.dot(
                h, w_ref[:, c0:c0 + nchunk], preferred_element_type=f32).astype(jnp.bfloat16)

    conv_ref[0:8, :] = conv_ref[tm:tm + 8, :]


def _inproj(x, mod4, layer, g_pre, w_pack, wg_t, conv_w, conv_b):
    bsz, seq, d = x.shape
    tm = 1024
    kern = functools.partial(_inproj_kernel, tm=tm, sub=512, nchunk=512)
    const = lambda shape: pl.BlockSpec(shape, lambda b, s: (0,) * len(shape), pipeline_mode=pl.Buffered(1))
    return pl.pallas_call(
        kern,
        grid=(bsz, seq // tm),
        in_specs=[pl.BlockSpec((1, tm, d), lambda b, s: (b, s, 0)),
                  pl.BlockSpec((None, None, 1, d), lambda b, s: (layer, b, 0, 0)),
                  pl.BlockSpec((None, None, 1, d), lambda b, s: (layer, b, 0, 1)),
                  const((1, d)), const((d, N_PACK)), const((8, d)), const((M_CONV, M_WP)), const((1, M_WP))],
        out_specs=[pl.BlockSpec((1, tm, N_PACK), lambda b, s: (b, s, 0)),
                   pl.BlockSpec((1, 8, tm), lambda b, s: (b, 0, s))],
        out_shape=[jax.ShapeDtypeStruct((bsz, seq, N_PACK), jnp.bfloat16),
                   jax.ShapeDtypeStruct((bsz, 8, seq), jnp.float32)],
        scratch_shapes=[pltpu.VMEM((tm + 8, M_WP), jnp.float32)],
        compiler_params=_params(2),
        name="inproj",
    )(x, mod4, mod4, g_pre, w_pack, wg_t, conv_w, conv_b)


def _mlstm_kernel(cu_ref, v_ref, z_ref, gates_ref, gbias_ref, wq_ref, wkt_ref, hg_ref, skip_ref,
                  o_ref, st_ref, a_ref, b_ref, mp_ref, ml_ref, *, seq):
    L = M_CHUNK
    nchunks = seq // L
    f32 = jnp.float32
    bf16 = jnp.bfloat16
    lane8 = lax.broadcasted_iota(jnp.int32, (8, L), 1)
    row8 = lax.broadcasted_iota(jnp.int32, (8, L), 0)

    amax, gsum = [], []
    for c in range(nchunks):
        g = gates_ref[0, :, c * L:(c + 1) * L] + gbias_ref[...]
        cum = jnp.where(row8 >= M_HEADS, _log_sigmoid(g), 0.0)
        k = 1
        while k < L:
            cum = cum + jnp.where(lane8 >= k, pltpu.roll(cum, k, axis=1), 0.0)
            k *= 2
        b8 = pltpu.roll(cum, M_HEADS, axis=0)
        a8 = g - b8
        a_ref[:, c * L:(c + 1) * L] = a8
        b_ref[:, c * L:(c + 1) * L] = b8
        amax.append(jnp.broadcast_to(jnp.max(a8, axis=1, keepdims=True), (8, L)))
        gsum.append(jnp.broadcast_to(jnp.max(jnp.where(lane8 == L - 1, b8, -jnp.inf), axis=1, keepdims=True), (8, L)))
    m_prev = jnp.zeros((8, L), f32)
    for c in range(nchunks):
        m_last = jnp.maximum(m_prev, amax[c])
        mp_ref[:, c * L:(c + 1) * L] = m_prev
        ml_ref[:, c * L:(c + 1) * L] = m_last
        m_prev = gsum[c] + m_last

    st_ref[...] = jnp.zeros(st_ref.shape, f32)

    t_idx = lax.broadcasted_iota(jnp.int32, (L, L), 0)
    s_idx = lax.broadcasted_iota(jnp.int32, (L, L), 1)
    tri = s_idx <= t_idx
    eye = s_idx == t_idx
    ones = jnp.ones((L, LANES), bf16)
    scale = M_DH ** -0.5
    heads = range(M_HEADS)
    cols = [slice(h * LANES, (h + 1) * LANES) for h in heads]

    def chunk(c, carry):
        r0 = pl.multiple_of(c * L, L)
        rows = pl.ds(r0, L)
        cu = [cu_ref[0, rows, cols[h]] for h in heads]
        q = [(jnp.dot(cu[h], wq_ref[h], preferred_element_type=f32) * scale).astype(bf16) for h in heads]
        kt = [lax.dot_general(wkt_ref[h], cu[h], _NT, preferred_element_type=f32) for h in heads]
        st = [st_ref[h] for h in heads]
        qk = [jnp.dot(q[h], kt[h].astype(bf16), preferred_element_type=f32) for h in heads]
        qs = [jnp.dot(q[h], st[h].astype(bf16), preferred_element_type=f32) for h in heads]
        smat, mmb, bb = [], [], []
        for h in heads:
            a_mat = jnp.broadcast_to(a_ref[h:h + 1, rows], (L, L))
            m1 = jnp.max(jnp.where(tri, a_mat, -jnp.inf), axis=1, keepdims=True)
            mm = jnp.maximum(mp_ref[h:h + 1, rows], m1)
            p = jnp.where(tri, jnp.exp(a_mat - mm), 0.0)
            smat.append((qk[h] * p).astype(bf16))
            mmb.append(mm)
            b_mat = jnp.broadcast_to(b_ref[h:h + 1, rows], (L, L))
            bb.append(jnp.broadcast_to(jnp.sum(jnp.where(eye, b_mat, 0.0), axis=1, keepdims=True), (L, LANES)))
        v2 = [jnp.concatenate([v_ref[0, rows, cols[h]], ones], axis=1) for h in heads]
        sv = [jnp.dot(smat[h], v2[h], preferred_element_type=f32) for h in heads]
        upd = []
        for h in heads:
            ws = jnp.exp(a_ref[h:h + 1, rows] - ml_ref[h:h + 1, rows])
            upd.append(jnp.dot((kt[h] * ws).astype(bf16), v2[h], preferred_element_type=f32))
        for h in heads:
            mp_row = mp_ref[h:h + 1, rows]
            w_inter = jnp.exp(mp_row - mmb[h])
            num = w_inter * qs[h][:, :LANES] + sv[h][:, :LANES]
            den = w_inter * qs[h][:, LANES:] + sv[h][:, LANES:]
            hh = num / jnp.maximum(jnp.abs(den), jnp.exp(-(bb[h] + mmb[h])))
            hh = hh * lax.rsqrt(jnp.sum(hh * hh, axis=1, keepdims=True) * (1.0 / M_DH) + EPS)
            y = hh * hg_ref[:, cols[h]] + skip_ref[:, cols[h]] * cu[h].astype(f32)
            y = y * _silu(z_ref[0, rows, cols[h]].astype(f32))
            o_ref[0, rows, cols[h]] = y.astype(o_ref.dtype)
            cs = jnp.exp(mp_row - ml_ref[h:h + 1, rows])
            st_ref[h] = jnp.concatenate([cs, cs], axis=1) * st[h] + upd[h]
        return carry

    lax.fori_loop(0, nchunks, chunk, 0, unroll=4)


def _mlstm(proj, gates, gbias, wq, wkt, hn_g, skip):
    bsz, seq, _ = proj.shape
    kern = functools.partial(_mlstm_kernel, seq=seq)
    blk = lambda j: pl.BlockSpec((1, seq, M_WP), lambda b: (b, 0, j))
    full = lambda shape: pl.BlockSpec(shape, lambda b: (0,) * len(shape))
    rows = pltpu.VMEM((8, seq), jnp.float32)
    return pl.pallas_call(
        kern,
        grid=(bsz,),
        in_specs=[blk(OFF_MU // M_WP), blk(OFF_MV // M_WP), blk(OFF_MZ // M_WP),
                  pl.BlockSpec((1, 8, seq), lambda b: (b, 0, 0)),
                  full((8, M_CHUNK)), full((M_HEADS, LANES, LANES)), full((M_HEADS, LANES, LANES)),
                  full((1, M_WP)), full((1, M_WP))],
        out_specs=pl.BlockSpec((1, seq, M_WP), lambda b: (b, 0, 0)),
        out_shape=jax.ShapeDtypeStruct((bsz, seq, M_WP), jnp.bfloat16),
        scratch_shapes=[pltpu.VMEM((M_HEADS, LANES, 2 * LANES), jnp.float32), rows, rows, rows, rows],
        compiler_params=_params(1),
        name="mlstm",
    )(proj, proj, proj, gates, gbias, wq, wkt, hn_g, skip)


def _swa_kernel(q_ref, z_ref, k_ref, v_ref, slope_ref, sink_ref, o_ref, *, seq):
    f32 = jnp.float32
    bf16 = jnp.bfloat16
    W = WINDOW
    ti = lax.broadcasted_iota(jnp.int32, (W, W), 0)
    si = lax.broadcasted_iota(jnp.int32, (W, W), 1)
    cur = si <= ti
    neg_rel = -jnp.where(cur, ti - si, W + ti - si).astype(f32)
    lane = lax.broadcasted_iota(jnp.int32, (W, LANES), 1)
    lane2 = lax.broadcasted_iota(jnp.int32, (2 * W, LANES), 1)
    halves = (lane < A_DH, lane >= A_DH)
    pairs = [(g, kv) for g in range(A_G) for kv in range(A_KV)]

    def block(n, carry):
        rows = pl.ds(pl.multiple_of(n * W, W), W)
        prev = pl.ds(pl.multiple_of(jnp.maximum(n - 1, 0) * W, W), W)
        bias = jnp.where((cur.astype(jnp.int32) + n) > 0, neg_rel, -jnp.inf)
        kk = jnp.concatenate([k_ref[0, rows, :], k_ref[0, prev, :]], axis=0)
        vv = jnp.concatenate([v_ref[0, rows, :], v_ref[0, prev, :]], axis=0)
        vms = [jnp.where(lane2 < A_DH, vv, jnp.zeros_like(vv)), jnp.where(lane2 >= A_DH, vv, jnp.zeros_like(vv))]
        q_all = q_ref[0, rows, :]
        q_all = q_all * jnp.asarray(A_DH ** -0.5, q_all.dtype)
        sc = []
        for g, kv in pairs:
            qg = q_all[:, g * LANES:(g + 1) * LANES]
            qm = jnp.where(halves[kv], qg, jnp.zeros_like(qg))
            sc.append(lax.dot_general(qm, kk, _NT, preferred_element_type=f32))
        pc = []
        for i, (g, kv) in enumerate(pairs):
            head = kv * A_G + g
            s2 = jnp.where(cur, sc[i][:, :W], sc[i][:, W:]) + slope_ref[head] * bias
            sink = sink_ref[head]
            mx = jnp.maximum(jnp.max(s2, axis=1, keepdims=True), sink)
            e = jnp.exp(s2 - mx)
            den = jnp.sum(e, axis=1, keepdims=True) + jnp.exp(sink - mx)
            probs = (e * (1.0 / den)).astype(bf16)
            zero = jnp.zeros_like(probs)
            pc.append(jnp.concatenate([jnp.where(cur, probs, zero), jnp.where(cur, zero, probs)], axis=1))
        for g in range(A_G):
            cols = slice(g * LANES, (g + 1) * LANES)
            o = (jnp.dot(pc[2 * g], vms[0], preferred_element_type=f32)
                 + jnp.dot(pc[2 * g + 1], vms[1], preferred_element_type=f32))
            o_ref[0, rows, cols] = (o * _silu(z_ref[0, rows, cols].astype(f32))).astype(o_ref.dtype)
        return carry

    lax.fori_loop(0, seq // W, block, 0, unroll=4)


def _swa(proj, slopes, sinks):
    bsz, seq, _ = proj.shape
    kern = functools.partial(_swa_kernel, seq=seq)
    smem = pl.BlockSpec(memory_space=pltpu.SMEM)
    full = lambda off, w: pl.BlockSpec((1, seq, w), lambda b: (b, 0, off // w))
    return pl.pallas_call(
        kern,
        grid=(bsz,),
        in_specs=[full(OFF_AQ, A_W), full(OFF_AZ, A_W), full(OFF_AK, A_KVW), full(OFF_AV, A_KVW), smem, smem],
        out_specs=pl.BlockSpec((1, seq, A_W), lambda b: (b, 0, 0)),
        out_shape=jax.ShapeDtypeStruct((bsz, seq, A_W), jnp.bfloat16),
        compiler_params=_params(1),
        name="swa",
    )(proj, proj, proj, proj, slopes, sinks)


def _sb_kernel(q_ref, z_ref, k_ref, v_ref, o_ref, acc_ref, run_ref, *, seq):
    f32 = jnp.float32
    bf16 = jnp.bfloat16
    T = SB_TILE
    lane = lax.broadcasted_iota(jnp.int32, (T, LANES), 1)
    halves = (lane < S_DH, lane >= S_DH)
    ti = lax.broadcasted_iota(jnp.int32, (T, T), 0)
    si = lax.broadcasted_iota(jnp.int32, (T, T), 1)
    strict = si < ti
    usum = jnp.where(ti > si, 1.0, 0.0).astype(bf16)
    heads = range(S_HEADS)
    cols = [slice((h // 2) * LANES, (h // 2 + 1) * LANES) for h in heads]

    def log_beta(zz):
        soft = jnp.log(1.0 + jnp.exp2(jnp.abs(zz) * (-LOG2E)))
        ls = jnp.minimum(zz, 0.0) - soft
        return ls, ls - zz

    def masked(x, h):
        return jnp.where(halves[h % 2], x[:, cols[h]], jnp.zeros((x.shape[0], LANES), x.dtype))

    def head_dots(a, vb):
        outs = []
        for p in range(S_PAIRS):
            pv = None
            for h in (2 * p, 2 * p + 1):
                d = None
                for ah, vh in zip(a[h], vb):
                    dd = jnp.dot(ah, masked(vh, h), preferred_element_type=f32)
                    d = dd if d is None else d + dd
                pv = d if pv is None else pv + d
            outs.append(pv)
        return jnp.concatenate(outs, axis=1)

    def qblock(n, first):
        r0 = 0 if first else pl.multiple_of(n * T, T)
        q = q_ref[0, pl.ds(r0, T), :]
        q = q * jnp.asarray(S_DH ** -0.5, q.dtype)
        qm = [masked(q, h) for h in heads]
        kc = k_ref[0, pl.ds(r0, T), :]
        vc = v_ref[0, pl.ds(r0, T), :]
        zero = jnp.zeros((T, T), bf16)

        if first:
            a = []
            for h in heads:
                ls, lk = log_beta(lax.dot_general(qm[h], kc[:, cols[h]], _NT, preferred_element_type=f32))
                lk = jnp.where(strict, lk, 0.0)
                suf = jnp.dot(lk.astype(bf16), usum, preferred_element_type=f32)
                a.append([jnp.where(strict, jnp.exp2((ls + suf) * LOG2E), 0.0).astype(bf16)])
            acc_ref[...] = head_dots(a, [vc])
        else:
            p0 = pl.multiple_of((n - 1) * T, T)
            kp = k_ref[0, pl.ds(p0, T), :]
            vp = v_ref[0, pl.ds(p0, T), :]
            zz = [jnp.where(strict,
                            lax.dot_general(qm[h], kc[:, cols[h]], _NT, preferred_element_type=f32),
                            lax.dot_general(qm[h], kp[:, cols[h]], _NT, preferred_element_type=f32)) for h in heads]
            ls, lkb = [], []
            for h in heads:
                lsh, lkh = log_beta(zz[h])
                ls.append(lsh)
                lkb.append(lkh.astype(bf16))
            sufm = [jnp.dot(lkb[h], usum, preferred_element_type=f32) for h in heads]
            a, tots = [], []
            for h in heads:
                lkf = lkb[h].astype(f32)
                rc = jnp.sum(jnp.where(strict, lkf, 0.0), axis=1, keepdims=True)
                tot = jnp.sum(lkf, axis=1, keepdims=True)
                ah = jnp.exp2((ls[h] + sufm[h] + jnp.where(strict, rc - tot, rc)) * LOG2E).astype(bf16)
                a.append([jnp.where(strict, ah, zero), jnp.where(strict, zero, ah)])
                tots.append(jnp.broadcast_to(tot, (T, LANES)))
                run_ref[h] = tots[h]
            acc_ref[...] = head_dots(a, [vc, vp])
            live0 = (jnp.max(functools.reduce(jnp.maximum, tots)) > SB_DEAD).astype(jnp.int32)

            def cond(c):
                i, live = c
                return jnp.logical_and(i <= n, live > 0)

            def body(c):
                i, _ = c
                j0 = pl.multiple_of((n - i) * T, T)
                kb = k_ref[0, pl.ds(j0, T), :]
                vb = v_ref[0, pl.ds(j0, T), :]
                keep = si < ti + jnp.where(i > 1, T, 0)
                a, runs = [], []
                for h in heads:
                    lsh, lkh = log_beta(lax.dot_general(qm[h], kb[:, cols[h]], _NT, preferred_element_type=f32))
                    lkh = jnp.where(keep, lkh, 0.0)
                    suf = jnp.dot(lkh.astype(bf16), usum, preferred_element_type=f32)
                    run = run_ref[h]
                    ah = jnp.exp2((lsh + suf + jnp.concatenate([run] * (T // LANES), axis=1)) * LOG2E)
                    a.append([jnp.where(keep, ah, 0.0).astype(bf16)])
                    runs.append(run + jnp.sum(lkh, axis=1, keepdims=True))
                    run_ref[h] = runs[h]
                acc_ref[...] += head_dots(a, [vb])
                live = jnp.max(functools.reduce(jnp.maximum, runs)) > SB_DEAD
                return i + 1, live.astype(jnp.int32)

            lax.while_loop(cond, body, (jnp.int32(1), live0))

        zg = z_ref[0, pl.ds(r0, T), :].astype(f32)
        o_ref[0, pl.ds(r0, T), :] = (acc_ref[...] * _silu(zg)).astype(o_ref.dtype)

    qblock(0, True)

    def blocks(n, carry):
        qblock(n, False)
        return carry

    lax.fori_loop(1, seq // T, blocks, 0)


def _stickbreak(proj):
    bsz, seq, _ = proj.shape
    kern = functools.partial(_sb_kernel, seq=seq)
    full = lambda off: pl.BlockSpec((1, seq, S_W), lambda b: (b, 0, off // S_W))
    return pl.pallas_call(
        kern,
        grid=(bsz,),
        in_specs=[full(OFF_SQ), full(OFF_SZ), full(OFF_SK), full(OFF_SV)],
        out_specs=pl.BlockSpec((1, seq, S_W), lambda b: (b, 0, 0)),
        out_shape=jax.ShapeDtypeStruct((bsz, seq, S_W), jnp.bfloat16),
        scratch_shapes=[pltpu.VMEM((SB_TILE, S_W), jnp.float32),
                        pltpu.VMEM((S_HEADS, SB_TILE, LANES), jnp.float32)],
        compiler_params=_params(1),
        name="stickbreak",
    )(proj, proj, proj, proj)


def _outproj_kernel(x_ref, ym_ref, ya_ref, ys_ref, wm_ref, wa_ref, ws_ref, g_ref, gate_ref, o_ref, *, tm, sub):
    f32 = jnp.float32
    for i in range(tm // sub):
        r = slice(i * sub, (i + 1) * sub)
        y = (jnp.dot(ym_ref[0, r, :], wm_ref[...], preferred_element_type=f32)
             + jnp.dot(ya_ref[0, r, :], wa_ref[...], preferred_element_type=f32)
             + jnp.dot(ys_ref[0, r, :], ws_ref[...], preferred_element_type=f32))
        yn = y * lax.rsqrt(jnp.mean(y * y, axis=-1, keepdims=True) + EPS) * g_ref[...]
        o_ref[0, r, :] = x_ref[0, r, :] + gate_ref[...] * yn


def _outproj(x, ym, ya, ys, wm, wa, ws, g_post, mod4, layer):
    bsz, seq, d = x.shape
    tm = 1024
    kern = functools.partial(_outproj_kernel, tm=tm, sub=256)
    row = lambda w: pl.BlockSpec((1, tm, w), lambda b, s: (b, s, 0))
    const = lambda shape: pl.BlockSpec(shape, lambda b, s: (0,) * len(shape), pipeline_mode=pl.Buffered(1))
    return pl.pallas_call(
        kern,
        grid=(bsz, seq // tm),
        in_specs=[row(d), row(M_WP), row(A_W), row(S_W),
                  const((M_WP, d)), const((A_W, d)), const((S_W, d)), const((1, d)),
                  pl.BlockSpec((None, None, 1, d), lambda b, s: (layer, b, 0, 2))],
        out_specs=row(d),
        out_shape=jax.ShapeDtypeStruct((bsz, seq, d), jnp.float32),
        compiler_params=_params(2),
        name="outproj",
    )(x, ym, ya, ys, wm, wa, ws, g_post, mod4)


def _pad_heads(w, axis):
    shape = list(w.shape)
    shape[axis:axis + 1] = [M_HEADS, M_DH]
    w = w.reshape(shape)
    pad = [(0, 0)] * w.ndim
    pad[axis + 1] = (0, LANES - M_DH)
    w = jnp.pad(w, pad)
    shape[axis:axis + 2] = [M_WP]
    return w.reshape(shape)


def _pair_heads(w, axis):
    shape = list(w.shape)
    shape[axis:axis + 1] = [A_KV, A_G, A_DH]
    w = jnp.swapaxes(w.reshape(shape), axis, axis + 1)
    shape[axis:axis + 3] = [A_W]
    return w.reshape(shape)


def _pack_w_in(w):
    o = np.cumsum([0, M_W, M_W, M_HEADS, M_HEADS, M_W, A_W, A_KVW, A_KVW, A_W, S_W, S_W, S_W, S_W])
    seg = lambda i: w[:, int(o[i]):int(o[i + 1])]
    packed = jnp.concatenate(
        [_pad_heads(seg(0), 1), _pad_heads(seg(1), 1), _pad_heads(seg(4), 1),
         _pair_heads(seg(5), 1), _pair_heads(seg(8), 1), seg(6), seg(7),
         seg(9), seg(10), seg(11), seg(12)], axis=1).astype(jnp.bfloat16)
    gates_t = jnp.concatenate([seg(2), seg(3)], axis=1).T.astype(jnp.bfloat16)
    return packed, gates_t


def _pack_w_out(w):
    wm = _pad_heads(w[:M_W], 0)
    wa = _pair_heads(w[M_W:M_W + A_W], 0)
    ws = w[M_W + A_W:]
    return wm.astype(jnp.bfloat16), wa.astype(jnp.bfloat16), ws.astype(jnp.bfloat16)


def _pad_qk(w):
    return jnp.pad(w, ((0, 0), (0, LANES - M_DH), (0, LANES - M_DH))).astype(jnp.bfloat16)


def kernel(x, c, w_mod, b_mod, g_pre, g_post, w_in, m_conv_w, m_conv_b, m_wq, m_wk, m_b_i, m_b_f,
           m_norm_g, m_skip, a_sinks, w_out):
    bsz = x.shape[0]
    mod4 = _modulation(c, w_mod, b_mod).reshape(DEPTH, bsz, 1, 3 * D_MODEL)
    slopes = jnp.asarray(2.0 ** (-8.0 * np.arange(1, A_HEADS + 1) / A_HEADS), dtype=jnp.float32)
    for l in range(DEPTH):
        w_pack, wg_t = _pack_w_in(w_in[l])
        wm, wa, ws = _pack_w_out(w_out[l])
        gbias = jnp.broadcast_to(jnp.concatenate([m_b_i[l], m_b_f[l]])[:, None], (2 * M_HEADS, M_CHUNK))
        proj, gates = _inproj(x, mod4, l, g_pre[l][None], w_pack, wg_t,
                              _pad_heads(m_conv_w[l], 1), _pad_heads(m_conv_b[l][None], 1))
        ym = _mlstm(proj, gates, gbias, _pad_qk(m_wq[l]), _pad_qk(jnp.swapaxes(m_wk[l], 1, 2)),
                    _pad_heads(m_norm_g[l][None], 1), _pad_heads(m_skip[l][None], 1))
        ya = _swa(proj, slopes, a_sinks[l])
        ys = _stickbreak(proj)
        x = _outproj(x, ym, ya, ys, wm, wa, ws, g_post[l][None], mod4, l)
    return x
```

```python
import functools

import jax
import jax.numpy as jnp
import numpy as np
from jax import lax
from jax.experimental import pallas as pl
from jax.experimental.pallas import tpu as pltpu

D_MODEL = 1024
DEPTH = 2
M_HEADS = 4
M_DH = 96
M_W = M_HEADS * M_DH
M_CONV = 4
A_HEADS = 6
A_KV = 2
A_G = A_HEADS // A_KV
A_DH = 64
A_W = A_HEADS * A_DH
A_KVW = A_KV * A_DH
WINDOW = 128
S_HEADS = 4
S_DH = 64
S_W = S_HEADS * S_DH
S_BLOCK = 128
EPS = 1e-6

LANES = 128
M_CHUNK = 128
M_WP = M_HEADS * LANES
SB_TILE = 256
S_PAIRS = S_W // LANES
SB_DEAD = -93.0

OFF_MU, OFF_MV, OFF_MZ = 0, M_WP, 2 * M_WP
OFF_AQ = 3 * M_WP
OFF_AZ = OFF_AQ + A_W
OFF_AK = OFF_AZ + A_W
OFF_AV = OFF_AK + A_KVW
OFF_SQ = OFF_AV + A_KVW
OFF_SK = OFF_SQ + S_W
OFF_SV = OFF_SK + S_W
OFF_SZ = OFF_SV + S_W
N_PACK = OFF_SZ + S_W

VMEM_LIMIT = 48 * 1024 * 1024

_NT = (((1,), (1,)), ((), ()))
LOG2E = 1.4426950408889634


def _log_sigmoid(x):
    return jnp.minimum(x, 0.0) - jnp.log(1.0 + jnp.exp(-jnp.abs(x)))


def _silu(x):
    return x / (1.0 + jnp.exp(-x))


def _params(n_axes):
    return pltpu.CompilerParams(dimension_semantics=("arbitrary",) * n_axes, vmem_limit_bytes=VMEM_LIMIT)


def _mod_kernel(c_ref, w_ref, b_ref, o_ref):
    c_act = _silu(c_ref[...]).astype(jnp.bfloat16)
    o_ref[0] = jnp.dot(c_act, w_ref[0].astype(jnp.bfloat16), preferred_element_type=jnp.float32) + b_ref[0]


def _modulation(c, w_mod, b_mod):
    depth, d, n = w_mod.shape
    bsz = c.shape[0]
    tn = 1024
    return pl.pallas_call(
        _mod_kernel,
        grid=(depth, n // tn),
        in_specs=[pl.BlockSpec((bsz, d), lambda l, j: (0, 0)),
                  pl.BlockSpec((1, d, tn), lambda l, j: (l, 0, j)),
                  pl.BlockSpec((1, 1, tn), lambda l, j: (l, 0, j))],
        out_specs=pl.BlockSpec((1, bsz, tn), lambda l, j: (l, 0, j)),
        out_shape=jax.ShapeDtypeStruct((depth, bsz, n), jnp.float32),
        compiler_params=_params(2),
        name="modulation",
    )(c, w_mod, b_mod.reshape(depth, 1, n))


def _inproj_kernel(x_ref, shift_ref, scale_ref, g_ref, w_ref, wg_ref, cw_ref, cb_ref,
                   proj_ref, gates_ref, conv_ref, *, tm, sub, nchunk):
    s = pl.program_id(1)
    f32 = jnp.float32

    @pl.when(s == 0)
    def _():
        conv_ref[0:8, :] = jnp.zeros((8, M_WP), f32)

    subs = [slice(i * sub, (i + 1) * sub) for i in range(tm // sub)]
    hs = []
    for r in subs:
        x = x_ref[0, r, :]
        ms = jnp.mean(x * x, axis=-1, keepdims=True)
        h = x * lax.rsqrt(ms + EPS) * g_ref[...]
        hs.append((h * (1.0 + scale_ref[...]) + shift_ref[...]).astype(jnp.bfloat16))

    for i, (r, h) in enumerate(zip(subs, hs)):
        gates_ref[0, :, r] = lax.dot_general(wg_ref[...], h, _NT, preferred_element_type=f32)

        base = 8 + i * sub
        conv_ref[base:base + sub, :] = jnp.dot(h, w_ref[:, OFF_MU:OFF_MU + M_WP], preferred_element_type=f32)
        acc = cb_ref[...] + cw_ref[M_CONV - 1:M_CONV, :] * conv_ref[base:base + sub, :]
        for j in range(M_CONV - 1):
            lag = M_CONV - 1 - j
            acc = acc + cw_ref[j:j + 1, :] * conv_ref[base - lag:base - lag + sub, :]
        proj_ref[0, r, OFF_MU:OFF_MU + M_WP] = _silu(acc).astype(jnp.bfloat16)

        for c0 in range(OFF_MV, N_PACK, nchunk):
            proj_ref[0, r, c0:c0 + nchunk] = jnp.dot(
                h, w_ref[:, c0:c0 + nchunk], preferred_element_type=f32).astype(jnp.bfloat16)

    conv_ref[0:8, :] = conv_ref[tm:tm + 8, :]


def _inproj(x, mod4, layer, g_pre, w_pack, wg_t, conv_w, conv_b):
    bsz, seq, d = x.shape
    tm = 1024
    kern = functools.partial(_inproj_kernel, tm=tm, sub=512, nchunk=512)
    const = lambda shape: pl.BlockSpec(shape, lambda b, s: (0,) * len(shape), pipeline_mode=pl.Buffered(1))
    return pl.pallas_call(
        kern,
        grid=(bsz, seq // tm),
        in_specs=[pl.BlockSpec((1, tm, d), lambda b, s: (b, s, 0)),
                  pl.BlockSpec((None, None, 1, d), lambda b, s: (layer, b, 0, 0)),
                  pl.BlockSpec((None, None, 1, d), lambda b, s: (layer, b, 0, 1)),
                  const((1, d)), const((d, N_PACK)), const((8, d)), const((M_CONV, M_WP)), const((1, M_WP))],
        out_specs=[pl.BlockSpec((1, tm, N_PACK), lambda b, s: (b, s, 0)),
                   pl.BlockSpec((1, 8, tm), lambda b, s: (b, 0, s))],
        out_shape=[jax.ShapeDtypeStruct((bsz, seq, N_PACK), jnp.bfloat16),
                   jax.ShapeDtypeStruct((bsz, 8, seq), jnp.float32)],
        scratch_shapes=[pltpu.VMEM((tm + 8, M_WP), jnp.float32)],
        compiler_params=_params(2),
        name="inproj",
    )(x, mod4, mod4, g_pre, w_pack, wg_t, conv_w, conv_b)


def _mlstm_kernel(cu_ref, v_ref, z_ref, gates_ref, gbias_ref, wq_ref, wkt_ref, hg_ref, skip_ref,
                  o_ref, st_ref, a_ref, b_ref, mp_ref, ml_ref, *, seq):
    L = M_CHUNK
    nchunks = seq // L
    f32 = jnp.float32
    bf16 = jnp.bfloat16
    lane8 = lax.broadcasted_iota(jnp.int32, (8, L), 1)
    row8 = lax.broadcasted_iota(jnp.int32, (8, L), 0)

    amax, gsum = [], []
    for c in range(nchunks):
        g = gates_ref[0, :, c * L:(c + 1) * L] + gbias_ref[...]
        cum = jnp.where(row8 >= M_HEADS, _log_sigmoid(g), 0.0)
        k = 1
        while k < L:
            cum = cum + jnp.where(lane8 >= k, pltpu.roll(cum, k, axis=1), 0.0)
            k *= 2
        b8 = pltpu.roll(cum, M_HEADS, axis=0)
        a8 = g - b8
        a_ref[:, c * L:(c + 1) * L] = a8
        b_ref[:, c * L:(c + 1) * L] = b8
        amax.append(jnp.broadcast_to(jnp.max(a8, axis=1, keepdims=True), (8, L)))
        gsum.append(jnp.broadcast_to(jnp.max(jnp.where(lane8 == L - 1, b8, -jnp.inf), axis=1, keepdims=True), (8, L)))
    m_prev = jnp.zeros((8, L), f32)
    for c in range(nchunks):
        m_last = jnp.maximum(m_prev, amax[c])
        mp_ref[:, c * L:(c + 1) * L] = m_prev
        ml_ref[:, c * L:(c + 1) * L] = m_last
        m_prev = gsum[c] + m_last

    st_ref[...] = jnp.zeros(st_ref.shape, f32)

    t_idx = lax.broadcasted_iota(jnp.int32, (L, L), 0)
    s_idx = lax.broadcasted_iota(jnp.int32, (L, L), 1)
    tri = s_idx <= t_idx
    eye = s_idx == t_idx
    ones = jnp.ones((L, LANES), bf16)
    scale = M_DH ** -0.5
    heads = range(M_HEADS)
    cols = [slice(h * LANES, (h + 1) * LANES) for h in heads]

    def chunk(c, carry):
        r0 = pl.multiple_of(c * L, L)
        rows = pl.ds(r0, L)
        cu = [cu_ref[0, rows, cols[h]] for h in heads]
        q = [(jnp.dot(cu[h], wq_ref[h], preferred_element_type=f32) * scale).astype(bf16) for h in heads]
        kt = [lax.dot_general(wkt_ref[h], cu[h], _NT, preferred_element_type=f32) for h in heads]
        st = [st_ref[h] for h in heads]
        qk = [jnp.dot(q[h], kt[h].astype(bf16), preferred_element_type=f32) for h in heads]
        qs = [jnp.dot(q[h], st[h].astype(bf16), preferred_element_type=f32) for h in heads]
        smat, mmb, bb = [], [], []
        for h in heads:
            a_mat = jnp.broadcast_to(a_ref[h:h + 1, rows], (L, L))
            m1 = jnp.max(jnp.where(tri, a_mat, -jnp.inf), axis=1, keepdims=True)
            mm = jnp.maximum(mp_ref[h:h + 1, rows], m1)
            p = jnp.where(tri, jnp.exp(a_mat - mm), 0.0)
            smat.append((qk[h] * p).astype(bf16))
            mmb.append(mm)
            b_mat = jnp.broadcast_to(b_ref[h:h + 1, rows], (L, L))
            bb.append(jnp.broadcast_to(jnp.sum(jnp.where(eye, b_mat, 0.0), axis=1, keepdims=True), (L, LANES)))
        v2 = [jnp.concatenate([v_ref[0, rows, cols[h]], ones], axis=1) for h in heads]
        sv = [jnp.dot(smat[h], v2[h], preferred_element_type=f32) for h in heads]
        upd = []
        for h in heads:
            ws = jnp.exp(a_ref[h:h + 1, rows] - ml_ref[h:h + 1, rows])
            upd.append(jnp.dot((kt[h] * ws).astype(bf16), v2[h], preferred_element_type=f32))
        for h in heads:
            mp_row = mp_ref[h:h + 1, rows]
            w_inter = jnp.exp(mp_row - mmb[h])
            num = w_inter * qs[h][:, :LANES] + sv[h][:, :LANES]
            den = w_inter * qs[h][:, LANES:] + sv[h][:, LANES:]
            hh = num / jnp.maximum(jnp.abs(den), jnp.exp(-(bb[h] + mmb[h])))
            hh = hh * lax.rsqrt(jnp.sum(hh * hh, axis=1, keepdims=True) * (1.0 / M_DH) + EPS)
            y = hh * hg_ref[:, cols[h]] + skip_ref[:, cols[h]] * cu[h].astype(f32)
            y = y * _silu(z_ref[0, rows, cols[h]].astype(f32))
            o_ref[0, rows, cols[h]] = y.astype(o_ref.dtype)
            cs = jnp.exp(mp_row - ml_ref[h:h + 1, rows])
            st_ref[h] = jnp.concatenate([cs, cs], axis=1) * st[h] + upd[h]
        return carry

    lax.fori_loop(0, nchunks, chunk, 0, unroll=4)


def _mlstm(proj, gates, gbias, wq, wkt, hn_g, skip):
    bsz, seq, _ = proj.shape
    kern = functools.partial(_mlstm_kernel, seq=seq)
    blk = lambda j: pl.BlockSpec((1, seq, M_WP), lambda b: (b, 0, j))
    full = lambda shape: pl.BlockSpec(shape, lambda b: (0,) * len(shape))
    rows = pltpu.VMEM((8, seq), jnp.float32)
    return pl.pallas_call(
        kern,
        grid=(bsz,),
        in_specs=[blk(OFF_MU // M_WP), blk(OFF_MV // M_WP), blk(OFF_MZ // M_WP),
                  pl.BlockSpec((1, 8, seq), lambda b: (b, 0, 0)),
                  full((8, M_CHUNK)), full((M_HEADS, LANES, LANES)), full((M_HEADS, LANES, LANES)),
                  full((1, M_WP)), full((1, M_WP))],
        out_specs=pl.BlockSpec((1, seq, M_WP), lambda b: (b, 0, 0)),
        out_shape=jax.ShapeDtypeStruct((bsz, seq, M_WP), jnp.bfloat16),
        scratch_shapes=[pltpu.VMEM((M_HEADS, LANES, 2 * LANES), jnp.float32), rows, rows, rows, rows],
        compiler_params=_params(1),
        name="mlstm",
    )(proj, proj, proj, gates, gbias, wq, wkt, hn_g, skip)


def _swa_kernel(q_ref, z_ref, k_ref, v_ref, slope_ref, sink_ref, o_ref, *, seq):
    f32 = jnp.float32
    bf16 = jnp.bfloat16
    W = WINDOW
    ti = lax.broadcasted_iota(jnp.int32, (W, W), 0)
    si = lax.broadcasted_iota(jnp.int32, (W, W), 1)
    cur = si <= ti
    neg_rel = -jnp.where(cur, ti - si, W + ti - si).astype(f32)
    lane = lax.broadcasted_iota(jnp.int32, (W, LANES), 1)
    lane2 = lax.broadcasted_iota(jnp.int32, (2 * W, LANES), 1)
    halves = (lane < A_DH, lane >= A_DH)
    pairs = [(g, kv) for g in range(A_G) for kv in range(A_KV)]

    def block(n, carry):
        rows = pl.ds(pl.multiple_of(n * W, W), W)
        prev = pl.ds(pl.multiple_of(jnp.maximum(n - 1, 0) * W, W), W)
        bias = jnp.where((cur.astype(jnp.int32) + n) > 0, neg_rel, -jnp.inf)
        kk = jnp.concatenate([k_ref[0, rows, :], k_ref[0, prev, :]], axis=0)
        vv = jnp.concatenate([v_ref[0, rows, :], v_ref[0, prev, :]], axis=0)
        vms = [jnp.where(lane2 < A_DH, vv, jnp.zeros_like(vv)), jnp.where(lane2 >= A_DH, vv, jnp.zeros_like(vv))]
        q_all = q_ref[0, rows, :]
        q_all = q_all * jnp.asarray(A_DH ** -0.5, q_all.dtype)
        sc = []
        for g, kv in pairs:
            qg = q_all[:, g * LANES:(g + 1) * LANES]
            qm = jnp.where(halves[kv], qg, jnp.zeros_like(qg))
            sc.append(lax.dot_general(qm, kk, _NT, preferred_element_type=f32))
        pc = []
        for i, (g, kv) in enumerate(pairs):
            head = kv * A_G + g
            s2 = jnp.where(cur, sc[i][:, :W], sc[i][:, W:]) + slope_ref[head] * bias
            sink = sink_ref[head]
            mx = jnp.maximum(jnp.max(s2, axis=1, keepdims=True), sink)
            e = jnp.exp(s2 - mx)
            den = jnp.sum(e, axis=1, keepdims=True) + jnp.exp(sink - mx)
            probs = (e * (1.0 / den)).astype(bf16)
            zero = jnp.zeros_like(probs)
            pc.append(jnp.concatenate([jnp.where(cur, probs, zero), jnp.where(cur, zero, probs)], axis=1))
        for g in range(A_G):
            cols = slice(g * LANES, (g + 1) * LANES)
            o = (jnp.dot(pc[2 * g], vms[0], preferred_element_type=f32)
                 + jnp.dot(pc[2 * g + 1], vms[1], preferred_element_type=f32))
            o_ref[0, rows, cols] = (o * _silu(z_ref[0, rows, cols].astype(f32))).astype(o_ref.dtype)
        return carry

    lax.fori_loop(0, seq // W, block, 0, unroll=4)


def _swa(proj, slopes, sinks):
    bsz, seq, _ = proj.shape
    kern = functools.partial(_swa_kernel, seq=seq)
    smem = pl.BlockSpec(memory_space=pltpu.SMEM)
    full = lambda off, w: pl.BlockSpec((1, seq, w), lambda b: (b, 0, off // w))
    return pl.pallas_call(
        kern,
        grid=(bsz,),
        in_specs=[full(OFF_AQ, A_W), full(OFF_AZ, A_W), full(OFF_AK, A_KVW), full(OFF_AV, A_KVW), smem, smem],
        out_specs=pl.BlockSpec((1, seq, A_W), lambda b: (b, 0, 0)),
        out_shape=jax.ShapeDtypeStruct((bsz, seq, A_W), jnp.bfloat16),
        compiler_params=_params(1),
        name="swa",
    )(proj, proj, proj, proj, slopes, sinks)


def _sb_kernel(q_ref, z_ref, k_ref, v_ref, o_ref, acc_ref, run_ref, zz_ref, a_ref, *, seq):
    f32 = jnp.float32
    bf16 = jnp.bfloat16
    T = SB_TILE
    lane = lax.broadcasted_iota(jnp.int32, (T, LANES), 1)
    halves = (lane < S_DH, lane >= S_DH)
    ti = lax.broadcasted_iota(jnp.int32, (T, T), 0)
    si = lax.broadcasted_iota(jnp.int32, (T, T), 1)
    strict = si < ti
    usum = jnp.where(ti > si, 1.0, 0.0).astype(bf16)
    heads = range(S_HEADS)
    cols = [slice((h // 2) * LANES, (h // 2 + 1) * LANES) for h in heads]

    def qblock(n, carry):
        r0 = pl.multiple_of(n * T, T)
        q = q_ref[0, pl.ds(r0, T), :]
        q = q * jnp.asarray(S_DH ** -0.5, q.dtype)
        qm = [jnp.where(halves[h % 2], q[:, cols[h]], jnp.zeros((T, LANES), q.dtype)) for h in heads]
        acc_ref[...] = jnp.zeros(acc_ref.shape, f32)
        run_ref[...] = jnp.zeros(run_ref.shape, f32)

        def score(j):
            k0 = pl.multiple_of(j * T, T)
            kb = k_ref[0, pl.ds(k0, T), :]
            for h in heads:
                zz_ref[h] = lax.dot_general(qm[h], kb[:, cols[h]], _NT, preferred_element_type=f32)

        def apply(j):
            k0 = pl.multiple_of(j * T, T)
            vb = v_ref[0, pl.ds(k0, T), :]
            for p in range(S_PAIRS):
                pv = None
                for h in (2 * p, 2 * p + 1):
                    vm = jnp.where(halves[h % 2], vb[:, cols[h]], jnp.zeros((T, LANES), vb.dtype))
                    d = jnp.dot(a_ref[h], vm, preferred_element_type=f32)
                    pv = d if pv is None else pv + d
                acc_ref[:, cols[2 * p]] += pv

        def weights(j_next, diag):
            ls, lk = [], []
            for h in heads:
                zz = zz_ref[h]
                soft = jnp.log(1.0 + jnp.exp2(jnp.abs(zz) * (-LOG2E)))
                ls.append(jnp.minimum(zz, 0.0) - soft)
                lkh = ls[h] - zz
                lk.append(jnp.where(strict, lkh, 0.0) if diag else lkh)
            suf = [jnp.dot(lk[h].astype(bf16), usum, preferred_element_type=f32) for h in heads]
            runs = [run_ref[h] for h in heads]
            new_runs = [runs[h] + jnp.sum(lk[h], axis=1, keepdims=True) for h in heads]
            live = jnp.max(functools.reduce(jnp.maximum, new_runs)) > SB_DEAD
            score(j_next)
            for h in heads:
                ah = jnp.exp2((ls[h] + suf[h] + jnp.concatenate([runs[h]] * (T // LANES), axis=1)) * LOG2E)
                a_ref[h] = (jnp.where(strict, ah, 0.0) if diag else ah).astype(bf16)
                run_ref[h] = new_runs[h]
            return live.astype(jnp.int32)

        score(n)
        live0 = weights(max(n - 1, 0), True)

        def cond(c):
            i, live = c
            return jnp.logical_and(i <= n, live > 0)

        def body(c):
            i, _ = c
            apply(n - i + 1)
            return i + 1, weights(jnp.maximum(n - i - 1, 0), False)

        i_end, _ = lax.while_loop(cond, body, (jnp.int32(1), live0))
        apply(n - i_end + 1)
        zg = z_ref[0, pl.ds(r0, T), :].astype(f32)
        o_ref[0, pl.ds(r0, T), :] = (acc_ref[...] * _silu(zg)).astype(o_ref.dtype)
        return carry

    for n in range(seq // T):
        qblock(n, 0)


def _stickbreak(proj):
    bsz, seq, _ = proj.shape
    kern = functools.partial(_sb_kernel, seq=seq)
    full = lambda off: pl.BlockSpec((1, seq, S_W), lambda b: (b, 0, off // S_W))
    return pl.pallas_call(
        kern,
        grid=(bsz,),
        in_specs=[full(OFF_SQ), full(OFF_SZ), full(OFF_SK), full(OFF_SV)],
        out_specs=pl.BlockSpec((1, seq, S_W), lambda b: (b, 0, 0)),
        out_shape=jax.ShapeDtypeStruct((bsz, seq, S_W), jnp.bfloat16),
        scratch_shapes=[pltpu.VMEM((SB_TILE, S_W), jnp.float32),
                        pltpu.VMEM((S_HEADS, SB_TILE, LANES), jnp.float32),
                        pltpu.VMEM((S_HEADS, SB_TILE, SB_TILE), jnp.float32),
                        pltpu.VMEM((S_HEADS, SB_TILE, SB_TILE), jnp.bfloat16)],
        compiler_params=_params(1),
        name="stickbreak",
    )(proj, proj, proj, proj)


def _outproj_kernel(x_ref, ym_ref, ya_ref, ys_ref, wm_ref, wa_ref, ws_ref, g_ref, gate_ref, o_ref, *, tm, sub):
    f32 = jnp.float32
    for i in range(tm // sub):
        r = slice(i * sub, (i + 1) * sub)
        y = (jnp.dot(ym_ref[0, r, :], wm_ref[...], preferred_element_type=f32)
             + jnp.dot(ya_ref[0, r, :], wa_ref[...], preferred_element_type=f32)
             + jnp.dot(ys_ref[0, r, :], ws_ref[...], preferred_element_type=f32))
        yn = y * lax.rsqrt(jnp.mean(y * y, axis=-1, keepdims=True) + EPS) * g_ref[...]
        o_ref[0, r, :] = x_ref[0, r, :] + gate_ref[...] * yn


def _outproj(x, ym, ya, ys, wm, wa, ws, g_post, mod4, layer):
    bsz, seq, d = x.shape
    tm = 1024
    kern = functools.partial(_outproj_kernel, tm=tm, sub=256)
    row = lambda w: pl.BlockSpec((1, tm, w), lambda b, s: (b, s, 0))
    const = lambda shape: pl.BlockSpec(shape, lambda b, s: (0,) * len(shape), pipeline_mode=pl.Buffered(1))
    return pl.pallas_call(
        kern,
        grid=(bsz, seq // tm),
        in_specs=[row(d), row(M_WP), row(A_W), row(S_W),
                  const((M_WP, d)), const((A_W, d)), const((S_W, d)), const((1, d)),
                  pl.BlockSpec((None, None, 1, d), lambda b, s: (layer, b, 0, 2))],
        out_specs=row(d),
        out_shape=jax.ShapeDtypeStruct((bsz, seq, d), jnp.float32),
        compiler_params=_params(2),
        name="outproj",
    )(x, ym, ya, ys, wm, wa, ws, g_post, mod4)


def _pad_heads(w, axis):
    shape = list(w.shape)
    shape[axis:axis + 1] = [M_HEADS, M_DH]
    w = w.reshape(shape)
    pad = [(0, 0)] * w.ndim
    pad[axis + 1] = (0, LANES - M_DH)
    w = jnp.pad(w, pad)
    shape[axis:axis + 2] = [M_WP]
    return w.reshape(shape)


def _pair_heads(w, axis):
    shape = list(w.shape)
    shape[axis:axis + 1] = [A_KV, A_G, A_DH]
    w = jnp.swapaxes(w.reshape(shape), axis, axis + 1)
    shape[axis:axis + 3] = [A_W]
    return w.reshape(shape)


def _pack_w_in(w):
    o = np.cumsum([0, M_W, M_W, M_HEADS, M_HEADS, M_W, A_W, A_KVW, A_KVW, A_W, S_W, S_W, S_W, S_W])
    w = w.astype(jnp.bfloat16)
    seg = lambda i: w[:, int(o[i]):int(o[i + 1])]
    packed = jnp.concatenate(
        [_pad_heads(seg(0), 1), _pad_heads(seg(1), 1), _pad_heads(seg(4), 1),
         _pair_heads(seg(5), 1), _pair_heads(seg(8), 1), seg(6), seg(7),
         seg(9), seg(10), seg(11), seg(12)], axis=1)
    gates_t = jnp.concatenate([seg(2), seg(3)], axis=1).T
    return packed, gates_t


def _pack_w_out(w):
    w = w.astype(jnp.bfloat16)
    return _pad_heads(w[:M_W], 0), _pair_heads(w[M_W:M_W + A_W], 0), w[M_W + A_W:]


def _pad_qk(w):
    return jnp.pad(w, ((0, 0), (0, LANES - M_DH), (0, LANES - M_DH))).astype(jnp.bfloat16)


def kernel(x, c, w_mod, b_mod, g_pre, g_post, w_in, m_conv_w, m_conv_b, m_wq, m_wk, m_b_i, m_b_f,
           m_norm_g, m_skip, a_sinks, w_out):
    bsz = x.shape[0]
    mod4 = _modulation(c, w_mod, b_mod).reshape(DEPTH, bsz, 1, 3 * D_MODEL)
    slopes = jnp.asarray(2.0 ** (-8.0 * np.arange(1, A_HEADS + 1) / A_HEADS), dtype=jnp.float32)
    for l in range(DEPTH):
        w_pack, wg_t = _pack_w_in(w_in[l])
        wm, wa, ws = _pack_w_out(w_out[l])
        gbias = jnp.broadcast_to(jnp.concatenate([m_b_i[l], m_b_f[l]])[:, None], (2 * M_HEADS, M_CHUNK))
        proj, gates = _inproj(x, mod4, l, g_pre[l][None], w_pack, wg_t,
                              _pad_heads(m_conv_w[l], 1), _pad_heads(m_conv_b[l][None], 1))
        ym = _mlstm(proj, gates, gbias, _pad_qk(m_wq[l]), _pad_qk(jnp.swapaxes(m_wk[l], 1, 2)),
                    _pad_heads(m_norm_g[l][None], 1), _pad_heads(m_skip[l][None], 1))
        ya = _swa(proj, slopes, a_sinks[l])
        ys = _stickbreak(proj)
        x = _outproj(x, ym, ya, ys, wm, wa, ws, g_post[l][None], mod4, l)
    return x
```

```python
import functools

import jax
import jax.numpy as jnp
import numpy as np
from jax import lax
from jax.experimental import pallas as pl
from jax.experimental.pallas import tpu as pltpu

D_MODEL = 1024
DEPTH = 2
M_HEADS = 4
M_DH = 96
M_W = M_HEADS * M_DH
M_CONV = 4
A_HEADS = 6
A_KV = 2
A_G = A_HEADS // A_KV
A_DH = 64
A_W = A_HEADS * A_DH
A_KVW = A_KV * A_DH
WINDOW = 128
S_HEADS = 4
S_DH = 64
S_W = S_HEADS * S_DH
S_BLOCK = 128
EPS = 1e-6

LANES = 128
M_CHUNK = 128
M_WP = M_HEADS * LANES
SB_TILE = 256
S_PAIRS = S_W // LANES
SB_DEAD = -93.0

OFF_MU, OFF_MV, OFF_MZ = 0, M_WP, 2 * M_WP
OFF_AQ = 3 * M_WP
OFF_AZ = OFF_AQ + A_W
OFF_AK = OFF_AZ + A_W
OFF_AV = OFF_AK + A_KVW
OFF_SQ = OFF_AV + A_KVW
OFF_SK = OFF_SQ + S_W
OFF_SV = OFF_SK + S_W
OFF_SZ = OFF_SV + S_W
N_PACK = OFF_SZ + S_W

VMEM_LIMIT = 48 * 1024 * 1024

_NT = (((1,), (1,)), ((), ()))
LOG2E = 1.4426950408889634


def _log_sigmoid(x):
    return jnp.minimum(x, 0.0) - jnp.log(1.0 + jnp.exp(-jnp.abs(x)))


def _silu(x):
    return x / (1.0 + jnp.exp(-x))


def _params(n_axes):
    return pltpu.CompilerParams(dimension_semantics=("arbitrary",) * n_axes, vmem_limit_bytes=VMEM_LIMIT)


def _mod_kernel(c_ref, w_ref, b_ref, o_ref):
    c_act = _silu(c_ref[...]).astype(jnp.bfloat16)
    o_ref[0] = jnp.dot(c_act, w_ref[0].astype(jnp.bfloat16), preferred_element_type=jnp.float32) + b_ref[0]


def _modulation(c, w_mod, b_mod):
    depth, d, n = w_mod.shape
    bsz = c.shape[0]
    tn = 1024
    return pl.pallas_call(
        _mod_kernel,
        grid=(depth, n // tn),
        in_specs=[pl.BlockSpec((bsz, d), lambda l, j: (0, 0)),
                  pl.BlockSpec((1, d, tn), lambda l, j: (l, 0, j)),
                  pl.BlockSpec((1, 1, tn), lambda l, j: (l, 0, j))],
        out_specs=pl.BlockSpec((1, bsz, tn), lambda l, j: (l, 0, j)),
        out_shape=jax.ShapeDtypeStruct((depth, bsz, n), jnp.float32),
        compiler_params=_params(2),
        name="modulation",
    )(c, w_mod, b_mod.reshape(depth, 1, n))


def _inproj_kernel(x_ref, shift_ref, scale_ref, g_ref, w_ref, wg_ref, cw_ref, cb_ref,
                   proj_ref, gates_ref, conv_ref, *, tm, sub, nchunk):
    s = pl.program_id(1)
    f32 = jnp.float32

    @pl.when(s == 0)
    def _():
        conv_ref[0:8, :] = jnp.zeros((8, M_WP), f32)

    subs = [slice(i * sub, (i + 1) * sub) for i in range(tm // sub)]
    hs = []
    for r in subs:
        x = x_ref[0, r, :]
        ms = jnp.mean(x * x, axis=-1, keepdims=True)
        h = x * lax.rsqrt(ms + EPS) * g_ref[...]
        hs.append((h * (1.0 + scale_ref[...]) + shift_ref[...]).astype(jnp.bfloat16))

    for i, (r, h) in enumerate(zip(subs, hs)):
        gates_ref[0, :, r] = lax.dot_general(wg_ref[...], h, _NT, preferred_element_type=f32)

        base = 8 + i * sub
        conv_ref[base:base + sub, :] = jnp.dot(h, w_ref[:, OFF_MU:OFF_MU + M_WP], preferred_element_type=f32)
        acc = cb_ref[...] + cw_ref[M_CONV - 1:M_CONV, :] * conv_ref[base:base + sub, :]
        for j in range(M_CONV - 1):
            lag = M_CONV - 1 - j
            acc = acc + cw_ref[j:j + 1, :] * conv_ref[base - lag:base - lag + sub, :]
        proj_ref[0, r, OFF_MU:OFF_MU + M_WP] = _silu(acc).astype(jnp.bfloat16)

        for c0 in range(OFF_MV, N_PACK, nchunk):
            proj_ref[0, r, c0:c0 + nchunk] = jnp.dot(
                h, w_ref[:, c0:c0 + nchunk], preferred_element_type=f32).astype(jnp.bfloat16)

    conv_ref[0:8, :] = conv_ref[tm:tm + 8, :]


def _inproj(x, mod4, layer, g_pre, w_pack, wg_t, conv_w, conv_b):
    bsz, seq, d = x.shape
    tm = 1024
    kern = functools.partial(_inproj_kernel, tm=tm, sub=128, nchunk=512)
    const = lambda shape: pl.BlockSpec(shape, lambda b, s: (0,) * len(shape), pipeline_mode=pl.Buffered(1))
    return pl.pallas_call(
        kern,
        grid=(bsz, seq // tm),
        in_specs=[pl.BlockSpec((1, tm, d), lambda b, s: (b, s, 0)),
                  pl.BlockSpec((None, None, 1, d), lambda b, s: (layer, b, 0, 0)),
                  pl.BlockSpec((None, None, 1, d), lambda b, s: (layer, b, 0, 1)),
                  const((1, d)), const((d, N_PACK)), const((8, d)), const((M_CONV, M_WP)), const((1, M_WP))],
        out_specs=[pl.BlockSpec((1, tm, N_PACK), lambda b, s: (b, s, 0)),
                   pl.BlockSpec((1, 8, tm), lambda b, s: (b, 0, s))],
        out_shape=[jax.ShapeDtypeStruct((bsz, seq, N_PACK), jnp.bfloat16),
                   jax.ShapeDtypeStruct((bsz, 8, seq), jnp.float32)],
        scratch_shapes=[pltpu.VMEM((tm + 8, M_WP), jnp.float32)],
        compiler_params=_params(2),
        name="inproj",
    )(x, mod4, mod4, g_pre, w_pack, wg_t, conv_w, conv_b)


def _mlstm_kernel(cu_ref, v_ref, z_ref, gates_ref, gbias_ref, wq_ref, wkt_ref, hg_ref, skip_ref,
                  o_ref, st_ref, a_ref, b_ref, mp_ref, ml_ref, *, seq):
    L = M_CHUNK
    nchunks = seq // L
    f32 = jnp.float32
    bf16 = jnp.bfloat16
    lane8 = lax.broadcasted_iota(jnp.int32, (8, L), 1)
    row8 = lax.broadcasted_iota(jnp.int32, (8, L), 0)

    amax, gsum = [], []
    for c in range(nchunks):
        g = gates_ref[0, :, c * L:(c + 1) * L] + gbias_ref[...]
        cum = jnp.where(row8 >= M_HEADS, _log_sigmoid(g), 0.0)
        k = 1
        while k < L:
            cum = cum + jnp.where(lane8 >= k, pltpu.roll(cum, k, axis=1), 0.0)
            k *= 2
        b8 = pltpu.roll(cum, M_HEADS, axis=0)
        a8 = g - b8
        a_ref[:, c * L:(c + 1) * L] = a8
        b_ref[:, c * L:(c + 1) * L] = b8
        amax.append(jnp.broadcast_to(jnp.max(a8, axis=1, keepdims=True), (8, L)))
        gsum.append(jnp.broadcast_to(jnp.max(jnp.where(lane8 == L - 1, b8, -jnp.inf), axis=1, keepdims=True), (8, L)))
    m_prev = jnp.zeros((8, L), f32)
    for c in range(nchunks):
        m_last = jnp.maximum(m_prev, amax[c])
        mp_ref[:, c * L:(c + 1) * L] = m_prev
        ml_ref[:, c * L:(c + 1) * L] = m_last
        m_prev = gsum[c] + m_last

    st_ref[...] = jnp.zeros(st_ref.shape, f32)

    t_idx = lax.broadcasted_iota(jnp.int32, (L, L), 0)
    s_idx = lax.broadcasted_iota(jnp.int32, (L, L), 1)
    tri = s_idx <= t_idx
    eye = s_idx == t_idx
    ones = jnp.ones((L, LANES), bf16)
    scale = M_DH ** -0.5
    heads = range(M_HEADS)
    cols = [slice(h * LANES, (h + 1) * LANES) for h in heads]

    def chunk(c, carry):
        r0 = pl.multiple_of(c * L, L)
        rows = pl.ds(r0, L)
        cu = [cu_ref[0, rows, cols[h]] for h in heads]
        q = [(jnp.dot(cu[h], wq_ref[h], preferred_element_type=f32) * scale).astype(bf16) for h in heads]
        kt = [lax.dot_general(wkt_ref[h], cu[h], _NT, preferred_element_type=f32) for h in heads]
        st = [st_ref[h] for h in heads]
        qk = [jnp.dot(q[h], kt[h].astype(bf16), preferred_element_type=f32) for h in heads]
        qs = [jnp.dot(q[h], st[h].astype(bf16), preferred_element_type=f32) for h in heads]
        smat, mmb, bb = [], [], []
        for h in heads:
            a_mat = jnp.broadcast_to(a_ref[h:h + 1, rows], (L, L))
            m1 = jnp.max(jnp.where(tri, a_mat, -jnp.inf), axis=1, keepdims=True)
            mm = jnp.maximum(mp_ref[h:h + 1, rows], m1)
            p = jnp.where(tri, jnp.exp(a_mat - mm), 0.0)
            smat.append((qk[h] * p).astype(bf16))
            mmb.append(mm)
            b_mat = jnp.broadcast_to(b_ref[h:h + 1, rows], (L, L))
            bb.append(jnp.broadcast_to(jnp.sum(jnp.where(eye, b_mat, 0.0), axis=1, keepdims=True), (L, LANES)))
        v2 = [jnp.concatenate([v_ref[0, rows, cols[h]], ones], axis=1) for h in heads]
        sv = [jnp.dot(smat[h], v2[h], preferred_element_type=f32) for h in heads]
        upd = []
        for h in heads:
            ws = jnp.exp(a_ref[h:h + 1, rows] - ml_ref[h:h + 1, rows])
            upd.append(jnp.dot((kt[h] * ws).astype(bf16), v2[h], preferred_element_type=f32))
        for h in heads:
            mp_row = mp_ref[h:h + 1, rows]
            w_inter = jnp.exp(mp_row - mmb[h])
            num = w_inter * qs[h][:, :LANES] + sv[h][:, :LANES]
            den = w_inter * qs[h][:, LANES:] + sv[h][:, LANES:]
            hh = num / jnp.maximum(jnp.abs(den), jnp.exp(-(bb[h] + mmb[h])))
            hh = hh * lax.rsqrt(jnp.sum(hh * hh, axis=1, keepdims=True) * (1.0 / M_DH) + EPS)
            y = hh * hg_ref[:, cols[h]] + skip_ref[:, cols[h]] * cu[h].astype(f32)
            y = y * _silu(z_ref[0, rows, cols[h]].astype(f32))
            o_ref[0, rows, cols[h]] = y.astype(o_ref.dtype)
            cs = jnp.exp(mp_row - ml_ref[h:h + 1, rows])
            st_ref[h] = jnp.concatenate([cs, cs], axis=1) * st[h] + upd[h]
        return carry

    lax.fori_loop(0, nchunks, chunk, 0, unroll=4)


def _mlstm(proj, gates, gbias, wq, wkt, hn_g, skip):
    bsz, seq, _ = proj.shape
    kern = functools.partial(_mlstm_kernel, seq=seq)
    blk = lambda j: pl.BlockSpec((1, seq, M_WP), lambda b: (b, 0, j))
    full = lambda shape: pl.BlockSpec(shape, lambda b: (0,) * len(shape))
    rows = pltpu.VMEM((8, seq), jnp.float32)
    return pl.pallas_call(
        kern,
        grid=(bsz,),
        in_specs=[blk(OFF_MU // M_WP), blk(OFF_MV // M_WP), blk(OFF_MZ // M_WP),
                  pl.BlockSpec((1, 8, seq), lambda b: (b, 0, 0)),
                  full((8, M_CHUNK)), full((M_HEADS, LANES, LANES)), full((M_HEADS, LANES, LANES)),
                  full((1, M_WP)), full((1, M_WP))],
        out_specs=pl.BlockSpec((1, seq, M_WP), lambda b: (b, 0, 0)),
        out_shape=jax.ShapeDtypeStruct((bsz, seq, M_WP), jnp.bfloat16),
        scratch_shapes=[pltpu.VMEM((M_HEADS, LANES, 2 * LANES), jnp.float32), rows, rows, rows, rows],
        compiler_params=_params(1),
        name="mlstm",
    )(proj, proj, proj, gates, gbias, wq, wkt, hn_g, skip)


def _swa_kernel(q_ref, z_ref, k_ref, v_ref, slope_ref, sink_ref, o_ref, *, seq):
    f32 = jnp.float32
    bf16 = jnp.bfloat16
    W = WINDOW
    ti = lax.broadcasted_iota(jnp.int32, (W, W), 0)
    si = lax.broadcasted_iota(jnp.int32, (W, W), 1)
    cur = si <= ti
    neg_rel = -jnp.where(cur, ti - si, W + ti - si).astype(f32)
    lane = lax.broadcasted_iota(jnp.int32, (W, LANES), 1)
    lane2 = lax.broadcasted_iota(jnp.int32, (2 * W, LANES), 1)
    halves = (lane < A_DH, lane >= A_DH)
    pairs = [(g, kv) for g in range(A_G) for kv in range(A_KV)]

    def block(n, carry):
        rows = pl.ds(pl.multiple_of(n * W, W), W)
        prev = pl.ds(pl.multiple_of(jnp.maximum(n - 1, 0) * W, W), W)
        bias = jnp.where((cur.astype(jnp.int32) + n) > 0, neg_rel, -jnp.inf)
        kk = jnp.concatenate([k_ref[0, rows, :], k_ref[0, prev, :]], axis=0)
        vv = jnp.concatenate([v_ref[0, rows, :], v_ref[0, prev, :]], axis=0)
        vms = [jnp.where(lane2 < A_DH, vv, jnp.zeros_like(vv)), jnp.where(lane2 >= A_DH, vv, jnp.zeros_like(vv))]
        q_all = q_ref[0, rows, :]
        q_all = q_all * jnp.asarray(A_DH ** -0.5, q_all.dtype)
        sc = []
        for g, kv in pairs:
            qg = q_all[:, g * LANES:(g + 1) * LANES]
            qm = jnp.where(halves[kv], qg, jnp.zeros_like(qg))
            sc.append(lax.dot_general(qm, kk, _NT, preferred_element_type=f32))
        pc = []
        for i, (g, kv) in enumerate(pairs):
            head = kv * A_G + g
            s2 = jnp.where(cur, sc[i][:, :W], sc[i][:, W:]) + slope_ref[head] * bias
            sink = sink_ref[head]
            mx = jnp.maximum(jnp.max(s2, axis=1, keepdims=True), sink)
            e = jnp.exp(s2 - mx)
            den = jnp.sum(e, axis=1, keepdims=True) + jnp.exp(sink - mx)
            probs = (e * (1.0 / den)).astype(bf16)
            zero = jnp.zeros_like(probs)
            pc.append(jnp.concatenate([jnp.where(cur, probs, zero), jnp.where(cur, zero, probs)], axis=1))
        for g in range(A_G):
            cols = slice(g * LANES, (g + 1) * LANES)
            o = (jnp.dot(pc[2 * g], vms[0], preferred_element_type=f32)
                 + jnp.dot(pc[2 * g + 1], vms[1], preferred_element_type=f32))
            o_ref[0, rows, cols] = (o * _silu(z_ref[0, rows, cols].astype(f32))).astype(o_ref.dtype)
        return carry

    lax.fori_loop(0, seq // W, block, 0, unroll=8)


def _swa(proj, slopes, sinks):
    bsz, seq, _ = proj.shape
    kern = functools.partial(_swa_kernel, seq=seq)
    smem = pl.BlockSpec(memory_space=pltpu.SMEM)
    full = lambda off, w: pl.BlockSpec((1, seq, w), lambda b: (b, 0, off // w))
    return pl.pallas_call(
        kern,
        grid=(bsz,),
        in_specs=[full(OFF_AQ, A_W), full(OFF_AZ, A_W), full(OFF_AK, A_KVW), full(OFF_AV, A_KVW), smem, smem],
        out_specs=pl.BlockSpec((1, seq, A_W), lambda b: (b, 0, 0)),
        out_shape=jax.ShapeDtypeStruct((bsz, seq, A_W), jnp.bfloat16),
        compiler_params=_params(1),
        name="swa",
    )(proj, proj, proj, proj, slopes, sinks)


def _sb_kernel(q_ref, z_ref, k_ref, v_ref, o_ref, acc_ref, run_ref, zz_ref, a_ref, *, seq):
    f32 = jnp.float32
    bf16 = jnp.bfloat16
    T = SB_TILE
    lane = lax.broadcasted_iota(jnp.int32, (T, LANES), 1)
    halves = (lane < S_DH, lane >= S_DH)
    ti = lax.broadcasted_iota(jnp.int32, (T, T), 0)
    si = lax.broadcasted_iota(jnp.int32, (T, T), 1)
    strict = si < ti
    usum = jnp.where(ti > si, 1.0, 0.0).astype(bf16)
    heads = range(S_HEADS)
    cols = [slice((h // 2) * LANES, (h // 2 + 1) * LANES) for h in heads]

    def qblock(n, carry):
        r0 = pl.multiple_of(n * T, T)
        q = q_ref[0, pl.ds(r0, T), :]
        q = q * jnp.asarray(S_DH ** -0.5, q.dtype)
        qm = [jnp.where(halves[h % 2], q[:, cols[h]], jnp.zeros((T, LANES), q.dtype)) for h in heads]
        acc_ref[...] = jnp.zeros(acc_ref.shape, f32)
        run_ref[...] = jnp.zeros(run_ref.shape, f32)

        def score(j):
            k0 = pl.multiple_of(j * T, T)
            kb = k_ref[0, pl.ds(k0, T), :]
            for h in heads:
                zz_ref[h] = lax.dot_general(qm[h], kb[:, cols[h]], _NT, preferred_element_type=f32)

        def apply(j):
            k0 = pl.multiple_of(j * T, T)
            vb = v_ref[0, pl.ds(k0, T), :]
            for p in range(S_PAIRS):
                pv = None
                for h in (2 * p, 2 * p + 1):
                    vm = jnp.where(halves[h % 2], vb[:, cols[h]], jnp.zeros((T, LANES), vb.dtype))
                    d = jnp.dot(a_ref[h], vm, preferred_element_type=f32)
                    pv = d if pv is None else pv + d
                acc_ref[:, cols[2 * p]] += pv

        def weights(j_next, diag):
            ls, lk = [], []
            for h in heads:
                zz = zz_ref[h]
                soft = jnp.log(1.0 + jnp.exp2(jnp.abs(zz) * (-LOG2E)))
                ls.append(jnp.minimum(zz, 0.0) - soft)
                lkh = ls[h] - zz
                lk.append(jnp.where(strict, lkh, 0.0) if diag else lkh)
            suf = [jnp.dot(lk[h].astype(bf16), usum, preferred_element_type=f32) for h in heads]
            runs = [run_ref[h] for h in heads]
            new_runs = [runs[h] + jnp.sum(lk[h], axis=1, keepdims=True) for h in heads]
            live = jnp.max(functools.reduce(jnp.maximum, new_runs)) > SB_DEAD
            score(j_next)
            for h in heads:
                ah = jnp.exp2((ls[h] + suf[h] + jnp.concatenate([runs[h]] * (T // LANES), axis=1)) * LOG2E)
                a_ref[h] = (jnp.where(strict, ah, 0.0) if diag else ah).astype(bf16)
                run_ref[h] = new_runs[h]
            return live.astype(jnp.int32)

        score(n)
        live0 = weights(max(n - 1, 0), True)

        def cond(c):
            i, live = c
            return jnp.logical_and(i <= n, live > 0)

        def body(c):
            i, _ = c
            apply(n - i + 1)
            return i + 1, weights(jnp.maximum(n - i - 1, 0), False)

        i_end, _ = lax.while_loop(cond, body, (jnp.int32(1), live0))
        apply(n - i_end + 1)
        zg = z_ref[0, pl.ds(r0, T), :].astype(f32)
        o_ref[0, pl.ds(r0, T), :] = (acc_ref[...] * _silu(zg)).astype(o_ref.dtype)
        return carry

    for n in range(seq // T):
        qblock(n, 0)


def _stickbreak(proj):
    bsz, seq, _ = proj.shape
    kern = functools.partial(_sb_kernel, seq=seq)
    full = lambda off: pl.BlockSpec((1, seq, S_W), lambda b: (b, 0, off // S_W))
    return pl.pallas_call(
        kern,
        grid=(bsz,),
        in_specs=[full(OFF_SQ), full(OFF_SZ), full(OFF_SK), full(OFF_SV)],
        out_specs=pl.BlockSpec((1, seq, S_W), lambda b: (b, 0, 0)),
        out_shape=jax.ShapeDtypeStruct((bsz, seq, S_W), jnp.bfloat16),
        scratch_shapes=[pltpu.VMEM((SB_TILE, S_W), jnp.float32),
                        pltpu.VMEM((S_HEADS, SB_TILE, LANES), jnp.float32),
                        pltpu.VMEM((S_HEADS, SB_TILE, SB_TILE), jnp.float32),
                        pltpu.VMEM((S_HEADS, SB_TILE, SB_TILE), jnp.bfloat16)],
        compiler_params=_params(1),
        name="stickbreak",
    )(proj, proj, proj, proj)


def _outproj_kernel(x_ref, ym_ref, ya_ref, ys_ref, wm_ref, wa_ref, ws_ref, g_ref, gate_ref, o_ref, *, tm, sub):
    f32 = jnp.float32
    for i in range(tm // sub):
        r = slice(i * sub, (i + 1) * sub)
        y = (jnp.dot(ym_ref[0, r, :], wm_ref[...], preferred_element_type=f32)
             + jnp.dot(ya_ref[0, r, :], wa_ref[...], preferred_element_type=f32)
             + jnp.dot(ys_ref[0, r, :], ws_ref[...], preferred_element_type=f32))
        yn = y * lax.rsqrt(jnp.mean(y * y, axis=-1, keepdims=True) + EPS) * g_ref[...]
        o_ref[0, r, :] = x_ref[0, r, :] + gate_ref[...] * yn


def _outproj(x, ym, ya, ys, wm, wa, ws, g_post, mod4, layer):
    bsz, seq, d = x.shape
    tm = 1024
    kern = functools.partial(_outproj_kernel, tm=tm, sub=256)
    row = lambda w: pl.BlockSpec((1, tm, w), lambda b, s: (b, s, 0))
    const = lambda shape: pl.BlockSpec(shape, lambda b, s: (0,) * len(shape), pipeline_mode=pl.Buffered(1))
    return pl.pallas_call(
        kern,
        grid=(bsz, seq // tm),
        in_specs=[row(d), row(M_WP), row(A_W), row(S_W),
                  const((M_WP, d)), const((A_W, d)), const((S_W, d)), const((1, d)),
                  pl.BlockSpec((None, None, 1, d), lambda b, s: (layer, b, 0, 2))],
        out_specs=row(d),
        out_shape=jax.ShapeDtypeStruct((bsz, seq, d), jnp.float32),
        compiler_params=_params(2),
        name="outproj",
    )(x, ym, ya, ys, wm, wa, ws, g_post, mod4)


def _pad_heads(w, axis):
    shape = list(w.shape)
    shape[axis:axis + 1] = [M_HEADS, M_DH]
    w = w.reshape(shape)
    pad = [(0, 0)] * w.ndim
    pad[axis + 1] = (0, LANES - M_DH)
    w = jnp.pad(w, pad)
    shape[axis:axis + 2] = [M_WP]
    return w.reshape(shape)


def _pair_heads(w, axis):
    shape = list(w.shape)
    shape[axis:axis + 1] = [A_KV, A_G, A_DH]
    w = jnp.swapaxes(w.reshape(shape), axis, axis + 1)
    shape[axis:axis + 3] = [A_W]
    return w.reshape(shape)


def _pack_w_in(w):
    o = np.cumsum([0, M_W, M_W, M_HEADS, M_HEADS, M_W, A_W, A_KVW, A_KVW, A_W, S_W, S_W, S_W, S_W])
    w = w.astype(jnp.bfloat16)
    seg = lambda i: w[:, int(o[i]):int(o[i + 1])]
    packed = jnp.concatenate(
        [_pad_heads(seg(0), 1), _pad_heads(seg(1), 1), _pad_heads(seg(4), 1),
         _pair_heads(seg(5), 1), _pair_heads(seg(8), 1), seg(6), seg(7),
         seg(9), seg(10), seg(11), seg(12)], axis=1)
    gates_t = jnp.concatenate([seg(2), seg(3)], axis=1).T
    return packed, gates_t


def _pack_w_out(w):
    w = w.astype(jnp.bfloat16)
    return _pad_heads(w[:M_W], 0), _pair_heads(w[M_W:M_W + A_W], 0), w[M_W + A_W:]


def _pad_qk(w):
    return jnp.pad(w, ((0, 0), (0, LANES - M_DH), (0, LANES - M_DH))).astype(jnp.bfloat16)


def kernel(x, c, w_mod, b_mod, g_pre, g_post, w_in, m_conv_w, m_conv_b, m_wq, m_wk, m_b_i, m_b_f,
           m_norm_g, m_skip, a_sinks, w_out):
    bsz = x.shape[0]
    mod4 = _modulation(c, w_mod, b_mod).reshape(DEPTH, bsz, 1, 3 * D_MODEL)
    slopes = jnp.asarray(2.0 ** (-8.0 * np.arange(1, A_HEADS + 1) / A_HEADS), dtype=jnp.float32)
    for l in range(DEPTH):
        w_pack, wg_t = _pack_w_in(w_in[l])
        wm, wa, ws = _pack_w_out(w_out[l])
        gbias = jnp.broadcast_to(jnp.concatenate([m_b_i[l], m_b_f[l]])[:, None], (2 * M_HEADS, M_CHUNK))
        proj, gates = _inproj(x, mod4, l, g_pre[l][None], w_pack, wg_t,
                              _pad_heads(m_conv_w[l], 1), _pad_heads(m_conv_b[l][None], 1))
        ym = _mlstm(proj, gates, gbias, _pad_qk(m_wq[l]), _pad_qk(jnp.swapaxes(m_wk[l], 1, 2)),
                    _pad_heads(m_norm_g[l][None], 1), _pad_heads(m_skip[l][None], 1))
        ya = _swa(proj, slopes, a_sinks[l])
        ys = _stickbreak(proj)
        x = _outproj(x, ym, ya, ys, wm, wa, ws, g_post[l][None], mod4, l)
    return x
```

```python
import functools

import jax
import jax.numpy as jnp
import numpy as np
from jax import lax
from jax.experimental import pallas as pl
from jax.experimental.pallas import tpu as pltpu

D_MODEL = 1024
DEPTH = 2
M_HEADS = 4
M_DH = 96
M_W = M_HEADS * M_DH
M_CONV = 4
A_HEADS = 6
A_KV = 2
A_G = A_HEADS // A_KV
A_DH = 64
A_W = A_HEADS * A_DH
A_KVW = A_KV * A_DH
WINDOW = 128
S_HEADS = 4
S_DH = 64
S_W = S_HEADS * S_DH
S_BLOCK = 128
EPS = 1e-6

LANES = 128
M_CHUNK = 128
M_WP = M_HEADS * LANES
SB_TILE = 256
S_PAIRS = S_W // LANES
SB_DEAD = -93.0

OFF_MU, OFF_MV, OFF_MZ = 0, M_WP, 2 * M_WP
OFF_AQ = 3 * M_WP
OFF_AZ = OFF_AQ + A_W
OFF_AK = OFF_AZ + A_W
OFF_AV = OFF_AK + A_KVW
OFF_SQ = OFF_AV + A_KVW
OFF_SK = OFF_SQ + S_W
OFF_SV = OFF_SK + S_W
OFF_SZ = OFF_SV + S_W
N_PACK = OFF_SZ + S_W

VMEM_LIMIT = 48 * 1024 * 1024

_NT = (((1,), (1,)), ((), ()))
LOG2E = 1.4426950408889634


def _log_sigmoid(x):
    return jnp.minimum(x, 0.0) - jnp.log(1.0 + jnp.exp(-jnp.abs(x)))


def _silu(x):
    u = 0.5 * x
    return u + u * jnp.tanh(u)


def _params(n_axes):
    return pltpu.CompilerParams(dimension_semantics=("arbitrary",) * n_axes, vmem_limit_bytes=VMEM_LIMIT)


def _mod_kernel(c_ref, w_ref, b_ref, o_ref):
    c_act = _silu(c_ref[...]).astype(jnp.bfloat16)
    o_ref[0] = jnp.dot(c_act, w_ref[0].astype(jnp.bfloat16), preferred_element_type=jnp.float32) + b_ref[0]


def _modulation(c, w_mod, b_mod):
    depth, d, n = w_mod.shape
    bsz = c.shape[0]
    tn = 1024
    return pl.pallas_call(
        _mod_kernel,
        grid=(depth, n // tn),
        in_specs=[pl.BlockSpec((bsz, d), lambda l, j: (0, 0)),
                  pl.BlockSpec((1, d, tn), lambda l, j: (l, 0, j)),
                  pl.BlockSpec((1, 1, tn), lambda l, j: (l, 0, j))],
        out_specs=pl.BlockSpec((1, bsz, tn), lambda l, j: (l, 0, j)),
        out_shape=jax.ShapeDtypeStruct((depth, bsz, n), jnp.float32),
        compiler_params=_params(2),
        name="modulation",
    )(c, w_mod, b_mod.reshape(depth, 1, n))


def _inproj_kernel(x_ref, shift_ref, scale_ref, g_ref, w_ref, wg_ref, cw_ref, cb_ref,
                   proj_ref, gates_ref, conv_ref, *, tm, sub, nchunk):
    s = pl.program_id(1)
    f32 = jnp.float32

    @pl.when(s == 0)
    def _():
        conv_ref[0:8, :] = jnp.zeros((8, M_WP), f32)

    subs = [slice(i * sub, (i + 1) * sub) for i in range(tm // sub)]
    hs = []
    for r in subs:
        x = x_ref[0, r, :]
        ms = jnp.mean(x * x, axis=-1, keepdims=True)
        h = x * lax.rsqrt(ms + EPS) * g_ref[...]
        hs.append((h * (1.0 + scale_ref[...]) + shift_ref[...]).astype(jnp.bfloat16))

    for i, (r, h) in enumerate(zip(subs, hs)):
        gates_ref[0, :, r] = lax.dot_general(wg_ref[...], h, _NT, preferred_element_type=f32)

        base = 8 + i * sub
        conv_ref[base:base + sub, :] = jnp.dot(h, w_ref[:, OFF_MU:OFF_MU + M_WP], preferred_element_type=f32)
        acc = cb_ref[...] + cw_ref[M_CONV - 1:M_CONV, :] * conv_ref[base:base + sub, :]
        for j in range(M_CONV - 1):
            lag = M_CONV - 1 - j
            acc = acc + cw_ref[j:j + 1, :] * conv_ref[base - lag:base - lag + sub, :]
        proj_ref[0, r, OFF_MU:OFF_MU + M_WP] = _silu(acc).astype(jnp.bfloat16)

        for c0 in range(OFF_MV, N_PACK, nchunk):
            proj_ref[0, r, c0:c0 + nchunk] = jnp.dot(
                h, w_ref[:, c0:c0 + nchunk], preferred_element_type=f32).astype(jnp.bfloat16)

    conv_ref[0:8, :] = conv_ref[tm:tm + 8, :]


def _inproj(x, mod4, layer, g_pre, w_pack, wg_t, conv_w, conv_b):
    bsz, seq, d = x.shape
    tm = 1024
    kern = functools.partial(_inproj_kernel, tm=tm, sub=128, nchunk=512)
    const = lambda shape: pl.BlockSpec(shape, lambda b, s: (0,) * len(shape), pipeline_mode=pl.Buffered(1))
    return pl.pallas_call(
        kern,
        grid=(bsz, seq // tm),
        in_specs=[pl.BlockSpec((1, tm, d), lambda b, s: (b, s, 0)),
                  pl.BlockSpec((None, None, 1, d), lambda b, s: (layer, b, 0, 0)),
                  pl.BlockSpec((None, None, 1, d), lambda b, s: (layer, b, 0, 1)),
                  const((1, d)), const((d, N_PACK)), const((8, d)), const((M_CONV, M_WP)), const((1, M_WP))],
        out_specs=[pl.BlockSpec((1, tm, N_PACK), lambda b, s: (b, s, 0)),
                   pl.BlockSpec((1, 8, tm), lambda b, s: (b, 0, s))],
        out_shape=[jax.ShapeDtypeStruct((bsz, seq, N_PACK), jnp.bfloat16),
                   jax.ShapeDtypeStruct((bsz, 8, seq), jnp.float32)],
        scratch_shapes=[pltpu.VMEM((tm + 8, M_WP), jnp.float32)],
        compiler_params=_params(2),
        name="inproj",
    )(x, mod4, mod4, g_pre, w_pack, wg_t, conv_w, conv_b)


def _mlstm_kernel(cu_ref, v_ref, z_ref, gates_ref, gbias_ref, wq_ref, wkt_ref, hg_ref, skip_ref,
                  o_ref, st_ref, a_ref, b_ref, mp_ref, ml_ref, *, seq):
    L = M_CHUNK
    nchunks = seq // L
    f32 = jnp.float32
    bf16 = jnp.bfloat16
    lane8 = lax.broadcasted_iota(jnp.int32, (8, L), 1)
    row8 = lax.broadcasted_iota(jnp.int32, (8, L), 0)

    amax, gsum = [], []
    for c in range(nchunks):
        g = gates_ref[0, :, c * L:(c + 1) * L] + gbias_ref[...]
        cum = jnp.where(row8 >= M_HEADS, _log_sigmoid(g), 0.0)
        k = 1
        while k < L:
            cum = cum + jnp.where(lane8 >= k, pltpu.roll(cum, k, axis=1), 0.0)
            k *= 2
        b8 = pltpu.roll(cum, M_HEADS, axis=0) * LOG2E
        a8 = g * LOG2E - b8
        a_ref[:, c * L:(c + 1) * L] = a8
        b_ref[:, c * L:(c + 1) * L] = b8
        amax.append(jnp.broadcast_to(jnp.max(a8, axis=1, keepdims=True), (8, L)))
        gsum.append(jnp.broadcast_to(jnp.max(jnp.where(lane8 == L - 1, b8, -jnp.inf), axis=1, keepdims=True), (8, L)))
    m_prev = jnp.zeros((8, L), f32)
    for c in range(nchunks):
        m_last = jnp.maximum(m_prev, amax[c])
        mp_ref[:, c * L:(c + 1) * L] = m_prev
        ml_ref[:, c * L:(c + 1) * L] = m_last
        m_prev = gsum[c] + m_last

    st_ref[...] = jnp.zeros(st_ref.shape, f32)

    t_idx = lax.broadcasted_iota(jnp.int32, (L, L), 0)
    s_idx = lax.broadcasted_iota(jnp.int32, (L, L), 1)
    tri = s_idx <= t_idx
    eye = s_idx == t_idx
    ones = jnp.ones((L, LANES), bf16)
    scale = M_DH ** -0.5
    heads = range(M_HEADS)
    cols = [slice(h * LANES, (h + 1) * LANES) for h in heads]

    def chunk(c, carry):
        r0 = pl.multiple_of(c * L, L)
        rows = pl.ds(r0, L)
        cu = [cu_ref[0, rows, cols[h]] for h in heads]
        q = [(jnp.dot(cu[h], wq_ref[h], preferred_element_type=f32) * scale).astype(bf16) for h in heads]
        kt = [lax.dot_general(wkt_ref[h], cu[h], _NT, preferred_element_type=f32) for h in heads]
        st = [st_ref[h] for h in heads]
        qk = [jnp.dot(q[h], kt[h].astype(bf16), preferred_element_type=f32) for h in heads]
        qs = [jnp.dot(q[h], st[h].astype(bf16), preferred_element_type=f32) for h in heads]
        smat, mmb, bb = [], [], []
        for h in heads:
            a_mat = jnp.broadcast_to(a_ref[h:h + 1, rows], (L, L))
            m1 = jnp.max(jnp.where(tri, a_mat, -jnp.inf), axis=1, keepdims=True)
            mm = jnp.maximum(mp_ref[h:h + 1, rows], m1)
            p = jnp.where(tri, jnp.exp2(a_mat - mm), 0.0)
            smat.append((qk[h] * p).astype(bf16))
            mmb.append(mm)
            b_mat = jnp.broadcast_to(b_ref[h:h + 1, rows], (L, L))
            bb.append(jnp.broadcast_to(jnp.sum(jnp.where(eye, b_mat, 0.0), axis=1, keepdims=True), (L, LANES)))
        v2 = [jnp.concatenate([v_ref[0, rows, cols[h]], ones], axis=1) for h in heads]
        sv = [jnp.dot(smat[h], v2[h], preferred_element_type=f32) for h in heads]
        upd = []
        for h in heads:
            ws = jnp.exp2(a_ref[h:h + 1, rows] - ml_ref[h:h + 1, rows])
            upd.append(jnp.dot((kt[h] * ws).astype(bf16), v2[h], preferred_element_type=f32))
        for h in heads:
            mp_row = mp_ref[h:h + 1, rows]
            w_inter = jnp.exp2(mp_row - mmb[h])
            num = w_inter * qs[h][:, :LANES] + sv[h][:, :LANES]
            den = w_inter * qs[h][:, LANES:] + sv[h][:, LANES:]
            hh = num / jnp.maximum(jnp.abs(den), jnp.exp2(-(bb[h] + mmb[h])))
            hh = hh * lax.rsqrt(jnp.sum(hh * hh, axis=1, keepdims=True) * (1.0 / M_DH) + EPS)
            y = hh * hg_ref[:, cols[h]] + skip_ref[:, cols[h]] * cu[h].astype(f32)
            y = y * _silu(z_ref[0, rows, cols[h]].astype(f32))
            o_ref[0, rows, cols[h]] = y.astype(o_ref.dtype)
            cs = jnp.exp2(mp_row - ml_ref[h:h + 1, rows])
            st_ref[h] = jnp.concatenate([cs, cs], axis=1) * st[h] + upd[h]
        return carry

    lax.fori_loop(0, nchunks, chunk, 0, unroll=4)


def _mlstm(proj, gates, gbias, wq, wkt, hn_g, skip):
    bsz, seq, _ = proj.shape
    kern = functools.partial(_mlstm_kernel, seq=seq)
    blk = lambda j: pl.BlockSpec((1, seq, M_WP), lambda b: (b, 0, j))
    full = lambda shape: pl.BlockSpec(shape, lambda b: (0,) * len(shape))
    rows = pltpu.VMEM((8, seq), jnp.float32)
    return pl.pallas_call(
        kern,
        grid=(bsz,),
        in_specs=[blk(OFF_MU // M_WP), blk(OFF_MV // M_WP), blk(OFF_MZ // M_WP),
                  pl.BlockSpec((1, 8, seq), lambda b: (b, 0, 0)),
                  full((8, M_CHUNK)), full((M_HEADS, LANES, LANES)), full((M_HEADS, LANES, LANES)),
                  full((1, M_WP)), full((1, M_WP))],
        out_specs=pl.BlockSpec((1, seq, M_WP), lambda b: (b, 0, 0)),
        out_shape=jax.ShapeDtypeStruct((bsz, seq, M_WP), jnp.bfloat16),
        scratch_shapes=[pltpu.VMEM((M_HEADS, LANES, 2 * LANES), jnp.float32), rows, rows, rows, rows],
        compiler_params=_params(1),
        name="mlstm",
    )(proj, proj, proj, gates, gbias, wq, wkt, hn_g, skip)


def _swa_kernel(q_ref, z_ref, k_ref, v_ref, slope_ref, sink_ref, o_ref, *, seq):
    f32 = jnp.float32
    bf16 = jnp.bfloat16
    W = WINDOW
    ti = lax.broadcasted_iota(jnp.int32, (W, W), 0)
    si = lax.broadcasted_iota(jnp.int32, (W, W), 1)
    cur = si <= ti
    neg_rel = -jnp.where(cur, ti - si, W + ti - si).astype(f32)
    lane = lax.broadcasted_iota(jnp.int32, (W, LANES), 1)
    lane2 = lax.broadcasted_iota(jnp.int32, (2 * W, LANES), 1)
    halves = (lane < A_DH, lane >= A_DH)
    pairs = [(g, kv) for g in range(A_G) for kv in range(A_KV)]

    def block(n, carry):
        rows = pl.ds(pl.multiple_of(n * W, W), W)
        prev = pl.ds(pl.multiple_of(jnp.maximum(n - 1, 0) * W, W), W)
        bias = jnp.where((cur.astype(jnp.int32) + n) > 0, neg_rel, -jnp.inf)
        kk = jnp.concatenate([k_ref[0, rows, :], k_ref[0, prev, :]], axis=0)
        vv = jnp.concatenate([v_ref[0, rows, :], v_ref[0, prev, :]], axis=0)
        vms = [jnp.where(lane2 < A_DH, vv, jnp.zeros_like(vv)), jnp.where(lane2 >= A_DH, vv, jnp.zeros_like(vv))]
        q_all = q_ref[0, rows, :]
        q_all = q_all * jnp.asarray(A_DH ** -0.5, q_all.dtype)
        sc = []
        for g, kv in pairs:
            qg = q_all[:, g * LANES:(g + 1) * LANES]
            qm = jnp.where(halves[kv], qg, jnp.zeros_like(qg))
            sc.append(lax.dot_general(qm, kk, _NT, preferred_element_type=f32))
        pc = []
        for i, (g, kv) in enumerate(pairs):
            head = kv * A_G + g
            s2 = jnp.where(cur, sc[i][:, :W], sc[i][:, W:]) + slope_ref[head] * bias
            sink = sink_ref[head]
            mx = jnp.maximum(jnp.max(s2, axis=1, keepdims=True), sink)
            e = jnp.exp(s2 - mx)
            den = jnp.sum(e, axis=1, keepdims=True) + jnp.exp(sink - mx)
            probs = (e * (1.0 / den)).astype(bf16)
            zero = jnp.zeros_like(probs)
            pc.append(jnp.concatenate([jnp.where(cur, probs, zero), jnp.where(cur, zero, probs)], axis=1))
        for g in range(A_G):
            cols = slice(g * LANES, (g + 1) * LANES)
            o = (jnp.dot(pc[2 * g], vms[0], preferred_element_type=f32)
                 + jnp.dot(pc[2 * g + 1], vms[1], preferred_element_type=f32))
            o_ref[0, rows, cols] = (o * _silu(z_ref[0, rows, cols].astype(f32))).astype(o_ref.dtype)
        return carry

    lax.fori_loop(0, seq // W, block, 0, unroll=8)


def _swa(proj, slopes, sinks):
    bsz, seq, _ = proj.shape
    kern = functools.partial(_swa_kernel, seq=seq)
    smem = pl.BlockSpec(memory_space=pltpu.SMEM)
    full = lambda off, w: pl.BlockSpec((1, seq, w), lambda b: (b, 0, off // w))
    return pl.pallas_call(
        kern,
        grid=(bsz,),
        in_specs=[full(OFF_AQ, A_W), full(OFF_AZ, A_W), full(OFF_AK, A_KVW), full(OFF_AV, A_KVW), smem, smem],
        out_specs=pl.BlockSpec((1, seq, A_W), lambda b: (b, 0, 0)),
        out_shape=jax.ShapeDtypeStruct((bsz, seq, A_W), jnp.bfloat16),
        compiler_params=_params(1),
        name="swa",
    )(proj, proj, proj, proj, slopes, sinks)


def _sb_kernel(q_ref, z_ref, k_ref, v_ref, o_ref, acc_ref, run_ref, zz_ref, a_ref, *, seq):
    f32 = jnp.float32
    bf16 = jnp.bfloat16
    T = SB_TILE
    lane = lax.broadcasted_iota(jnp.int32, (T, LANES), 1)
    halves = (lane < S_DH, lane >= S_DH)
    ti = lax.broadcasted_iota(jnp.int32, (T, T), 0)
    si = lax.broadcasted_iota(jnp.int32, (T, T), 1)
    strict = si < ti
    usum = jnp.where(ti > si, 1.0, 0.0).astype(bf16)
    heads = range(S_HEADS)
    cols = [slice((h // 2) * LANES, (h // 2 + 1) * LANES) for h in heads]

    def qblock(n, carry):
        r0 = pl.multiple_of(n * T, T)
        q = q_ref[0, pl.ds(r0, T), :]
        q = q * jnp.asarray(S_DH ** -0.5, q.dtype)
        qm = [jnp.where(halves[h % 2], q[:, cols[h]], jnp.zeros((T, LANES), q.dtype)) for h in heads]
        acc_ref[...] = jnp.zeros(acc_ref.shape, f32)
        run_ref[...] = jnp.zeros(run_ref.shape, f32)

        def score(j):
            k0 = pl.multiple_of(j * T, T)
            kb = k_ref[0, pl.ds(k0, T), :]
            for h in heads:
                zz_ref[h] = lax.dot_general(qm[h], kb[:, cols[h]], _NT, preferred_element_type=f32)

        def apply(j):
            k0 = pl.multiple_of(j * T, T)
            vb = v_ref[0, pl.ds(k0, T), :]
            for p in range(S_PAIRS):
                pv = None
                for h in (2 * p, 2 * p + 1):
                    vm = jnp.where(halves[h % 2], vb[:, cols[h]], jnp.zeros((T, LANES), vb.dtype))
                    d = jnp.dot(a_ref[h], vm, preferred_element_type=f32)
                    pv = d if pv is None else pv + d
                acc_ref[:, cols[2 * p]] += pv

        def weights(j_next, diag):
            ls, lk = [], []
            for h in heads:
                zz = zz_ref[h]
                soft = jnp.log(1.0 + jnp.exp2(jnp.abs(zz) * (-LOG2E)))
                ls.append(jnp.minimum(zz, 0.0) - soft)
                lkh = ls[h] - zz
                lk.append(jnp.where(strict, lkh, 0.0) if diag else lkh)
            suf = [jnp.dot(lk[h].astype(bf16), usum, preferred_element_type=f32) for h in heads]
            runs = [run_ref[h] for h in heads]
            new_runs = [runs[h] + jnp.sum(lk[h], axis=1, keepdims=True) for h in heads]
            live = jnp.max(functools.reduce(jnp.maximum, new_runs)) > SB_DEAD
            score(j_next)
            for h in heads:
                ah = jnp.exp2((ls[h] + suf[h] + jnp.concatenate([runs[h]] * (T // LANES), axis=1)) * LOG2E)
                a_ref[h] = (jnp.where(strict, ah, 0.0) if diag else ah).astype(bf16)
                run_ref[h] = new_runs[h]
            return live.astype(jnp.int32)

        score(n)
        live0 = weights(max(n - 1, 0), True)

        def cond(c):
            i, live = c
            return jnp.logical_and(i <= n, live > 0)

        def body(c):
            i, _ = c
            apply(n - i + 1)
            return i + 1, weights(jnp.maximum(n - i - 1, 0), False)

        i_end, _ = lax.while_loop(cond, body, (jnp.int32(1), live0))
        apply(n - i_end + 1)
        zg = z_ref[0, pl.ds(r0, T), :].astype(f32)
        o_ref[0, pl.ds(r0, T), :] = (acc_ref[...] * _silu(zg)).astype(o_ref.dtype)
        return carry

    for n in range(seq // T):
        qblock(n, 0)


def _stickbreak(proj):
    bsz, seq, _ = proj.shape
    kern = functools.partial(_sb_kernel, seq=seq)
    full = lambda off: pl.BlockSpec((1, seq, S_W), lambda b: (b, 0, off // S_W))
    return pl.pallas_call(
        kern,
        grid=(bsz,),
        in_specs=[full(OFF_SQ), full(OFF_SZ), full(OFF_SK), full(OFF_SV)],
        out_specs=pl.BlockSpec((1, seq, S_W), lambda b: (b, 0, 0)),
        out_shape=jax.ShapeDtypeStruct((bsz, seq, S_W), jnp.bfloat16),
        scratch_shapes=[pltpu.VMEM((SB_TILE, S_W), jnp.float32),
                        pltpu.VMEM((S_HEADS, SB_TILE, LANES), jnp.float32),
                        pltpu.VMEM((S_HEADS, SB_TILE, SB_TILE), jnp.float32),
                        pltpu.VMEM((S_HEADS, SB_TILE, SB_TILE), jnp.bfloat16)],
        compiler_params=_params(1),
        name="stickbreak",
    )(proj, proj, proj, proj)


def _outproj_kernel(x_ref, ym_ref, ya_ref, ys_ref, wm_ref, wa_ref, ws_ref, g_ref, gate_ref, o_ref, *, tm, sub):
    f32 = jnp.float32
    for i in range(tm // sub):
        r = slice(i * sub, (i + 1) * sub)
        y = (jnp.dot(ym_ref[0, r, :], wm_ref[...], preferred_element_type=f32)
             + jnp.dot(ya_ref[0, r, :], wa_ref[...], preferred_element_type=f32)
             + jnp.dot(ys_ref[0, r, :], ws_ref[...], preferred_element_type=f32))
        yn = y * lax.rsqrt(jnp.mean(y * y, axis=-1, keepdims=True) + EPS) * g_ref[...]
        o_ref[0, r, :] = x_ref[0, r, :] + gate_ref[...] * yn


def _outproj(x, ym, ya, ys, wm, wa, ws, g_post, mod4, layer):
    bsz, seq, d = x.shape
    tm = 1024
    kern = functools.partial(_outproj_kernel, tm=tm, sub=256)
    row = lambda w: pl.BlockSpec((1, tm, w), lambda b, s: (b, s, 0))
    const = lambda shape: pl.BlockSpec(shape, lambda b, s: (0,) * len(shape), pipeline_mode=pl.Buffered(1))
    return pl.pallas_call(
        kern,
        grid=(bsz, seq // tm),
        in_specs=[row(d), row(M_WP), row(A_W), row(S_W),
                  const((M_WP, d)), const((A_W, d)), const((S_W, d)), const((1, d)),
                  pl.BlockSpec((None, None, 1, d), lambda b, s: (layer, b, 0, 2))],
        out_specs=row(d),
        out_shape=jax.ShapeDtypeStruct((bsz, seq, d), jnp.float32),
        compiler_params=_params(2),
        name="outproj",
    )(x, ym, ya, ys, wm, wa, ws, g_post, mod4)


def _pad_heads(w, axis):
    shape = list(w.shape)
    shape[axis:axis + 1] = [M_HEADS, M_DH]
    w = w.reshape(shape)
    pad = [(0, 0)] * w.ndim
    pad[axis + 1] = (0, LANES - M_DH)
    w = jnp.pad(w, pad)
    shape[axis:axis + 2] = [M_WP]
    return w.reshape(shape)


def _pair_heads(w, axis):
    shape = list(w.shape)
    shape[axis:axis + 1] = [A_KV, A_G, A_DH]
    w = jnp.swapaxes(w.reshape(shape), axis, axis + 1)
    shape[axis:axis + 3] = [A_W]
    return w.reshape(shape)


def _pack_w_in(w):
    o = np.cumsum([0, M_W, M_W, M_HEADS, M_HEADS, M_W, A_W, A_KVW, A_KVW, A_W, S_W, S_W, S_W, S_W])
    w = w.astype(jnp.bfloat16)
    seg = lambda i: w[:, int(o[i]):int(o[i + 1])]
    packed = jnp.concatenate(
        [_pad_heads(seg(0), 1), _pad_heads(seg(1), 1), _pad_heads(seg(4), 1),
         _pair_heads(seg(5), 1), _pair_heads(seg(8), 1), seg(6), seg(7),
         seg(9), seg(10), seg(11), seg(12)], axis=1)
    gates_t = jnp.concatenate([seg(2), seg(3)], axis=1).T
    return packed, gates_t


def _pack_w_out(w):
    w = w.astype(jnp.bfloat16)
    return _pad_heads(w[:M_W], 0), _pair_heads(w[M_W:M_W + A_W], 0), w[M_W + A_W:]


def _pad_qk(w):
    return jnp.pad(w, ((0, 0), (0, LANES - M_DH), (0, LANES - M_DH))).astype(jnp.bfloat16)


def kernel(x, c, w_mod, b_mod, g_pre, g_post, w_in, m_conv_w, m_conv_b, m_wq, m_wk, m_b_i, m_b_f,
           m_norm_g, m_skip, a_sinks, w_out):
    bsz = x.shape[0]
    mod4 = _modulation(c, w_mod, b_mod).reshape(DEPTH, bsz, 1, 3 * D_MODEL)
    slopes = jnp.asarray(2.0 ** (-8.0 * np.arange(1, A_HEADS + 1) / A_HEADS), dtype=jnp.float32)
    for l in range(DEPTH):
        w_pack, wg_t = _pack_w_in(w_in[l])
        wm, wa, ws = _pack_w_out(w_out[l])
        gbias = jnp.broadcast_to(jnp.concatenate([m_b_i[l], m_b_f[l]])[:, None], (2 * M_HEADS, M_CHUNK))
        proj, gates = _inproj(x, mod4, l, g_pre[l][None], w_pack, wg_t,
                              _pad_heads(m_conv_w[l], 1), _pad_heads(m_conv_b[l][None], 1))
        ym = _mlstm(proj, gates, gbias, _pad_qk(m_wq[l]), _pad_qk(jnp.swapaxes(m_wk[l], 1, 2)),
                    _pad_heads(m_norm_g[l][None], 1), _pad_heads(m_skip[l][None], 1))
        ya = _swa(proj, slopes, a_sinks[l])
        ys = _stickbreak(proj)
        x = _outproj(x, ym, ya, ys, wm, wa, ws, g_post[l][None], mod4, l)
    return x
```

```python
import functools

import jax
import jax.numpy as jnp
import numpy as np
from jax import lax
from jax.experimental import pallas as pl
from jax.experimental.pallas import tpu as pltpu

D_MODEL = 1024
DEPTH = 2
M_HEADS = 4
M_DH = 96
M_W = M_HEADS * M_DH
M_CONV = 4
A_HEADS = 6
A_KV = 2
A_G = A_HEADS // A_KV
A_DH = 64
A_W = A_HEADS * A_DH
A_KVW = A_KV * A_DH
WINDOW = 128
S_HEADS = 4
S_DH = 64
S_W = S_HEADS * S_DH
S_BLOCK = 128
EPS = 1e-6

LANES = 128
M_CHUNK = 128
M_WP = M_HEADS * LANES
SB_TILE = 256
S_PAIRS = S_W // LANES
SB_DEAD = -93.0

OFF_MU, OFF_MV, OFF_MZ = 0, M_WP, 2 * M_WP
OFF_AQ = 3 * M_WP
OFF_AZ = OFF_AQ + A_W
OFF_AK = OFF_AZ + A_W
OFF_AV = OFF_AK + A_KVW
OFF_SQ = OFF_AV + A_KVW
OFF_SK = OFF_SQ + S_W
OFF_SV = OFF_SK + S_W
OFF_SZ = OFF_SV + S_W
N_PACK = OFF_SZ + S_W

VMEM_LIMIT = 48 * 1024 * 1024

_NT = (((1,), (1,)), ((), ()))
LOG2E = 1.4426950408889634


def _log_sigmoid(x):
    return jnp.minimum(x, 0.0) - jnp.log(1.0 + jnp.exp(-jnp.abs(x)))


def _silu(x):
    u = 0.5 * x
    return u + u * jnp.tanh(u)


def _params(n_axes):
    return pltpu.CompilerParams(dimension_semantics=("arbitrary",) * n_axes, vmem_limit_bytes=VMEM_LIMIT)


def _mod_kernel(c_ref, w_ref, b_ref, o_ref):
    c_act = _silu(c_ref[...]).astype(jnp.bfloat16)
    o_ref[0] = jnp.dot(c_act, w_ref[0].astype(jnp.bfloat16), preferred_element_type=jnp.float32) + b_ref[0]


def _modulation(c, w_mod, b_mod):
    depth, d, n = w_mod.shape
    bsz = c.shape[0]
    tn = 1024
    return pl.pallas_call(
        _mod_kernel,
        grid=(depth, n // tn),
        in_specs=[pl.BlockSpec((bsz, d), lambda l, j: (0, 0)),
                  pl.BlockSpec((1, d, tn), lambda l, j: (l, 0, j)),
                  pl.BlockSpec((1, 1, tn), lambda l, j: (l, 0, j))],
        out_specs=pl.BlockSpec((1, bsz, tn), lambda l, j: (l, 0, j)),
        out_shape=jax.ShapeDtypeStruct((depth, bsz, n), jnp.float32),
        compiler_params=_params(2),
        name="modulation",
    )(c, w_mod, b_mod.reshape(depth, 1, n))


def _inproj_kernel(x_ref, shift_ref, scale_ref, g_ref, w_ref, wg_ref, cw_ref, cb_ref,
                   proj_ref, gates_ref, conv_ref, *, tm, sub, nchunk):
    s = pl.program_id(1)
    f32 = jnp.float32

    @pl.when(s == 0)
    def _():
        conv_ref[0:8, :] = jnp.zeros((8, M_WP), f32)

    subs = [slice(i * sub, (i + 1) * sub) for i in range(tm // sub)]
    gain = g_ref[...] * (1.0 + scale_ref[...])
    hs = []
    for r in subs:
        x = x_ref[0, r, :]
        ms = jnp.mean(x * x, axis=-1, keepdims=True)
        hs.append((x * lax.rsqrt(ms + EPS) * gain + shift_ref[...]).astype(jnp.bfloat16))

    for i, (r, h) in enumerate(zip(subs, hs)):
        gates_ref[0, :, r] = lax.dot_general(wg_ref[...], h, _NT, preferred_element_type=f32)

        base = 8 + i * sub
        conv_ref[base:base + sub, :] = jnp.dot(h, w_ref[:, OFF_MU:OFF_MU + M_WP], preferred_element_type=f32)
        acc = cb_ref[...] + cw_ref[M_CONV - 1:M_CONV, :] * conv_ref[base:base + sub, :]
        for j in range(M_CONV - 1):
            lag = M_CONV - 1 - j
            acc = acc + cw_ref[j:j + 1, :] * conv_ref[base - lag:base - lag + sub, :]
        proj_ref[0, r, OFF_MU:OFF_MU + M_WP] = _silu(acc).astype(jnp.bfloat16)

        for c0 in range(OFF_MV, N_PACK, nchunk):
            proj_ref[0, r, c0:c0 + nchunk] = jnp.dot(
                h, w_ref[:, c0:c0 + nchunk], preferred_element_type=f32).astype(jnp.bfloat16)

    conv_ref[0:8, :] = conv_ref[tm:tm + 8, :]


def _inproj(x, mod4, layer, g_pre, w_pack, wg_t, conv_w, conv_b):
    bsz, seq, d = x.shape
    tm = 1024
    kern = functools.partial(_inproj_kernel, tm=tm, sub=128, nchunk=512)
    const = lambda shape: pl.BlockSpec(shape, lambda b, s: (0,) * len(shape), pipeline_mode=pl.Buffered(1))
    return pl.pallas_call(
        kern,
        grid=(bsz, seq // tm),
        in_specs=[pl.BlockSpec((1, tm, d), lambda b, s: (b, s, 0)),
                  pl.BlockSpec((None, None, 1, d), lambda b, s: (layer, b, 0, 0)),
                  pl.BlockSpec((None, None, 1, d), lambda b, s: (layer, b, 0, 1)),
                  const((1, d)), const((d, N_PACK)), const((8, d)), const((M_CONV, M_WP)), const((1, M_WP))],
        out_specs=[pl.BlockSpec((1, tm, N_PACK), lambda b, s: (b, s, 0)),
                   pl.BlockSpec((1, 8, tm), lambda b, s: (b, 0, s))],
        out_shape=[jax.ShapeDtypeStruct((bsz, seq, N_PACK), jnp.bfloat16),
                   jax.ShapeDtypeStruct((bsz, 8, seq), jnp.float32)],
        scratch_shapes=[pltpu.VMEM((tm + 8, M_WP), jnp.float32)],
        compiler_params=_params(2),
        name="inproj",
    )(x, mod4, mod4, g_pre, w_pack, wg_t, conv_w, conv_b)


def _mlstm_kernel(cu_ref, v_ref, z_ref, gates_ref, gbias_ref, wq_ref, wkt_ref, hg_ref, skip_ref,
                  o_ref, st_ref, a_ref, b_ref, mp_ref, ml_ref, *, seq):
    L = M_CHUNK
    nchunks = seq // L
    f32 = jnp.float32
    bf16 = jnp.bfloat16
    lane8 = lax.broadcasted_iota(jnp.int32, (8, L), 1)
    row8 = lax.broadcasted_iota(jnp.int32, (8, L), 0)

    amax, gsum = [], []
    for c in range(nchunks):
        g = gates_ref[0, :, c * L:(c + 1) * L] + gbias_ref[...]
        cum = jnp.where(row8 >= M_HEADS, _log_sigmoid(g), 0.0)
        k = 1
        while k < L:
            cum = cum + jnp.where(lane8 >= k, pltpu.roll(cum, k, axis=1), 0.0)
            k *= 2
        b8 = pltpu.roll(cum, M_HEADS, axis=0) * LOG2E
        a8 = g * LOG2E - b8
        a_ref[:, c * L:(c + 1) * L] = a8
        b_ref[:, c * L:(c + 1) * L] = b8
        amax.append(jnp.broadcast_to(jnp.max(a8, axis=1, keepdims=True), (8, L)))
        gsum.append(jnp.broadcast_to(jnp.max(jnp.where(lane8 == L - 1, b8, -jnp.inf), axis=1, keepdims=True), (8, L)))
    m_prev = jnp.zeros((8, L), f32)
    for c in range(nchunks):
        m_last = jnp.maximum(m_prev, amax[c])
        mp_ref[:, c * L:(c + 1) * L] = m_prev
        ml_ref[:, c * L:(c + 1) * L] = m_last
        m_prev = gsum[c] + m_last

    st_ref[...] = jnp.zeros(st_ref.shape, f32)

    t_idx = lax.broadcasted_iota(jnp.int32, (L, L), 0)
    s_idx = lax.broadcasted_iota(jnp.int32, (L, L), 1)
    tri = s_idx <= t_idx
    eye = s_idx == t_idx
    ones = jnp.ones((L, LANES), bf16)
    scale = M_DH ** -0.5
    heads = range(M_HEADS)
    cols = [slice(h * LANES, (h + 1) * LANES) for h in heads]

    def chunk(c, carry):
        r0 = pl.multiple_of(c * L, L)
        rows = pl.ds(r0, L)
        cu = [cu_ref[0, rows, cols[h]] for h in heads]
        q = [(jnp.dot(cu[h], wq_ref[h], preferred_element_type=f32) * scale).astype(bf16) for h in heads]
        kt = [lax.dot_general(wkt_ref[h], cu[h], _NT, preferred_element_type=f32) for h in heads]
        st = [st_ref[h] for h in heads]
        qk = [jnp.dot(q[h], kt[h].astype(bf16), preferred_element_type=f32) for h in heads]
        qs = [jnp.dot(q[h], st[h].astype(bf16), preferred_element_type=f32) for h in heads]
        smat, mmb, bb = [], [], []
        for h in heads:
            a_mat = jnp.broadcast_to(a_ref[h:h + 1, rows], (L, L))
            m1 = jnp.max(jnp.where(tri, a_mat, -jnp.inf), axis=1, keepdims=True)
            mm = jnp.maximum(mp_ref[h:h + 1, rows], m1)
            p = jnp.where(tri, jnp.exp2(a_mat - mm), 0.0)
            smat.append((qk[h] * p).astype(bf16))
            mmb.append(mm)
            b_mat = jnp.broadcast_to(b_ref[h:h + 1, rows], (L, L))
            bb.append(jnp.broadcast_to(jnp.sum(jnp.where(eye, b_mat, 0.0), axis=1, keepdims=True), (L, LANES)))
        v2 = [jnp.concatenate([v_ref[0, rows, cols[h]], ones], axis=1) for h in heads]
        sv = [jnp.dot(smat[h], v2[h], preferred_element_type=f32) for h in heads]
        upd = []
        for h in heads:
            ws = jnp.exp2(a_ref[h:h + 1, rows] - ml_ref[h:h + 1, rows])
            upd.append(jnp.dot((kt[h] * ws).astype(bf16), v2[h], preferred_element_type=f32))
        for h in heads:
            mp_row = mp_ref[h:h + 1, rows]
            w_inter = jnp.exp2(mp_row - mmb[h])
            num = w_inter * qs[h][:, :LANES] + sv[h][:, :LANES]
            den = w_inter * qs[h][:, LANES:] + sv[h][:, LANES:]
            hh = num / jnp.maximum(jnp.abs(den), jnp.exp2(-(bb[h] + mmb[h])))
            hh = hh * lax.rsqrt(jnp.sum(hh * hh, axis=1, keepdims=True) * (1.0 / M_DH) + EPS)
            y = hh * hg_ref[:, cols[h]] + skip_ref[:, cols[h]] * cu[h].astype(f32)
            y = y * _silu(z_ref[0, rows, cols[h]].astype(f32))
            o_ref[0, rows, cols[h]] = y.astype(o_ref.dtype)
            cs = jnp.exp2(mp_row - ml_ref[h:h + 1, rows])
            st_ref[h] = jnp.concatenate([cs, cs], axis=1) * st[h] + upd[h]
        return carry

    lax.fori_loop(0, nchunks, chunk, 0, unroll=4)


def _mlstm(proj, gates, gbias, wq, wkt, hn_g, skip):
    bsz, seq, _ = proj.shape
    kern = functools.partial(_mlstm_kernel, seq=seq)
    blk = lambda j: pl.BlockSpec((1, seq, M_WP), lambda b: (b, 0, j))
    full = lambda shape: pl.BlockSpec(shape, lambda b: (0,) * len(shape))
    rows = pltpu.VMEM((8, seq), jnp.float32)
    return pl.pallas_call(
        kern,
        grid=(bsz,),
        in_specs=[blk(OFF_MU // M_WP), blk(OFF_MV // M_WP), blk(OFF_MZ // M_WP),
                  pl.BlockSpec((1, 8, seq), lambda b: (b, 0, 0)),
                  full((8, M_CHUNK)), full((M_HEADS, LANES, LANES)), full((M_HEADS, LANES, LANES)),
                  full((1, M_WP)), full((1, M_WP))],
        out_specs=pl.BlockSpec((1, seq, M_WP), lambda b: (b, 0, 0)),
        out_shape=jax.ShapeDtypeStruct((bsz, seq, M_WP), jnp.bfloat16),
        scratch_shapes=[pltpu.VMEM((M_HEADS, LANES, 2 * LANES), jnp.float32), rows, rows, rows, rows],
        compiler_params=_params(1),
        name="mlstm",
    )(proj, proj, proj, gates, gbias, wq, wkt, hn_g, skip)


def _swa_kernel(q_ref, z_ref, k_ref, v_ref, slope_ref, sink_ref, o_ref, *, seq):
    f32 = jnp.float32
    bf16 = jnp.bfloat16
    W = WINDOW
    ti = lax.broadcasted_iota(jnp.int32, (W, W), 0)
    si = lax.broadcasted_iota(jnp.int32, (W, W), 1)
    cur = si <= ti
    neg_rel = -jnp.where(cur, ti - si, W + ti - si).astype(f32)
    lane = lax.broadcasted_iota(jnp.int32, (W, LANES), 1)
    lane2 = lax.broadcasted_iota(jnp.int32, (2 * W, LANES), 1)
    halves = (lane < A_DH, lane >= A_DH)
    pairs = [(g, kv) for g in range(A_G) for kv in range(A_KV)]

    def block(n, carry):
        rows = pl.ds(pl.multiple_of(n * W, W), W)
        prev = pl.ds(pl.multiple_of(jnp.maximum(n - 1, 0) * W, W), W)
        bias = jnp.where((cur.astype(jnp.int32) + n) > 0, neg_rel, -jnp.inf)
        kk = jnp.concatenate([k_ref[0, rows, :], k_ref[0, prev, :]], axis=0)
        vv = jnp.concatenate([v_ref[0, rows, :], v_ref[0, prev, :]], axis=0)
        vms = [jnp.where(lane2 < A_DH, vv, jnp.zeros_like(vv)), jnp.where(lane2 >= A_DH, vv, jnp.zeros_like(vv))]
        q_all = q_ref[0, rows, :]
        q_all = q_all * jnp.asarray(A_DH ** -0.5, q_all.dtype)
        sc = []
        for g, kv in pairs:
            qg = q_all[:, g * LANES:(g + 1) * LANES]
            qm = jnp.where(halves[kv], qg, jnp.zeros_like(qg))
            sc.append(lax.dot_general(qm, kk, _NT, preferred_element_type=f32))
        pc = []
        for i, (g, kv) in enumerate(pairs):
            head = kv * A_G + g
            s2 = jnp.where(cur, sc[i][:, :W], sc[i][:, W:]) + slope_ref[head] * bias
            sink = sink_ref[head]
            mx = jnp.maximum(jnp.max(s2, axis=1, keepdims=True), sink)
            e = jnp.exp(s2 - mx)
            den = jnp.sum(e, axis=1, keepdims=True) + jnp.exp(sink - mx)
            probs = (e * (1.0 / den)).astype(bf16)
            zero = jnp.zeros_like(probs)
            pc.append(jnp.concatenate([jnp.where(cur, probs, zero), jnp.where(cur, zero, probs)], axis=1))
        for g in range(A_G):
            cols = slice(g * LANES, (g + 1) * LANES)
            o = (jnp.dot(pc[2 * g], vms[0], preferred_element_type=f32)
                 + jnp.dot(pc[2 * g + 1], vms[1], preferred_element_type=f32))
            o_ref[0, rows, cols] = (o * _silu(z_ref[0, rows, cols].astype(f32))).astype(o_ref.dtype)
        return carry

    lax.fori_loop(0, seq // W, block, 0, unroll=8)


def _swa(proj, slopes, sinks):
    bsz, seq, _ = proj.shape
    kern = functools.partial(_swa_kernel, seq=seq)
    smem = pl.BlockSpec(memory_space=pltpu.SMEM)
    full = lambda off, w: pl.BlockSpec((1, seq, w), lambda b: (b, 0, off // w))
    return pl.pallas_call(
        kern,
        grid=(bsz,),
        in_specs=[full(OFF_AQ, A_W), full(OFF_AZ, A_W), full(OFF_AK, A_KVW), full(OFF_AV, A_KVW), smem, smem],
        out_specs=pl.BlockSpec((1, seq, A_W), lambda b: (b, 0, 0)),
        out_shape=jax.ShapeDtypeStruct((bsz, seq, A_W), jnp.bfloat16),
        compiler_params=_params(1),
        name="swa",
    )(proj, proj, proj, proj, slopes, sinks)


def _sb_kernel(q_ref, z_ref, k_ref, v_ref, o_ref, acc_ref, run_ref, zz_ref, a_ref, *, seq):
    f32 = jnp.float32
    bf16 = jnp.bfloat16
    T = SB_TILE
    lane = lax.broadcasted_iota(jnp.int32, (T, LANES), 1)
    halves = (lane < S_DH, lane >= S_DH)
    ti = lax.broadcasted_iota(jnp.int32, (T, T), 0)
    si = lax.broadcasted_iota(jnp.int32, (T, T), 1)
    strict = si < ti
    usum = jnp.where(ti > si, 1.0, 0.0).astype(bf16)
    heads = range(S_HEADS)
    cols = [slice((h // 2) * LANES, (h // 2 + 1) * LANES) for h in heads]

    def qblock(n, carry):
        r0 = pl.multiple_of(n * T, T)
        q = q_ref[0, pl.ds(r0, T), :]
        q = q * jnp.asarray(S_DH ** -0.5, q.dtype)
        qm = [jnp.where(halves[h % 2], q[:, cols[h]], jnp.zeros((T, LANES), q.dtype)) for h in heads]
        acc_ref[...] = jnp.zeros(acc_ref.shape, f32)
        run_ref[...] = jnp.zeros(run_ref.shape, f32)

        def score(j):
            k0 = pl.multiple_of(j * T, T)
            kb = k_ref[0, pl.ds(k0, T), :]
            for h in heads:
                zz_ref[h] = lax.dot_general(qm[h], kb[:, cols[h]], _NT, preferred_element_type=f32)

        def apply(j):
            k0 = pl.multiple_of(j * T, T)
            vb = v_ref[0, pl.ds(k0, T), :]
            for p in range(S_PAIRS):
                pv = None
                for h in (2 * p, 2 * p + 1):
                    vm = jnp.where(halves[h % 2], vb[:, cols[h]], jnp.zeros((T, LANES), vb.dtype))
                    d = jnp.dot(a_ref[h], vm, preferred_element_type=f32)
                    pv = d if pv is None else pv + d
                acc_ref[:, cols[2 * p]] += pv

        def weights(j_next, diag):
            ls, lk = [], []
            for h in heads:
                zz = zz_ref[h]
                soft = jnp.log(1.0 + jnp.exp2(jnp.abs(zz) * (-LOG2E)))
                ls.append(jnp.minimum(zz, 0.0) - soft)
                lkh = ls[h] - zz
                lk.append(jnp.where(strict, lkh, 0.0) if diag else lkh)
            suf = [jnp.dot(lk[h].astype(bf16), usum, preferred_element_type=f32) for h in heads]
            runs = [run_ref[h] for h in heads]
            new_runs = [runs[h] + jnp.sum(lk[h], axis=1, keepdims=True) for h in heads]
            live = jnp.max(functools.reduce(jnp.maximum, new_runs)) > SB_DEAD
            score(j_next)
            for h in heads:
                ah = jnp.exp2((ls[h] + suf[h] + jnp.concatenate([runs[h]] * (T // LANES), axis=1)) * LOG2E)
                a_ref[h] = (jnp.where(strict, ah, 0.0) if diag else ah).astype(bf16)
                run_ref[h] = new_runs[h]
            return live.astype(jnp.int32)

        score(n)
        live0 = weights(max(n - 1, 0), True)

        def cond(c):
            i, live = c
            return jnp.logical_and(i <= n, live > 0)

        def body(c):
            i, _ = c
            apply(n - i + 1)
            return i + 1, weights(jnp.maximum(n - i - 1, 0), False)

        i_end, _ = lax.while_loop(cond, body, (jnp.int32(1), live0))
        apply(n - i_end + 1)
        zg = z_ref[0, pl.ds(r0, T), :].astype(f32)
        o_ref[0, pl.ds(r0, T), :] = (acc_ref[...] * _silu(zg)).astype(o_ref.dtype)
        return carry

    for n in range(seq // T):
        qblock(n, 0)


def _stickbreak(proj):
    bsz, seq, _ = proj.shape
    kern = functools.partial(_sb_kernel, seq=seq)
    full = lambda off: pl.BlockSpec((1, seq, S_W), lambda b: (b, 0, off // S_W))
    return pl.pallas_call(
        kern,
        grid=(bsz,),
        in_specs=[full(OFF_SQ), full(OFF_SZ), full(OFF_SK), full(OFF_SV)],
        out_specs=pl.BlockSpec((1, seq, S_W), lambda b: (b, 0, 0)),
        out_shape=jax.ShapeDtypeStruct((bsz, seq, S_W), jnp.bfloat16),
        scratch_shapes=[pltpu.VMEM((SB_TILE, S_W), jnp.float32),
                        pltpu.VMEM((S_HEADS, SB_TILE, LANES), jnp.float32),
                        pltpu.VMEM((S_HEADS, SB_TILE, SB_TILE), jnp.float32),
                        pltpu.VMEM((S_HEADS, SB_TILE, SB_TILE), jnp.bfloat16)],
        compiler_params=_params(1),
        name="stickbreak",
    )(proj, proj, proj, proj)


def _outproj_kernel(x_ref, ym_ref, ya_ref, ys_ref, wm_ref, wa_ref, ws_ref, g_ref, gate_ref, o_ref, *, tm, sub):
    f32 = jnp.float32
    gain = gate_ref[...] * g_ref[...]
    for i in range(tm // sub):
        r = slice(i * sub, (i + 1) * sub)
        y = (jnp.dot(ym_ref[0, r, :], wm_ref[...], preferred_element_type=f32)
             + jnp.dot(ya_ref[0, r, :], wa_ref[...], preferred_element_type=f32)
             + jnp.dot(ys_ref[0, r, :], ws_ref[...], preferred_element_type=f32))
        o_ref[0, r, :] = x_ref[0, r, :] + y * lax.rsqrt(jnp.mean(y * y, axis=-1, keepdims=True) + EPS) * gain


def _outproj(x, ym, ya, ys, wm, wa, ws, g_post, mod4, layer):
    bsz, seq, d = x.shape
    tm = 2048
    kern = functools.partial(_outproj_kernel, tm=tm, sub=256)
    row = lambda w: pl.BlockSpec((1, tm, w), lambda b, s: (b, s, 0))
    const = lambda shape: pl.BlockSpec(shape, lambda b, s: (0,) * len(shape), pipeline_mode=pl.Buffered(1))
    return pl.pallas_call(
        kern,
        grid=(bsz, seq // tm),
        in_specs=[row(d), row(M_WP), row(A_W), row(S_W),
                  const((M_WP, d)), const((A_W, d)), const((S_W, d)), const((1, d)),
                  pl.BlockSpec((None, None, 1, d), lambda b, s: (layer, b, 0, 2))],
        out_specs=row(d),
        out_shape=jax.ShapeDtypeStruct((bsz, seq, d), jnp.float32),
        compiler_params=_params(2),
        name="outproj",
    )(x, ym, ya, ys, wm, wa, ws, g_post, mod4)


def _pad_heads(w, axis):
    shape = list(w.shape)
    shape[axis:axis + 1] = [M_HEADS, M_DH]
    w = w.reshape(shape)
    pad = [(0, 0)] * w.ndim
    pad[axis + 1] = (0, LANES - M_DH)
    w = jnp.pad(w, pad)
    shape[axis:axis + 2] = [M_WP]
    return w.reshape(shape)


def _pair_heads(w, axis):
    shape = list(w.shape)
    shape[axis:axis + 1] = [A_KV, A_G, A_DH]
    w = jnp.swapaxes(w.reshape(shape), axis, axis + 1)
    shape[axis:axis + 3] = [A_W]
    return w.reshape(shape)


def _pack_w_in(w):
    o = np.cumsum([0, M_W, M_W, M_HEADS, M_HEADS, M_W, A_W, A_KVW, A_KVW, A_W, S_W, S_W, S_W, S_W])
    w = w.astype(jnp.bfloat16)
    seg = lambda i: w[:, int(o[i]):int(o[i + 1])]
    packed = jnp.concatenate(
        [_pad_heads(seg(0), 1), _pad_heads(seg(1), 1), _pad_heads(seg(4), 1),
         _pair_heads(seg(5), 1), _pair_heads(seg(8), 1), seg(6), seg(7),
         seg(9), seg(10), seg(11), seg(12)], axis=1)
    gates_t = jnp.concatenate([seg(2), seg(3)], axis=1).T
    return packed, gates_t


def _pack_w_out(w):
    w = w.astype(jnp.bfloat16)
    return _pad_heads(w[:M_W], 0), _pair_heads(w[M_W:M_W + A_W], 0), w[M_W + A_W:]


def _pad_qk(w):
    return jnp.pad(w, ((0, 0), (0, LANES - M_DH), (0, LANES - M_DH))).astype(jnp.bfloat16)


def kernel(x, c, w_mod, b_mod, g_pre, g_post, w_in, m_conv_w, m_conv_b, m_wq, m_wk, m_b_i, m_b_f,
           m_norm_g, m_skip, a_sinks, w_out):
    bsz = x.shape[0]
    mod4 = _modulation(c, w_mod, b_mod).reshape(DEPTH, bsz, 1, 3 * D_MODEL)
    slopes = jnp.asarray(2.0 ** (-8.0 * np.arange(1, A_HEADS + 1) / A_HEADS), dtype=jnp.float32)
    for l in range(DEPTH):
        w_pack, wg_t = _pack_w_in(w_in[l])
        wm, wa, ws = _pack_w_out(w_out[l])
        gbias = jnp.broadcast_to(jnp.concatenate([m_b_i[l], m_b_f[l]])[:, None], (2 * M_HEADS, M_CHUNK))
        proj, gates = _inproj(x, mod4, l, g_pre[l][None], w_pack, wg_t,
                              _pad_heads(m_conv_w[l], 1), _pad_heads(m_conv_b[l][None], 1))
        ym = _mlstm(proj, gates, gbias, _pad_qk(m_wq[l]), _pad_qk(jnp.swapaxes(m_wk[l], 1, 2)),
                    _pad_heads(m_norm_g[l][None], 1), _pad_heads(m_skip[l][None], 1))
        ya = _swa(proj, slopes, a_sinks[l])
        ys = _stickbreak(proj)
        x = _outproj(x, ym, ya, ys, wm, wa, ws, g_post[l][None], mod4, l)
    return x
```

```python
import functools

import jax
import jax.numpy as jnp
import numpy as np
from jax import lax
from jax.experimental import pallas as pl
from jax.experimental.pallas import tpu as pltpu

D_MODEL = 1024
DEPTH = 2
M_HEADS = 4
M_DH = 96
M_W = M_HEADS * M_DH
M_CONV = 4
A_HEADS = 6
A_KV = 2
A_G = A_HEADS // A_KV
A_DH = 64
A_W = A_HEADS * A_DH
A_KVW = A_KV * A_DH
WINDOW = 128
S_HEADS = 4
S_DH = 64
S_W = S_HEADS * S_DH
S_BLOCK = 128
EPS = 1e-6

LANES = 128
M_CHUNK = 128
M_WP = M_HEADS * LANES
SB_TILE = 256
S_PAIRS = S_W // LANES
SB_DEAD = -93.0

OFF_MU, OFF_MV, OFF_MZ = 0, M_WP, 2 * M_WP
OFF_AQ = 3 * M_WP
OFF_AZ = OFF_AQ + A_W
OFF_AK = OFF_AZ + A_W
OFF_AV = OFF_AK + A_KVW
OFF_SQ = OFF_AV + A_KVW
OFF_SK = OFF_SQ + S_W
OFF_SV = OFF_SK + S_W
OFF_SZ = OFF_SV + S_W
N_PACK = OFF_SZ + S_W

VMEM_LIMIT = 48 * 1024 * 1024

_NT = (((1,), (1,)), ((), ()))
LOG2E = 1.4426950408889634


def _log_sigmoid(x):
    return jnp.minimum(x, 0.0) - jnp.log(1.0 + jnp.exp(-jnp.abs(x)))


def _silu(x):
    u = 0.5 * x
    return u + u * jnp.tanh(u)


def _params(n_axes):
    return pltpu.CompilerParams(dimension_semantics=("arbitrary",) * n_axes, vmem_limit_bytes=VMEM_LIMIT)


def _mod_kernel(c_ref, w_ref, b_ref, o_ref):
    c_act = _silu(c_ref[...]).astype(jnp.bfloat16)
    o_ref[0] = jnp.dot(c_act, w_ref[0].astype(jnp.bfloat16), preferred_element_type=jnp.float32) + b_ref[0]


def _modulation(c, w_mod, b_mod):
    depth, d, n = w_mod.shape
    bsz = c.shape[0]
    tn = 1024
    return pl.pallas_call(
        _mod_kernel,
        grid=(depth, n // tn),
        in_specs=[pl.BlockSpec((bsz, d), lambda l, j: (0, 0)),
                  pl.BlockSpec((1, d, tn), lambda l, j: (l, 0, j)),
                  pl.BlockSpec((1, 1, tn), lambda l, j: (l, 0, j))],
        out_specs=pl.BlockSpec((1, bsz, tn), lambda l, j: (l, 0, j)),
        out_shape=jax.ShapeDtypeStruct((depth, bsz, n), jnp.float32),
        compiler_params=_params(2),
        name="modulation",
    )(c, w_mod, b_mod.reshape(depth, 1, n))


def _inproj_kernel(x_ref, shift_ref, scale_ref, g_ref, wt_ref, wg_ref, cw_ref, cb_ref,
                   proj_ref, gates_ref, conv_ref, w_ref, *, tm, sub, nchunk):
    s = pl.program_id(1)
    f32 = jnp.float32

    @pl.when((pl.program_id(0) == 0) & (s == 0))
    def _():
        for c0 in range(0, N_PACK, 2 * LANES):
            w_ref[:, c0:c0 + 2 * LANES] = wt_ref[c0:c0 + 2 * LANES, :].T

    @pl.when(s == 0)
    def _():
        conv_ref[0:8, :] = jnp.zeros((8, M_WP), f32)

    subs = [slice(i * sub, (i + 1) * sub) for i in range(tm // sub)]
    gain = g_ref[...] * (1.0 + scale_ref[...])
    hs = []
    for r in subs:
        x = x_ref[0, r, :]
        ms = jnp.mean(x * x, axis=-1, keepdims=True)
        hs.append((x * lax.rsqrt(ms + EPS) * gain + shift_ref[...]).astype(jnp.bfloat16))

    for i, (r, h) in enumerate(zip(subs, hs)):
        gates_ref[0, :, r] = lax.dot_general(wg_ref[...], h, _NT, preferred_element_type=f32)

        base = 8 + i * sub
        conv_ref[base:base + sub, :] = jnp.dot(h, w_ref[:, OFF_MU:OFF_MU + M_WP], preferred_element_type=f32)
        acc = cb_ref[...] + cw_ref[M_CONV - 1:M_CONV, :] * conv_ref[base:base + sub, :]
        for j in range(M_CONV - 1):
            lag = M_CONV - 1 - j
            acc = acc + cw_ref[j:j + 1, :] * conv_ref[base - lag:base - lag + sub, :]
        proj_ref[0, r, OFF_MU:OFF_MU + M_WP] = _silu(acc).astype(jnp.bfloat16)

        for c0 in range(OFF_MV, N_PACK, nchunk):
            proj_ref[0, r, c0:c0 + nchunk] = jnp.dot(
                h, w_ref[:, c0:c0 + nchunk], preferred_element_type=f32).astype(jnp.bfloat16)

    conv_ref[0:8, :] = conv_ref[tm:tm + 8, :]


def _inproj(x, mod4, layer, g_pre, w_pack, wg_t, conv_w, conv_b):
    bsz, seq, d = x.shape
    tm = 1024
    kern = functools.partial(_inproj_kernel, tm=tm, sub=128, nchunk=512)
    const = lambda shape: pl.BlockSpec(shape, lambda b, s: (0,) * len(shape), pipeline_mode=pl.Buffered(1))
    return pl.pallas_call(
        kern,
        grid=(bsz, seq // tm),
        in_specs=[pl.BlockSpec((1, tm, d), lambda b, s: (b, s, 0)),
                  pl.BlockSpec((None, None, 1, d), lambda b, s: (layer, b, 0, 0)),
                  pl.BlockSpec((None, None, 1, d), lambda b, s: (layer, b, 0, 1)),
                  const((1, d)), const((N_PACK, d)), const((8, d)), const((M_CONV, M_WP)), const((1, M_WP))],
        out_specs=[pl.BlockSpec((1, tm, N_PACK), lambda b, s: (b, s, 0)),
                   pl.BlockSpec((1, 8, tm), lambda b, s: (b, 0, s))],
        out_shape=[jax.ShapeDtypeStruct((bsz, seq, N_PACK), jnp.bfloat16),
                   jax.ShapeDtypeStruct((bsz, 8, seq), jnp.float32)],
        scratch_shapes=[pltpu.VMEM((tm + 8, M_WP), jnp.float32), pltpu.VMEM((d, N_PACK), jnp.bfloat16)],
        compiler_params=_params(2),
        name="inproj",
    )(x, mod4, mod4, g_pre, w_pack, wg_t, conv_w, conv_b)


def _mlstm_kernel(cu_ref, v_ref, z_ref, gates_ref, gbias_ref, wq_ref, wkt_ref, hg_ref, skip_ref,
                  o_ref, st_ref, a_ref, b_ref, mp_ref, ml_ref, *, seq):
    L = M_CHUNK
    nchunks = seq // L
    f32 = jnp.float32
    bf16 = jnp.bfloat16
    lane8 = lax.broadcasted_iota(jnp.int32, (8, L), 1)
    row8 = lax.broadcasted_iota(jnp.int32, (8, L), 0)

    amax, gsum = [], []
    for c in range(nchunks):
        g = gates_ref[0, :, c * L:(c + 1) * L] + gbias_ref[...]
        cum = jnp.where(row8 >= M_HEADS, _log_sigmoid(g), 0.0)
        k = 1
        while k < L:
            cum = cum + jnp.where(lane8 >= k, pltpu.roll(cum, k, axis=1), 0.0)
            k *= 2
        b8 = pltpu.roll(cum, M_HEADS, axis=0) * LOG2E
        a8 = g * LOG2E - b8
        a_ref[:, c * L:(c + 1) * L] = a8
        b_ref[:, c * L:(c + 1) * L] = b8
        amax.append(jnp.broadcast_to(jnp.max(a8, axis=1, keepdims=True), (8, L)))
        gsum.append(jnp.broadcast_to(jnp.max(jnp.where(lane8 == L - 1, b8, -jnp.inf), axis=1, keepdims=True), (8, L)))
    m_prev = jnp.zeros((8, L), f32)
    for c in range(nchunks):
        m_last = jnp.maximum(m_prev, amax[c])
        mp_ref[:, c * L:(c + 1) * L] = m_prev
        ml_ref[:, c * L:(c + 1) * L] = m_last
        m_prev = gsum[c] + m_last

    st_ref[...] = jnp.zeros(st_ref.shape, f32)

    t_idx = lax.broadcasted_iota(jnp.int32, (L, L), 0)
    s_idx = lax.broadcasted_iota(jnp.int32, (L, L), 1)
    tri = s_idx <= t_idx
    eye = s_idx == t_idx
    ones = jnp.ones((L, LANES), bf16)
    scale = M_DH ** -0.5
    heads = range(M_HEADS)
    cols = [slice(h * LANES, (h + 1) * LANES) for h in heads]

    def chunk(c, carry):
        r0 = pl.multiple_of(c * L, L)
        rows = pl.ds(r0, L)
        cu = [cu_ref[0, rows, cols[h]] for h in heads]
        q = [(jnp.dot(cu[h], wq_ref[h], preferred_element_type=f32) * scale).astype(bf16) for h in heads]
        kt = [lax.dot_general(wkt_ref[h], cu[h], _NT, preferred_element_type=f32) for h in heads]
        st = [st_ref[h] for h in heads]
        qk = [jnp.dot(q[h], kt[h].astype(bf16), preferred_element_type=f32) for h in heads]
        qs = [jnp.dot(q[h], st[h].astype(bf16), preferred_element_type=f32) for h in heads]
        smat, mmb, bb = [], [], []
        for h in heads:
            a_mat = jnp.broadcast_to(a_ref[h:h + 1, rows], (L, L))
            m1 = jnp.max(jnp.where(tri, a_mat, -jnp.inf), axis=1, keepdims=True)
            mm = jnp.maximum(mp_ref[h:h + 1, rows], m1)
            p = jnp.where(tri, jnp.exp2(a_mat - mm), 0.0)
            smat.append((qk[h] * p).astype(bf16))
            mmb.append(mm)
            b_mat = jnp.broadcast_to(b_ref[h:h + 1, rows], (L, L))
            bb.append(jnp.broadcast_to(jnp.sum(jnp.where(eye, b_mat, 0.0), axis=1, keepdims=True), (L, LANES)))
        v2 = [jnp.concatenate([v_ref[0, rows, cols[h]], ones], axis=1) for h in heads]
        sv = [jnp.dot(smat[h], v2[h], preferred_element_type=f32) for h in heads]
        upd = []
        for h in heads:
            ws = jnp.exp2(a_ref[h:h + 1, rows] - ml_ref[h:h + 1, rows])
            upd.append(jnp.dot((kt[h] * ws).astype(bf16), v2[h], preferred_element_type=f32))
        for h in heads:
            mp_row = mp_ref[h:h + 1, rows]
            w_inter = jnp.exp2(mp_row - mmb[h])
            num = w_inter * qs[h][:, :LANES] + sv[h][:, :LANES]
            den = w_inter * qs[h][:, LANES:] + sv[h][:, LANES:]
            hh = num / jnp.maximum(jnp.abs(den), jnp.exp2(-(bb[h] + mmb[h])))
            hh = hh * lax.rsqrt(jnp.sum(hh * hh, axis=1, keepdims=True) * (1.0 / M_DH) + EPS)
            y = hh * hg_ref[:, cols[h]] + skip_ref[:, cols[h]] * cu[h].astype(f32)
            y = y * _silu(z_ref[0, rows, cols[h]].astype(f32))
            o_ref[0, rows, cols[h]] = y.astype(o_ref.dtype)
            cs = jnp.exp2(mp_row - ml_ref[h:h + 1, rows])
            st_ref[h] = jnp.concatenate([cs, cs], axis=1) * st[h] + upd[h]
        return carry

    lax.fori_loop(0, nchunks, chunk, 0, unroll=4)


def _mlstm(proj, gates, gbias, wq, wkt, hn_g, skip):
    bsz, seq, _ = proj.shape
    kern = functools.partial(_mlstm_kernel, seq=seq)
    blk = lambda j: pl.BlockSpec((1, seq, M_WP), lambda b: (b, 0, j))
    full = lambda shape: pl.BlockSpec(shape, lambda b: (0,) * len(shape))
    rows = pltpu.VMEM((8, seq), jnp.float32)
    return pl.pallas_call(
        kern,
        grid=(bsz,),
        in_specs=[blk(OFF_MU // M_WP), blk(OFF_MV // M_WP), blk(OFF_MZ // M_WP),
                  pl.BlockSpec((1, 8, seq), lambda b: (b, 0, 0)),
                  full((8, M_CHUNK)), full((M_HEADS, LANES, LANES)), full((M_HEADS, LANES, LANES)),
                  full((1, M_WP)), full((1, M_WP))],
        out_specs=pl.BlockSpec((1, seq, M_WP), lambda b: (b, 0, 0)),
        out_shape=jax.ShapeDtypeStruct((bsz, seq, M_WP), jnp.bfloat16),
        scratch_shapes=[pltpu.VMEM((M_HEADS, LANES, 2 * LANES), jnp.float32), rows, rows, rows, rows],
        compiler_params=_params(1),
        name="mlstm",
    )(proj, proj, proj, gates, gbias, wq, wkt, hn_g, skip)


def _swa_kernel(q_ref, z_ref, k_ref, v_ref, slope_ref, sink_ref, o_ref, *, seq):
    f32 = jnp.float32
    bf16 = jnp.bfloat16
    W = WINDOW
    ti = lax.broadcasted_iota(jnp.int32, (W, W), 0)
    si = lax.broadcasted_iota(jnp.int32, (W, W), 1)
    cur = si <= ti
    neg_rel = -jnp.where(cur, ti - si, W + ti - si).astype(f32)
    lane = lax.broadcasted_iota(jnp.int32, (W, LANES), 1)
    lane2 = lax.broadcasted_iota(jnp.int32, (2 * W, LANES), 1)
    halves = (lane < A_DH, lane >= A_DH)
    pairs = [(g, kv) for g in range(A_G) for kv in range(A_KV)]

    def block(n, carry):
        rows = pl.ds(pl.multiple_of(n * W, W), W)
        prev = pl.ds(pl.multiple_of(jnp.maximum(n - 1, 0) * W, W), W)
        bias = jnp.where((cur.astype(jnp.int32) + n) > 0, neg_rel, -jnp.inf)
        kk = jnp.concatenate([k_ref[0, rows, :], k_ref[0, prev, :]], axis=0)
        vv = jnp.concatenate([v_ref[0, rows, :], v_ref[0, prev, :]], axis=0)
        vms = [jnp.where(lane2 < A_DH, vv, jnp.zeros_like(vv)), jnp.where(lane2 >= A_DH, vv, jnp.zeros_like(vv))]
        q_all = q_ref[0, rows, :]
        q_all = q_all * jnp.asarray(A_DH ** -0.5, q_all.dtype)
        sc = []
        for g, kv in pairs:
            qg = q_all[:, g * LANES:(g + 1) * LANES]
            qm = jnp.where(halves[kv], qg, jnp.zeros_like(qg))
            sc.append(lax.dot_general(qm, kk, _NT, preferred_element_type=f32))
        pc = []
        for i, (g, kv) in enumerate(pairs):
            head = kv * A_G + g
            s2 = jnp.where(cur, sc[i][:, :W], sc[i][:, W:]) + slope_ref[head] * bias
            sink = sink_ref[head]
            mx = jnp.maximum(jnp.max(s2, axis=1, keepdims=True), sink)
            e = jnp.exp(s2 - mx)
            den = jnp.sum(e, axis=1, keepdims=True) + jnp.exp(sink - mx)
            probs = (e * (1.0 / den)).astype(bf16)
            zero = jnp.zeros_like(probs)
            pc.append(jnp.concatenate([jnp.where(cur, probs, zero), jnp.where(cur, zero, probs)], axis=1))
        for g in range(A_G):
            cols = slice(g * LANES, (g + 1) * LANES)
            o = (jnp.dot(pc[2 * g], vms[0], preferred_element_type=f32)
                 + jnp.dot(pc[2 * g + 1], vms[1], preferred_element_type=f32))
            o_ref[0, rows, cols] = (o * _silu(z_ref[0, rows, cols].astype(f32))).astype(o_ref.dtype)
        return carry

    lax.fori_loop(0, seq // W, block, 0, unroll=8)


def _swa(proj, slopes, sinks):
    bsz, seq, _ = proj.shape
    kern = functools.partial(_swa_kernel, seq=seq)
    smem = pl.BlockSpec(memory_space=pltpu.SMEM)
    full = lambda off, w: pl.BlockSpec((1, seq, w), lambda b: (b, 0, off // w))
    return pl.pallas_call(
        kern,
        grid=(bsz,),
        in_specs=[full(OFF_AQ, A_W), full(OFF_AZ, A_W), full(OFF_AK, A_KVW), full(OFF_AV, A_KVW), smem, smem],
        out_specs=pl.BlockSpec((1, seq, A_W), lambda b: (b, 0, 0)),
        out_shape=jax.ShapeDtypeStruct((bsz, seq, A_W), jnp.bfloat16),
        compiler_params=_params(1),
        name="swa",
    )(proj, proj, proj, proj, slopes, sinks)


def _sb_kernel(q_ref, z_ref, k_ref, v_ref, o_ref, acc_ref, run_ref, zz_ref, a_ref, *, seq):
    f32 = jnp.float32
    bf16 = jnp.bfloat16
    T = SB_TILE
    lane = lax.broadcasted_iota(jnp.int32, (T, LANES), 1)
    halves = (lane < S_DH, lane >= S_DH)
    ti = lax.broadcasted_iota(jnp.int32, (T, T), 0)
    si = lax.broadcasted_iota(jnp.int32, (T, T), 1)
    strict = si < ti
    usum = jnp.where(ti > si, 1.0, 0.0).astype(bf16)
    heads = range(S_HEADS)
    cols = [slice((h // 2) * LANES, (h // 2 + 1) * LANES) for h in heads]

    def qblock(n, carry):
        r0 = pl.multiple_of(n * T, T)
        q = q_ref[0, pl.ds(r0, T), :]
        q = q * jnp.asarray(S_DH ** -0.5, q.dtype)
        qm = [jnp.where(halves[h % 2], q[:, cols[h]], jnp.zeros((T, LANES), q.dtype)) for h in heads]
        acc_ref[...] = jnp.zeros(acc_ref.shape, f32)
        run_ref[...] = jnp.zeros(run_ref.shape, f32)

        def score(j):
            k0 = pl.multiple_of(j * T, T)
            kb = k_ref[0, pl.ds(k0, T), :]
            for h in heads:
                zz_ref[h] = lax.dot_general(qm[h], kb[:, cols[h]], _NT, preferred_element_type=f32)

        def apply(j):
            k0 = pl.multiple_of(j * T, T)
            vb = v_ref[0, pl.ds(k0, T), :]
            for p in range(S_PAIRS):
                pv = None
                for h in (2 * p, 2 * p + 1):
                    vm = jnp.where(halves[h % 2], vb[:, cols[h]], jnp.zeros((T, LANES), vb.dtype))
                    d = jnp.dot(a_ref[h], vm, preferred_element_type=f32)
                    pv = d if pv is None else pv + d
                acc_ref[:, cols[2 * p]] += pv

        def weights(j_next, diag):
            ls, lk = [], []
            for h in heads:
                zz = zz_ref[h]
                soft = jnp.log(1.0 + jnp.exp2(jnp.abs(zz) * (-LOG2E)))
                ls.append(jnp.minimum(zz, 0.0) - soft)
                lkh = ls[h] - zz
                lk.append(jnp.where(strict, lkh, 0.0) if diag else lkh)
            suf = [jnp.dot(lk[h].astype(bf16), usum, preferred_element_type=f32) for h in heads]
            runs = [run_ref[h] for h in heads]
            new_runs = [runs[h] + jnp.sum(lk[h], axis=1, keepdims=True) for h in heads]
            live = jnp.max(functools.reduce(jnp.maximum, new_runs)) > SB_DEAD
            score(j_next)
            for h in heads:
                ah = jnp.exp2((ls[h] + suf[h] + jnp.concatenate([runs[h]] * (T // LANES), axis=1)) * LOG2E)
                a_ref[h] = (jnp.where(strict, ah, 0.0) if diag else ah).astype(bf16)
                run_ref[h] = new_runs[h]
            return live.astype(jnp.int32)

        score(n)
        live0 = weights(max(n - 1, 0), True)

        def cond(c):
            i, live = c
            return jnp.logical_and(i <= n, live > 0)

        def body(c):
            i, _ = c
            apply(n - i + 1)
            return i + 1, weights(jnp.maximum(n - i - 1, 0), False)

        i_end, _ = lax.while_loop(cond, body, (jnp.int32(1), live0))
        apply(n - i_end + 1)
        zg = z_ref[0, pl.ds(r0, T), :].astype(f32)
        o_ref[0, pl.ds(r0, T), :] = (acc_ref[...] * _silu(zg)).astype(o_ref.dtype)
        return carry

    for n in range(seq // T):
        qblock(n, 0)


def _stickbreak(proj):
    bsz, seq, _ = proj.shape
    kern = functools.partial(_sb_kernel, seq=seq)
    full = lambda off: pl.BlockSpec((1, seq, S_W), lambda b: (b, 0, off // S_W))
    return pl.pallas_call(
        kern,
        grid=(bsz,),
        in_specs=[full(OFF_SQ), full(OFF_SZ), full(OFF_SK), full(OFF_SV)],
        out_specs=pl.BlockSpec((1, seq, S_W), lambda b: (b, 0, 0)),
        out_shape=jax.ShapeDtypeStruct((bsz, seq, S_W), jnp.bfloat16),
        scratch_shapes=[pltpu.VMEM((SB_TILE, S_W), jnp.float32),
                        pltpu.VMEM((S_HEADS, SB_TILE, LANES), jnp.float32),
                        pltpu.VMEM((S_HEADS, SB_TILE, SB_TILE), jnp.float32),
                        pltpu.VMEM((S_HEADS, SB_TILE, SB_TILE), jnp.bfloat16)],
        compiler_params=_params(1),
        name="stickbreak",
    )(proj, proj, proj, proj)


def _outproj_kernel(x_ref, ym_ref, ya_ref, ys_ref, wm_ref, wa_ref, ws_ref, g_ref, gate_ref, o_ref, *, tm, sub):
    f32 = jnp.float32
    gain = gate_ref[...] * g_ref[...]
    for i in range(tm // sub):
        r = slice(i * sub, (i + 1) * sub)
        y = (jnp.dot(ym_ref[0, r, :], wm_ref[...], preferred_element_type=f32)
             + jnp.dot(ya_ref[0, r, :], wa_ref[...], preferred_element_type=f32)
             + jnp.dot(ys_ref[0, r, :], ws_ref[...], preferred_element_type=f32))
        o_ref[0, r, :] = x_ref[0, r, :] + y * lax.rsqrt(jnp.mean(y * y, axis=-1, keepdims=True) + EPS) * gain


def _outproj(x, ym, ya, ys, wm, wa, ws, g_post, mod4, layer):
    bsz, seq, d = x.shape
    tm = 2048
    kern = functools.partial(_outproj_kernel, tm=tm, sub=256)
    row = lambda w: pl.BlockSpec((1, tm, w), lambda b, s: (b, s, 0))
    const = lambda shape: pl.BlockSpec(shape, lambda b, s: (0,) * len(shape), pipeline_mode=pl.Buffered(1))
    return pl.pallas_call(
        kern,
        grid=(bsz, seq // tm),
        in_specs=[row(d), row(M_WP), row(A_W), row(S_W),
                  const((M_WP, d)), const((A_W, d)), const((S_W, d)), const((1, d)),
                  pl.BlockSpec((None, None, 1, d), lambda b, s: (layer, b, 0, 2))],
        out_specs=row(d),
        out_shape=jax.ShapeDtypeStruct((bsz, seq, d), jnp.float32),
        compiler_params=_params(2),
        name="outproj",
    )(x, ym, ya, ys, wm, wa, ws, g_post, mod4)


def _pad_heads(w, axis):
    shape = list(w.shape)
    shape[axis:axis + 1] = [M_HEADS, M_DH]
    w = w.reshape(shape)
    pad = [(0, 0)] * w.ndim
    pad[axis + 1] = (0, LANES - M_DH)
    w = jnp.pad(w, pad)
    shape[axis:axis + 2] = [M_WP]
    return w.reshape(shape)


def _pair_heads(w, axis):
    shape = list(w.shape)
    shape[axis:axis + 1] = [A_KV, A_G, A_DH]
    w = jnp.swapaxes(w.reshape(shape), axis, axis + 1)
    shape[axis:axis + 3] = [A_W]
    return w.reshape(shape)


def _pack_w_in(w):
    o = np.cumsum([0, M_W, M_W, M_HEADS, M_HEADS, M_W, A_W, A_KVW, A_KVW, A_W, S_W, S_W, S_W, S_W])
    wt = w.astype(jnp.bfloat16).T
    seg = lambda i: wt[int(o[i]):int(o[i + 1])]
    packed_t = jnp.concatenate(
        [_pad_heads(seg(0), 0), _pad_heads(seg(1), 0), _pad_heads(seg(4), 0),
         _pair_heads(seg(5), 0), _pair_heads(seg(8), 0), seg(6), seg(7),
         seg(9), seg(10), seg(11), seg(12)], axis=0)
    gates_t = jnp.concatenate([seg(2), seg(3)], axis=0)
    return packed_t, gates_t


def _pack_w_out(w):
    w = w.astype(jnp.bfloat16)
    return _pad_heads(w[:M_W], 0), _pair_heads(w[M_W:M_W + A_W], 0), w[M_W + A_W:]


def _pad_qk(w):
    return jnp.pad(w, ((0, 0), (0, LANES - M_DH), (0, LANES - M_DH))).astype(jnp.bfloat16)


def kernel(x, c, w_mod, b_mod, g_pre, g_post, w_in, m_conv_w, m_conv_b, m_wq, m_wk, m_b_i, m_b_f,
           m_norm_g, m_skip, a_sinks, w_out):
    bsz = x.shape[0]
    mod4 = _modulation(c, w_mod, b_mod).reshape(DEPTH, bsz, 1, 3 * D_MODEL)
    slopes = jnp.asarray(2.0 ** (-8.0 * np.arange(1, A_HEADS + 1) / A_HEADS), dtype=jnp.float32)
    for l in range(DEPTH):
        w_pack, wg_t = _pack_w_in(w_in[l])
        wm, wa, ws = _pack_w_out(w_out[l])
        gbias = jnp.broadcast_to(jnp.concatenate([m_b_i[l], m_b_f[l]])[:, None], (2 * M_HEADS, M_CHUNK))
        proj, gates = _inproj(x, mod4, l, g_pre[l][None], w_pack, wg_t,
                              _pad_heads(m_conv_w[l], 1), _pad_heads(m_conv_b[l][None], 1))
        ym = _mlstm(proj, gates, gbias, _pad_qk(m_wq[l]), _pad_qk(jnp.swapaxes(m_wk[l], 1, 2)),
                    _pad_heads(m_norm_g[l][None], 1), _pad_heads(m_skip[l][None], 1))
        ya = _swa(proj, slopes, a_sinks[l])
        ys = _stickbreak(proj)
        x = _outproj(x, ym, ya, ys, wm, wa, ws, g_post[l][None], mod4, l)
    return x
```

```python
import functools

import jax
import jax.numpy as jnp
import numpy as np
from jax import lax
from jax.experimental import pallas as pl
from jax.experimental.pallas import tpu as pltpu

D_MODEL = 1024
DEPTH = 2
M_HEADS = 4
M_DH = 96
M_W = M_HEADS * M_DH
M_CONV = 4
A_HEADS = 6
A_KV = 2
A_G = A_HEADS // A_KV
A_DH = 64
A_W = A_HEADS * A_DH
A_KVW = A_KV * A_DH
WINDOW = 128
S_HEADS = 4
S_DH = 64
S_W = S_HEADS * S_DH
EPS = 1e-6

LANES = 128
SUBLANES = 8
GATE_ROWS = 2 * M_HEADS
M_CHUNK = 128
M_WP = M_HEADS * LANES
SB_TILE = 256
S_PAIRS = S_W // LANES
SB_DEAD = -93.0

OFF_MU, OFF_MV, OFF_MZ = 0, M_WP, 2 * M_WP
OFF_AQ = 3 * M_WP
OFF_AZ = OFF_AQ + A_W
OFF_AK = OFF_AZ + A_W
OFF_AV = OFF_AK + A_KVW
OFF_SQ = OFF_AV + A_KVW
OFF_SK = OFF_SQ + S_W
OFF_SV = OFF_SK + S_W
OFF_SZ = OFF_SV + S_W
N_PACK = OFF_SZ + S_W

VMEM_LIMIT = 48 * 1024 * 1024

_NT = (((1,), (1,)), ((), ()))
LOG2E = 1.4426950408889634


def _log_sigmoid(x):
    return jnp.minimum(x, 0.0) - jnp.log(1.0 + jnp.exp(-jnp.abs(x)))


def _silu(x):
    u = 0.5 * x
    return u + u * jnp.tanh(u)


def _params(n_axes):
    return pltpu.CompilerParams(dimension_semantics=("arbitrary",) * n_axes, vmem_limit_bytes=VMEM_LIMIT)


def _mod_kernel(c_ref, w_ref, b_ref, o_ref):
    c_act = _silu(c_ref[...]).astype(jnp.bfloat16)
    o_ref[0] = jnp.dot(c_act, w_ref[0].astype(jnp.bfloat16), preferred_element_type=jnp.float32) + b_ref[0]


def _modulation(c, w_mod, b_mod):
    depth, d, n = w_mod.shape
    bsz = c.shape[0]
    tn = 1024
    return pl.pallas_call(
        _mod_kernel,
        grid=(depth, n // tn),
        in_specs=[pl.BlockSpec((bsz, d), lambda l, j: (0, 0)),
                  pl.BlockSpec((1, d, tn), lambda l, j: (l, 0, j)),
                  pl.BlockSpec((1, 1, tn), lambda l, j: (l, 0, j))],
        out_specs=pl.BlockSpec((1, bsz, tn), lambda l, j: (l, 0, j)),
        out_shape=jax.ShapeDtypeStruct((depth, bsz, n), jnp.float32),
        compiler_params=_params(2),
        name="modulation",
    )(c, w_mod, b_mod.reshape(depth, 1, n))


def _inproj_kernel(x_ref, shift_ref, scale_ref, g_ref, wt_ref, wg_ref, cw_ref, cb_ref,
                   proj_ref, gates_ref, conv_ref, w_ref, *, tm, sub, nchunk):
    s = pl.program_id(1)
    f32 = jnp.float32

    @pl.when((pl.program_id(0) == 0) & (s == 0))
    def _():
        for c0 in range(0, N_PACK, 2 * LANES):
            w_ref[:, c0:c0 + 2 * LANES] = wt_ref[c0:c0 + 2 * LANES, :].T

    @pl.when(s == 0)
    def _():
        conv_ref[0:SUBLANES, :] = jnp.zeros((SUBLANES, M_WP), f32)

    subs = [slice(i * sub, (i + 1) * sub) for i in range(tm // sub)]
    gain = g_ref[...] * (1.0 + scale_ref[...])
    hs = []
    for r in subs:
        x = x_ref[0, r, :]
        ms = jnp.mean(x * x, axis=-1, keepdims=True)
        hs.append((x * lax.rsqrt(ms + EPS) * gain + shift_ref[...]).astype(jnp.bfloat16))

    for i, (r, h) in enumerate(zip(subs, hs)):
        gates_ref[0, :, r] = lax.dot_general(wg_ref[...], h, _NT, preferred_element_type=f32)

        base = SUBLANES + i * sub
        conv_ref[base:base + sub, :] = jnp.dot(h, w_ref[:, OFF_MU:OFF_MU + M_WP], preferred_element_type=f32)
        acc = cb_ref[...] + cw_ref[M_CONV - 1:M_CONV, :] * conv_ref[base:base + sub, :]
        for j in range(M_CONV - 1):
            lag = M_CONV - 1 - j
            acc = acc + cw_ref[j:j + 1, :] * conv_ref[base - lag:base - lag + sub, :]
        proj_ref[0, r, OFF_MU:OFF_MU + M_WP] = _silu(acc).astype(jnp.bfloat16)

        for c0 in range(OFF_MV, N_PACK, nchunk):
            proj_ref[0, r, c0:c0 + nchunk] = jnp.dot(
                h, w_ref[:, c0:c0 + nchunk], preferred_element_type=f32).astype(jnp.bfloat16)

    conv_ref[0:SUBLANES, :] = conv_ref[tm:tm + SUBLANES, :]


def _inproj(x, mod4, layer, g_pre, w_pack, wg_t, conv_w, conv_b):
    bsz, seq, d = x.shape
    tm = 1024
    kern = functools.partial(_inproj_kernel, tm=tm, sub=128, nchunk=512)
    const = lambda shape: pl.BlockSpec(shape, lambda b, s: (0,) * len(shape), pipeline_mode=pl.Buffered(1))
    return pl.pallas_call(
        kern,
        grid=(bsz, seq // tm),
        in_specs=[pl.BlockSpec((1, tm, d), lambda b, s: (b, s, 0)),
                  pl.BlockSpec((None, None, 1, d), lambda b, s: (layer, b, 0, 0)),
                  pl.BlockSpec((None, None, 1, d), lambda b, s: (layer, b, 0, 1)),
                  const((1, d)), const((N_PACK, d)), const((GATE_ROWS, d)), const((M_CONV, M_WP)), const((1, M_WP))],
        out_specs=[pl.BlockSpec((1, tm, N_PACK), lambda b, s: (b, s, 0)),
                   pl.BlockSpec((1, GATE_ROWS, tm), lambda b, s: (b, 0, s))],
        out_shape=[jax.ShapeDtypeStruct((bsz, seq, N_PACK), jnp.bfloat16),
                   jax.ShapeDtypeStruct((bsz, GATE_ROWS, seq), jnp.float32)],
        scratch_shapes=[pltpu.VMEM((tm + SUBLANES, M_WP), jnp.float32), pltpu.VMEM((d, N_PACK), jnp.bfloat16)],
        compiler_params=_params(2),
        name="inproj",
    )(x, mod4, mod4, g_pre, w_pack, wg_t, conv_w, conv_b)


def _mlstm_kernel(cu_ref, v_ref, z_ref, gates_ref, gbias_ref, wq_ref, wkt_ref, hg_ref, skip_ref,
                  o_ref, st_ref, a_ref, b_ref, mp_ref, ml_ref, *, seq):
    L = M_CHUNK
    nchunks = seq // L
    f32 = jnp.float32
    bf16 = jnp.bfloat16
    lane8 = lax.broadcasted_iota(jnp.int32, (GATE_ROWS, L), 1)
    row8 = lax.broadcasted_iota(jnp.int32, (GATE_ROWS, L), 0)

    amax, gsum = [], []
    for c in range(nchunks):
        g = gates_ref[0, :, c * L:(c + 1) * L] + gbias_ref[...]
        cum = jnp.where(row8 >= M_HEADS, _log_sigmoid(g), 0.0)
        k = 1
        while k < L:
            cum = cum + jnp.where(lane8 >= k, pltpu.roll(cum, k, axis=1), 0.0)
            k *= 2
        b8 = pltpu.roll(cum, M_HEADS, axis=0) * LOG2E
        a8 = g * LOG2E - b8
        a_ref[:, c * L:(c + 1) * L] = a8
        b_ref[:, c * L:(c + 1) * L] = b8
        amax.append(jnp.broadcast_to(jnp.max(a8, axis=1, keepdims=True), (GATE_ROWS, L)))
        gsum.append(jnp.broadcast_to(jnp.max(jnp.where(lane8 == L - 1, b8, -jnp.inf), axis=1, keepdims=True),
                                     (GATE_ROWS, L)))
    m_prev = jnp.zeros((GATE_ROWS, L), f32)
    for c in range(nchunks):
        m_last = jnp.maximum(m_prev, amax[c])
        mp_ref[:, c * L:(c + 1) * L] = m_prev
        ml_ref[:, c * L:(c + 1) * L] = m_last
        m_prev = gsum[c] + m_last

    st_ref[...] = jnp.zeros(st_ref.shape, f32)

    t_idx = lax.broadcasted_iota(jnp.int32, (L, L), 0)
    s_idx = lax.broadcasted_iota(jnp.int32, (L, L), 1)
    tri = s_idx <= t_idx
    eye = s_idx == t_idx
    ones = jnp.ones((L, LANES), bf16)
    scale = M_DH ** -0.5
    heads = range(M_HEADS)
    cols = [slice(h * LANES, (h + 1) * LANES) for h in heads]

    def chunk(c, carry):
        r0 = pl.multiple_of(c * L, L)
        rows = pl.ds(r0, L)
        cu = [cu_ref[0, rows, cols[h]] for h in heads]
        q = [(jnp.dot(cu[h], wq_ref[h], preferred_element_type=f32) * scale).astype(bf16) for h in heads]
        kt = [lax.dot_general(wkt_ref[h], cu[h], _NT, preferred_element_type=f32) for h in heads]
        st = [st_ref[h] for h in heads]
        qk = [jnp.dot(q[h], kt[h].astype(bf16), preferred_element_type=f32) for h in heads]
        qs = [jnp.dot(q[h], st[h].astype(bf16), preferred_element_type=f32) for h in heads]
        smat, mmb, bb = [], [], []
        for h in heads:
            a_mat = jnp.broadcast_to(a_ref[h:h + 1, rows], (L, L))
            m1 = jnp.max(jnp.where(tri, a_mat, -jnp.inf), axis=1, keepdims=True)
            mm = jnp.maximum(mp_ref[h:h + 1, rows], m1)
            p = jnp.where(tri, jnp.exp2(a_mat - mm), 0.0)
            smat.append((qk[h] * p).astype(bf16))
            mmb.append(mm)
            b_mat = jnp.broadcast_to(b_ref[h:h + 1, rows], (L, L))
            bb.append(jnp.broadcast_to(jnp.sum(jnp.where(eye, b_mat, 0.0), axis=1, keepdims=True), (L, LANES)))
        v2 = [jnp.concatenate([v_ref[0, rows, cols[h]], ones], axis=1) for h in heads]
        sv = [jnp.dot(smat[h], v2[h], preferred_element_type=f32) for h in heads]
        upd = []
        for h in heads:
            ws = jnp.exp2(a_ref[h:h + 1, rows] - ml_ref[h:h + 1, rows])
            upd.append(jnp.dot((kt[h] * ws).astype(bf16), v2[h], preferred_element_type=f32))
        for h in heads:
            mp_row = mp_ref[h:h + 1, rows]
            w_inter = jnp.exp2(mp_row - mmb[h])
            num = sv[h][:, :LANES] + w_inter * qs[h][:, :LANES]
            den = sv[h][:, LANES:] + w_inter * qs[h][:, LANES:]
            hh = num / jnp.maximum(jnp.abs(den), jnp.exp2(-(bb[h] + mmb[h])))
            hh = hh * lax.rsqrt(jnp.sum(hh * hh, axis=1, keepdims=True) * (1.0 / M_DH) + EPS)
            y = hh * hg_ref[:, cols[h]] + skip_ref[:, cols[h]] * cu[h].astype(f32)
            y = y * _silu(z_ref[0, rows, cols[h]].astype(f32))
            o_ref[0, rows, cols[h]] = y.astype(o_ref.dtype)
            cs = jnp.exp2(mp_row - ml_ref[h:h + 1, rows])
            st_ref[h] = upd[h] + jnp.concatenate([cs, cs], axis=1) * st[h]
        return carry

    lax.fori_loop(0, nchunks, chunk, 0, unroll=4)


def _mlstm(proj, gates, gbias, wq, wkt, hn_g, skip):
    bsz, seq, _ = proj.shape
    kern = functools.partial(_mlstm_kernel, seq=seq)
    blk = lambda j: pl.BlockSpec((1, seq, M_WP), lambda b: (b, 0, j))
    full = lambda shape: pl.BlockSpec(shape, lambda b: (0,) * len(shape))
    rows = pltpu.VMEM((GATE_ROWS, seq), jnp.float32)
    return pl.pallas_call(
        kern,
        grid=(bsz,),
        in_specs=[blk(OFF_MU // M_WP), blk(OFF_MV // M_WP), blk(OFF_MZ // M_WP),
                  pl.BlockSpec((1, GATE_ROWS, seq), lambda b: (b, 0, 0)),
                  full((GATE_ROWS, M_CHUNK)), full((M_HEADS, LANES, LANES)), full((M_HEADS, LANES, LANES)),
                  full((1, M_WP)), full((1, M_WP))],
        out_specs=pl.BlockSpec((1, seq, M_WP), lambda b: (b, 0, 0)),
        out_shape=jax.ShapeDtypeStruct((bsz, seq, M_WP), jnp.bfloat16),
        scratch_shapes=[pltpu.VMEM((M_HEADS, LANES, 2 * LANES), jnp.float32), rows, rows, rows, rows],
        compiler_params=_params(1),
        name="mlstm",
    )(proj, proj, proj, gates, gbias, wq, wkt, hn_g, skip)


def _swa_kernel(q_ref, z_ref, k_ref, v_ref, slope_ref, sink_ref, o_ref, *, seq):
    f32 = jnp.float32
    bf16 = jnp.bfloat16
    W = WINDOW
    ti = lax.broadcasted_iota(jnp.int32, (W, W), 0)
    si = lax.broadcasted_iota(jnp.int32, (W, W), 1)
    cur = si <= ti
    neg_rel = -jnp.where(cur, ti - si, W + ti - si).astype(f32)
    lane = lax.broadcasted_iota(jnp.int32, (W, LANES), 1)
    lane2 = lax.broadcasted_iota(jnp.int32, (2 * W, LANES), 1)
    halves = (lane < A_DH, lane >= A_DH)
    pairs = [(g, kv) for g in range(A_G) for kv in range(A_KV)]

    def block(n, carry):
        rows = pl.ds(pl.multiple_of(n * W, W), W)
        prev = pl.ds(pl.multiple_of(jnp.maximum(n - 1, 0) * W, W), W)
        bias = jnp.where((cur.astype(jnp.int32) + n) > 0, neg_rel, -jnp.inf)
        kk = jnp.concatenate([k_ref[0, rows, :], k_ref[0, prev, :]], axis=0)
        vv = jnp.concatenate([v_ref[0, rows, :], v_ref[0, prev, :]], axis=0)
        vms = [jnp.where(lane2 < A_DH, vv, jnp.zeros_like(vv)), jnp.where(lane2 >= A_DH, vv, jnp.zeros_like(vv))]
        q_all = q_ref[0, rows, :]
        q_all = q_all * jnp.asarray(A_DH ** -0.5, q_all.dtype)
        sc = []
        for g, kv in pairs:
            qg = q_all[:, g * LANES:(g + 1) * LANES]
            qm = jnp.where(halves[kv], qg, jnp.zeros_like(qg))
            sc.append(lax.dot_general(qm, kk, _NT, preferred_element_type=f32))
        pc = []
        for i, (g, kv) in enumerate(pairs):
            head = kv * A_G + g
            s2 = jnp.where(cur, sc[i][:, :W], sc[i][:, W:]) + slope_ref[head] * bias
            sink = sink_ref[head]
            mx = jnp.maximum(jnp.max(s2, axis=1, keepdims=True), sink)
            e = jnp.exp(s2 - mx)
            den = jnp.sum(e, axis=1, keepdims=True) + jnp.exp(sink - mx)
            probs = (e * (1.0 / den)).astype(bf16)
            zero = jnp.zeros_like(probs)
            pc.append(jnp.concatenate([jnp.where(cur, probs, zero), jnp.where(cur, zero, probs)], axis=1))
        for g in range(A_G):
            cols = slice(g * LANES, (g + 1) * LANES)
            o = (jnp.dot(pc[2 * g], vms[0], preferred_element_type=f32)
                 + jnp.dot(pc[2 * g + 1], vms[1], preferred_element_type=f32))
            o_ref[0, rows, cols] = (o * _silu(z_ref[0, rows, cols].astype(f32))).astype(o_ref.dtype)
        return carry

    lax.fori_loop(0, seq // W, block, 0, unroll=8)


def _swa(proj, slopes, sinks):
    bsz, seq, _ = proj.shape
    kern = functools.partial(_swa_kernel, seq=seq)
    smem = pl.BlockSpec(memory_space=pltpu.SMEM)
    full = lambda off, w: pl.BlockSpec((1, seq, w), lambda b: (b, 0, off // w))
    return pl.pallas_call(
        kern,
        grid=(bsz,),
        in_specs=[full(OFF_AQ, A_W), full(OFF_AZ, A_W), full(OFF_AK, A_KVW), full(OFF_AV, A_KVW), smem, smem],
        out_specs=pl.BlockSpec((1, seq, A_W), lambda b: (b, 0, 0)),
        out_shape=jax.ShapeDtypeStruct((bsz, seq, A_W), jnp.bfloat16),
        compiler_params=_params(1),
        name="swa",
    )(proj, proj, proj, proj, slopes, sinks)


def _sb_kernel(q_ref, z_ref, k_ref, v_ref, o_ref, acc_ref, run_ref, zz_ref, a_ref, *, seq):
    f32 = jnp.float32
    bf16 = jnp.bfloat16
    T = SB_TILE
    lane = lax.broadcasted_iota(jnp.int32, (T, LANES), 1)
    halves = (lane < S_DH, lane >= S_DH)
    ti = lax.broadcasted_iota(jnp.int32, (T, T), 0)
    si = lax.broadcasted_iota(jnp.int32, (T, T), 1)
    strict = si < ti
    usum = jnp.where(ti > si, 1.0, 0.0).astype(bf16)
    heads = range(S_HEADS)
    cols = [slice((h // 2) * LANES, (h // 2 + 1) * LANES) for h in heads]

    def qblock(n, carry):
        r0 = pl.multiple_of(n * T, T)
        q = q_ref[0, pl.ds(r0, T), :]
        q = q * jnp.asarray(S_DH ** -0.5, q.dtype)
        qm = [jnp.where(halves[h % 2], q[:, cols[h]], jnp.zeros((T, LANES), q.dtype)) for h in heads]
        acc_ref[...] = jnp.zeros(acc_ref.shape, f32)
        run_ref[...] = jnp.zeros(run_ref.shape, f32)

        def score(j):
            k0 = pl.multiple_of(j * T, T)
            kb = k_ref[0, pl.ds(k0, T), :]
            for h in heads:
                zz_ref[h] = lax.dot_general(qm[h], kb[:, cols[h]], _NT, preferred_element_type=f32)

        def apply(j):
            k0 = pl.multiple_of(j * T, T)
            vb = v_ref[0, pl.ds(k0, T), :]
            for p in range(S_PAIRS):
                pv = None
                for h in (2 * p, 2 * p + 1):
                    vm = jnp.where(halves[h % 2], vb[:, cols[h]], jnp.zeros((T, LANES), vb.dtype))
                    d = jnp.dot(a_ref[h], vm, preferred_element_type=f32)
                    pv = d if pv is None else pv + d
                acc_ref[:, cols[2 * p]] += pv

        def weights(j_next, diag):
            ls, lk = [], []
            for h in heads:
                zz = zz_ref[h]
                soft = jnp.log(1.0 + jnp.exp2(jnp.abs(zz) * (-LOG2E)))
                ls.append(jnp.minimum(zz, 0.0) - soft)
                lkh = ls[h] - zz
                lk.append(jnp.where(strict, lkh, 0.0) if diag else lkh)
            suf = [jnp.dot(lk[h].astype(bf16), usum, preferred_element_type=f32) for h in heads]
            runs = [run_ref[h] for h in heads]
            new_runs = [runs[h] + jnp.sum(lk[h], axis=1, keepdims=True) for h in heads]
            live = jnp.max(functools.reduce(jnp.maximum, new_runs)) > SB_DEAD
            score(j_next)
            for h in heads:
                ah = jnp.exp2((suf[h] + (ls[h] + jnp.concatenate([runs[h]] * (T // LANES), axis=1))) * LOG2E)
                a_ref[h] = (jnp.where(strict, ah, 0.0) if diag else ah).astype(bf16)
                run_ref[h] = new_runs[h]
            return live.astype(jnp.int32)

        score(n)
        live0 = weights(max(n - 1, 0), True)

        def cond(c):
            i, live = c
            return jnp.logical_and(i <= n, live > 0)

        def body(c):
            i, _ = c
            apply(n - i + 1)
            return i + 1, weights(jnp.maximum(n - i - 1, 0), False)

        i_end, _ = lax.while_loop(cond, body, (jnp.int32(1), live0))
        apply(n - i_end + 1)
        zg = z_ref[0, pl.ds(r0, T), :].astype(f32)
        o_ref[0, pl.ds(r0, T), :] = (acc_ref[...] * _silu(zg)).astype(o_ref.dtype)
        return carry

    for n in range(seq // T):
        qblock(n, 0)


def _stickbreak(proj):
    bsz, seq, _ = proj.shape
    kern = functools.partial(_sb_kernel, seq=seq)
    full = lambda off: pl.BlockSpec((1, seq, S_W), lambda b: (b, 0, off // S_W))
    return pl.pallas_call(
        kern,
        grid=(bsz,),
        in_specs=[full(OFF_SQ), full(OFF_SZ), full(OFF_SK), full(OFF_SV)],
        out_specs=pl.BlockSpec((1, seq, S_W), lambda b: (b, 0, 0)),
        out_shape=jax.ShapeDtypeStruct((bsz, seq, S_W), jnp.bfloat16),
        scratch_shapes=[pltpu.VMEM((SB_TILE, S_W), jnp.float32),
                        pltpu.VMEM((S_HEADS, SB_TILE, LANES), jnp.float32),
                        pltpu.VMEM((S_HEADS, SB_TILE, SB_TILE), jnp.float32),
                        pltpu.VMEM((S_HEADS, SB_TILE, SB_TILE), jnp.bfloat16)],
        compiler_params=_params(1),
        name="stickbreak",
    )(proj, proj, proj, proj)


def _outproj_kernel(x_ref, ym_ref, ya_ref, ys_ref, wm_ref, wa_ref, ws_ref, g_ref, gate_ref, o_ref, *, tm, sub):
    f32 = jnp.float32
    gain = gate_ref[...] * g_ref[...]
    for i in range(tm // sub):
        r = slice(i * sub, (i + 1) * sub)
        y = (jnp.dot(ym_ref[0, r, :], wm_ref[...], preferred_element_type=f32)
             + jnp.dot(ya_ref[0, r, :], wa_ref[...], preferred_element_type=f32)
             + jnp.dot(ys_ref[0, r, :], ws_ref[...], preferred_element_type=f32))
        o_ref[0, r, :] = x_ref[0, r, :] + y * lax.rsqrt(jnp.mean(y * y, axis=-1, keepdims=True) + EPS) * gain


def _outproj(x, ym, ya, ys, wm, wa, ws, g_post, mod4, layer):
    bsz, seq, d = x.shape
    tm = 2048
    kern = functools.partial(_outproj_kernel, tm=tm, sub=256)
    row = lambda w: pl.BlockSpec((1, tm, w), lambda b, s: (b, s, 0))
    const = lambda shape: pl.BlockSpec(shape, lambda b, s: (0,) * len(shape), pipeline_mode=pl.Buffered(1))
    return pl.pallas_call(
        kern,
        grid=(bsz, seq // tm),
        in_specs=[row(d), row(M_WP), row(A_W), row(S_W),
                  const((M_WP, d)), const((A_W, d)), const((S_W, d)), const((1, d)),
                  pl.BlockSpec((None, None, 1, d), lambda b, s: (layer, b, 0, 2))],
        out_specs=row(d),
        out_shape=jax.ShapeDtypeStruct((bsz, seq, d), jnp.float32),
        compiler_params=_params(2),
        name="outproj",
    )(x, ym, ya, ys, wm, wa, ws, g_post, mod4)


def _pad_heads(w, axis):
    shape = list(w.shape)
    shape[axis:axis + 1] = [M_HEADS, M_DH]
    w = w.reshape(shape)
    pad = [(0, 0)] * w.ndim
    pad[axis + 1] = (0, LANES - M_DH)
    w = jnp.pad(w, pad)
    shape[axis:axis + 2] = [M_WP]
    return w.reshape(shape)


def _pair_heads(w, axis):
    shape = list(w.shape)
    shape[axis:axis + 1] = [A_KV, A_G, A_DH]
    w = jnp.swapaxes(w.reshape(shape), axis, axis + 1)
    shape[axis:axis + 3] = [A_W]
    return w.reshape(shape)


def _pack_w_in(w):
    o = np.cumsum([0, M_W, M_W, M_HEADS, M_HEADS, M_W, A_W, A_KVW, A_KVW, A_W, S_W, S_W, S_W, S_W])
    wt = w.astype(jnp.bfloat16).T
    seg = lambda i: wt[int(o[i]):int(o[i + 1])]
    packed_t = jnp.concatenate(
        [_pad_heads(seg(0), 0), _pad_heads(seg(1), 0), _pad_heads(seg(4), 0),
         _pair_heads(seg(5), 0), _pair_heads(seg(8), 0), seg(6), seg(7),
         seg(9), seg(10), seg(11), seg(12)], axis=0)
    gates_t = jnp.concatenate([seg(2), seg(3)], axis=0)
    return packed_t, gates_t


def _pack_w_out(w):
    w = w.astype(jnp.bfloat16)
    return _pad_heads(w[:M_W], 0), _pair_heads(w[M_W:M_W + A_W], 0), w[M_W + A_W:]


def _pad_qk(w):
    return jnp.pad(w, ((0, 0), (0, LANES - M_DH), (0, LANES - M_DH))).astype(jnp.bfloat16)


def kernel(x, c, w_mod, b_mod, g_pre, g_post, w_in, m_conv_w, m_conv_b, m_wq, m_wk, m_b_i, m_b_f,
           m_norm_g, m_skip, a_sinks, w_out):
    bsz = x.shape[0]
    mod4 = _modulation(c, w_mod, b_mod).reshape(DEPTH, bsz, 1, 3 * D_MODEL)
    slopes = jnp.asarray(2.0 ** (-8.0 * np.arange(1, A_HEADS + 1) / A_HEADS), dtype=jnp.float32)
    for l in range(DEPTH):
        w_pack, wg_t = _pack_w_in(w_in[l])
        wm, wa, ws = _pack_w_out(w_out[l])
        gbias = jnp.broadcast_to(jnp.concatenate([m_b_i[l], m_b_f[l]])[:, None], (2 * M_HEADS, M_CHUNK))
        proj, gates = _inproj(x, mod4, l, g_pre[l][None], w_pack, wg_t,
                              _pad_heads(m_conv_w[l], 1), _pad_heads(m_conv_b[l][None], 1))
        ym = _mlstm(proj, gates, gbias, _pad_qk(m_wq[l]), _pad_qk(jnp.swapaxes(m_wk[l], 1, 2)),
                    _pad_heads(m_norm_g[l][None], 1), _pad_heads(m_skip[l][None], 1))
        ya = _swa(proj, slopes, a_sinks[l])
        ys = _stickbreak(proj)
        x = _outproj(x, ym, ya, ys, wm, wa, ws, g_post[l][None], mod4, l)
    return x
```

```python
import functools

import jax
import jax.numpy as jnp
import numpy as np
from jax import lax
from jax.experimental import pallas as pl
from jax.experimental.pallas import tpu as pltpu

D_MODEL = 1024
DEPTH = 2
M_HEADS = 4
M_DH = 96
M_W = M_HEADS * M_DH
M_CONV = 4
A_HEADS = 6
A_KV = 2
A_G = A_HEADS // A_KV
A_DH = 64
A_W = A_HEADS * A_DH
A_KVW = A_KV * A_DH
WINDOW = 128
S_HEADS = 4
S_DH = 64
S_W = S_HEADS * S_DH
EPS = 1e-6

LANES = 128
SUBLANES = 8
GATE_ROWS = 2 * M_HEADS
M_CHUNK = 128
M_WP = M_HEADS * LANES
SB_TILE = 256
S_PAIRS = S_W // LANES
INPROJ_ROWS, INPROJ_SUB, INPROJ_COLS = 1024, 128, 512
OUTPROJ_ROWS, OUTPROJ_SUB = 2048, 256
SB_DEAD = -93.0

OFF_MU, OFF_MV, OFF_MZ = 0, M_WP, 2 * M_WP
OFF_AQ = 3 * M_WP
OFF_AZ = OFF_AQ + A_W
OFF_AK = OFF_AZ + A_W
OFF_AV = OFF_AK + A_KVW
OFF_SQ = OFF_AV + A_KVW
OFF_SK = OFF_SQ + S_W
OFF_SV = OFF_SK + S_W
OFF_SZ = OFF_SV + S_W
N_PACK = OFF_SZ + S_W

VMEM_LIMIT = 48 * 1024 * 1024

_NT = (((1,), (1,)), ((), ()))
LOG2E = 1.4426950408889634


def _log_sigmoid(x):
    return jnp.minimum(x, 0.0) - jnp.log(1.0 + jnp.exp(-jnp.abs(x)))


def _silu(x):
    u = 0.5 * x
    return u + u * jnp.tanh(u)


def _params(n_axes):
    return pltpu.CompilerParams(dimension_semantics=("arbitrary",) * n_axes, vmem_limit_bytes=VMEM_LIMIT)


def _mod_kernel(c_ref, w_ref, b_ref, o_ref):
    c_act = _silu(c_ref[...]).astype(jnp.bfloat16)
    o_ref[0] = jnp.dot(c_act, w_ref[0].astype(jnp.bfloat16), preferred_element_type=jnp.float32) + b_ref[0]


def _modulation(c, w_mod, b_mod):
    depth, d, n = w_mod.shape
    bsz = c.shape[0]
    tn = 1024
    return pl.pallas_call(
        _mod_kernel,
        grid=(depth, n // tn),
        in_specs=[pl.BlockSpec((bsz, d), lambda l, j: (0, 0)),
                  pl.BlockSpec((1, d, tn), lambda l, j: (l, 0, j)),
                  pl.BlockSpec((1, 1, tn), lambda l, j: (l, 0, j))],
        out_specs=pl.BlockSpec((1, bsz, tn), lambda l, j: (l, 0, j)),
        out_shape=jax.ShapeDtypeStruct((depth, bsz, n), jnp.float32),
        compiler_params=_params(2),
        name="modulation",
    )(c, w_mod, b_mod.reshape(depth, 1, n))


def _inproj_kernel(x_ref, shift_ref, scale_ref, g_ref, wt_ref, wg_ref, cw_ref, cb_ref,
                   proj_ref, gates_ref, conv_ref, w_ref, *, tm, sub, nchunk):
    s = pl.program_id(1)
    f32 = jnp.float32

    @pl.when((pl.program_id(0) == 0) & (s == 0))
    def _():
        for c0 in range(0, N_PACK, 2 * LANES):
            w_ref[:, c0:c0 + 2 * LANES] = wt_ref[c0:c0 + 2 * LANES, :].T

    @pl.when(s == 0)
    def _():
        conv_ref[0:SUBLANES, :] = jnp.zeros((SUBLANES, M_WP), f32)

    subs = [slice(i * sub, (i + 1) * sub) for i in range(tm // sub)]
    gain = g_ref[...] * (1.0 + scale_ref[...])
    hs = []
    for r in subs:
        x = x_ref[0, r, :]
        ms = jnp.mean(x * x, axis=-1, keepdims=True)
        hs.append((x * lax.rsqrt(ms + EPS) * gain + shift_ref[...]).astype(jnp.bfloat16))

    for i, (r, h) in enumerate(zip(subs, hs)):
        gates_ref[0, :, r] = lax.dot_general(wg_ref[...], h, _NT, preferred_element_type=f32)

        base = SUBLANES + i * sub
        conv_ref[base:base + sub, :] = jnp.dot(h, w_ref[:, OFF_MU:OFF_MU + M_WP], preferred_element_type=f32)
        acc = cb_ref[...] + cw_ref[M_CONV - 1:M_CONV, :] * conv_ref[base:base + sub, :]
        for j in range(M_CONV - 1):
            lag = M_CONV - 1 - j
            acc = acc + cw_ref[j:j + 1, :] * conv_ref[base - lag:base - lag + sub, :]
        proj_ref[0, r, OFF_MU:OFF_MU + M_WP] = _silu(acc).astype(jnp.bfloat16)

        for c0 in range(OFF_MV, N_PACK, nchunk):
            proj_ref[0, r, c0:c0 + nchunk] = jnp.dot(
                h, w_ref[:, c0:c0 + nchunk], preferred_element_type=f32).astype(jnp.bfloat16)

    conv_ref[0:SUBLANES, :] = conv_ref[tm:tm + SUBLANES, :]


def _inproj(x, mod4, layer, g_pre, w_pack, wg_t, conv_w, conv_b):
    bsz, seq, d = x.shape
    tm = INPROJ_ROWS
    kern = functools.partial(_inproj_kernel, tm=tm, sub=INPROJ_SUB, nchunk=INPROJ_COLS)
    const = lambda shape: pl.BlockSpec(shape, lambda b, s: (0,) * len(shape), pipeline_mode=pl.Buffered(1))
    return pl.pallas_call(
        kern,
        grid=(bsz, seq // tm),
        in_specs=[pl.BlockSpec((1, tm, d), lambda b, s: (b, s, 0)),
                  pl.BlockSpec((None, None, 1, d), lambda b, s: (layer, b, 0, 0)),
                  pl.BlockSpec((None, None, 1, d), lambda b, s: (layer, b, 0, 1)),
                  const((1, d)), const((N_PACK, d)), const((GATE_ROWS, d)), const((M_CONV, M_WP)), const((1, M_WP))],
        out_specs=[pl.BlockSpec((1, tm, N_PACK), lambda b, s: (b, s, 0)),
                   pl.BlockSpec((1, GATE_ROWS, tm), lambda b, s: (b, 0, s))],
        out_shape=[jax.ShapeDtypeStruct((bsz, seq, N_PACK), jnp.bfloat16),
                   jax.ShapeDtypeStruct((bsz, GATE_ROWS, seq), jnp.float32)],
        scratch_shapes=[pltpu.VMEM((tm + SUBLANES, M_WP), jnp.float32), pltpu.VMEM((d, N_PACK), jnp.bfloat16)],
        compiler_params=_params(2),
        name="inproj",
    )(x, mod4, mod4, g_pre, w_pack, wg_t, conv_w, conv_b)


def _mlstm_kernel(cu_ref, v_ref, z_ref, gates_ref, gbias_ref, wq_ref, wkt_ref, hg_ref, skip_ref,
                  o_ref, st_ref, a_ref, b_ref, mp_ref, ml_ref, *, seq):
    L = M_CHUNK
    nchunks = seq // L
    f32 = jnp.float32
    bf16 = jnp.bfloat16
    lane8 = lax.broadcasted_iota(jnp.int32, (GATE_ROWS, L), 1)
    row8 = lax.broadcasted_iota(jnp.int32, (GATE_ROWS, L), 0)

    amax, gsum = [], []
    for c in range(nchunks):
        g = gates_ref[0, :, c * L:(c + 1) * L] + gbias_ref[...]
        cum = jnp.where(row8 >= M_HEADS, _log_sigmoid(g), 0.0)
        k = 1
        while k < L:
            cum = cum + jnp.where(lane8 >= k, pltpu.roll(cum, k, axis=1), 0.0)
            k *= 2
        b8 = pltpu.roll(cum, M_HEADS, axis=0) * LOG2E
        a8 = g * LOG2E - b8
        a_ref[:, c * L:(c + 1) * L] = a8
        b_ref[:, c * L:(c + 1) * L] = b8
        amax.append(jnp.broadcast_to(jnp.max(a8, axis=1, keepdims=True), (GATE_ROWS, L)))
        gsum.append(jnp.broadcast_to(jnp.max(jnp.where(lane8 == L - 1, b8, -jnp.inf), axis=1, keepdims=True),
                                     (GATE_ROWS, L)))
    m_prev = jnp.zeros((GATE_ROWS, L), f32)
    for c in range(nchunks):
        m_last = jnp.maximum(m_prev, amax[c])
        mp_ref[:, c * L:(c + 1) * L] = m_prev
        ml_ref[:, c * L:(c + 1) * L] = m_last
        m_prev = gsum[c] + m_last

    st_ref[...] = jnp.zeros(st_ref.shape, f32)

    t_idx = lax.broadcasted_iota(jnp.int32, (L, L), 0)
    s_idx = lax.broadcasted_iota(jnp.int32, (L, L), 1)
    tri = s_idx <= t_idx
    eye = s_idx == t_idx
    ones = jnp.ones((L, LANES), bf16)
    scale = M_DH ** -0.5
    heads = range(M_HEADS)
    cols = [slice(h * LANES, (h + 1) * LANES) for h in heads]

    def chunk(c, carry):
        r0 = pl.multiple_of(c * L, L)
        rows = pl.ds(r0, L)
        cu = [cu_ref[0, rows, cols[h]] for h in heads]
        q = [(jnp.dot(cu[h], wq_ref[h], preferred_element_type=f32) * scale).astype(bf16) for h in heads]
        kt = [lax.dot_general(wkt_ref[h], cu[h], _NT, preferred_element_type=f32) for h in heads]
        st = [st_ref[h] for h in heads]
        qk = [jnp.dot(q[h], kt[h].astype(bf16), preferred_element_type=f32) for h in heads]
        qs = [jnp.dot(q[h], st[h].astype(bf16), preferred_element_type=f32) for h in heads]
        smat, mmb, bb = [], [], []
        for h in heads:
            a_mat = jnp.broadcast_to(a_ref[h:h + 1, rows], (L, L))
            m1 = jnp.max(jnp.where(tri, a_mat, -jnp.inf), axis=1, keepdims=True)
            mm = jnp.maximum(mp_ref[h:h + 1, rows], m1)
            p = jnp.where(tri, jnp.exp2(a_mat - mm), 0.0)
            smat.append((qk[h] * p).astype(bf16))
            mmb.append(mm)
            b_mat = jnp.broadcast_to(b_ref[h:h + 1, rows], (L, L))
            bb.append(jnp.broadcast_to(jnp.sum(jnp.where(eye, b_mat, 0.0), axis=1, keepdims=True), (L, LANES)))
        v2 = [jnp.concatenate([v_ref[0, rows, cols[h]], ones], axis=1) for h in heads]
        sv = [jnp.dot(smat[h], v2[h], preferred_element_type=f32) for h in heads]
        upd = []
        for h in heads:
            ws = jnp.exp2(a_ref[h:h + 1, rows] - ml_ref[h:h + 1, rows])
            upd.append(jnp.dot((kt[h] * ws).astype(bf16), v2[h], preferred_element_type=f32))
        for h in heads:
            mp_row = mp_ref[h:h + 1, rows]
            w_inter = jnp.exp2(mp_row - mmb[h])
            num = w_inter * qs[h][:, :LANES] + sv[h][:, :LANES]
            den = w_inter * qs[h][:, LANES:] + sv[h][:, LANES:]
            hh = num / jnp.maximum(jnp.abs(den), jnp.exp2(-(bb[h] + mmb[h])))
            hh = hh * lax.rsqrt(jnp.sum(hh * hh, axis=1, keepdims=True) * (1.0 / M_DH) + EPS)
            y = hh * hg_ref[:, cols[h]] + skip_ref[:, cols[h]] * cu[h].astype(f32)
            y = y * _silu(z_ref[0, rows, cols[h]].astype(f32))
            o_ref[0, rows, cols[h]] = y.astype(o_ref.dtype)
            cs = jnp.exp2(mp_row - ml_ref[h:h + 1, rows])
            st_ref[h] = jnp.concatenate([cs, cs], axis=1) * st[h] + upd[h]
        return carry

    lax.fori_loop(0, nchunks, chunk, 0, unroll=4)


def _mlstm(proj, gates, gbias, wq, wkt, hn_g, skip):
    bsz, seq, _ = proj.shape
    kern = functools.partial(_mlstm_kernel, seq=seq)
    blk = lambda j: pl.BlockSpec((1, seq, M_WP), lambda b: (b, 0, j))
    full = lambda shape: pl.BlockSpec(shape, lambda b: (0,) * len(shape))
    rows = pltpu.VMEM((GATE_ROWS, seq), jnp.float32)
    return pl.pallas_call(
        kern,
        grid=(bsz,),
        in_specs=[blk(OFF_MU // M_WP), blk(OFF_MV // M_WP), blk(OFF_MZ // M_WP),
                  pl.BlockSpec((1, GATE_ROWS, seq), lambda b: (b, 0, 0)),
                  full((GATE_ROWS, M_CHUNK)), full((M_HEADS, LANES, LANES)), full((M_HEADS, LANES, LANES)),
                  full((1, M_WP)), full((1, M_WP))],
        out_specs=pl.BlockSpec((1, seq, M_WP), lambda b: (b, 0, 0)),
        out_shape=jax.ShapeDtypeStruct((bsz, seq, M_WP), jnp.bfloat16),
        scratch_shapes=[pltpu.VMEM((M_HEADS, LANES, 2 * LANES), jnp.float32), rows, rows, rows, rows],
        compiler_params=_params(1),
        name="mlstm",
    )(proj, proj, proj, gates, gbias, wq, wkt, hn_g, skip)


def _swa_kernel(q_ref, z_ref, k_ref, v_ref, slope_ref, sink_ref, o_ref, *, seq):
    f32 = jnp.float32
    bf16 = jnp.bfloat16
    W = WINDOW
    ti = lax.broadcasted_iota(jnp.int32, (W, W), 0)
    si = lax.broadcasted_iota(jnp.int32, (W, W), 1)
    cur = si <= ti
    neg_rel = -jnp.where(cur, ti - si, W + ti - si).astype(f32)
    lane = lax.broadcasted_iota(jnp.int32, (W, LANES), 1)
    lane2 = lax.broadcasted_iota(jnp.int32, (2 * W, LANES), 1)
    halves = (lane < A_DH, lane >= A_DH)
    pairs = [(g, kv) for g in range(A_G) for kv in range(A_KV)]

    def block(n, carry):
        rows = pl.ds(pl.multiple_of(n * W, W), W)
        prev = pl.ds(pl.multiple_of(jnp.maximum(n - 1, 0) * W, W), W)
        bias = jnp.where((cur.astype(jnp.int32) + n) > 0, neg_rel, -jnp.inf)
        kk = jnp.concatenate([k_ref[0, rows, :], k_ref[0, prev, :]], axis=0)
        vv = jnp.concatenate([v_ref[0, rows, :], v_ref[0, prev, :]], axis=0)
        vms = [jnp.where(lane2 < A_DH, vv, jnp.zeros_like(vv)), jnp.where(lane2 >= A_DH, vv, jnp.zeros_like(vv))]
        q_all = q_ref[0, rows, :]
        q_all = q_all * jnp.asarray(A_DH ** -0.5, q_all.dtype)
        sc = []
        for g, kv in pairs:
            qg = q_all[:, g * LANES:(g + 1) * LANES]
            qm = jnp.where(halves[kv], qg, jnp.zeros_like(qg))
            sc.append(lax.dot_general(qm, kk, _NT, preferred_element_type=f32))
        pc = []
        for i, (g, kv) in enumerate(pairs):
            head = kv * A_G + g
            s2 = jnp.where(cur, sc[i][:, :W], sc[i][:, W:]) + slope_ref[head] * bias
            sink = sink_ref[head]
            mx = jnp.maximum(jnp.max(s2, axis=1, keepdims=True), sink)
            e = jnp.exp(s2 - mx)
            den = jnp.sum(e, axis=1, keepdims=True) + jnp.exp(sink - mx)
            probs = (e * (1.0 / den)).astype(bf16)
            zero = jnp.zeros_like(probs)
            pc.append(jnp.concatenate([jnp.where(cur, probs, zero), jnp.where(cur, zero, probs)], axis=1))
        for g in range(A_G):
            cols = slice(g * LANES, (g + 1) * LANES)
            o = (jnp.dot(pc[2 * g], vms[0], preferred_element_type=f32)
                 + jnp.dot(pc[2 * g + 1], vms[1], preferred_element_type=f32))
            o_ref[0, rows, cols] = (o * _silu(z_ref[0, rows, cols].astype(f32))).astype(o_ref.dtype)
        return carry

    lax.fori_loop(0, seq // W, block, 0, unroll=8)


def _swa(proj, slopes, sinks):
    bsz, seq, _ = proj.shape
    kern = functools.partial(_swa_kernel, seq=seq)
    smem = pl.BlockSpec(memory_space=pltpu.SMEM)
    full = lambda off, w: pl.BlockSpec((1, seq, w), lambda b: (b, 0, off // w))
    return pl.pallas_call(
        kern,
        grid=(bsz,),
        in_specs=[full(OFF_AQ, A_W), full(OFF_AZ, A_W), full(OFF_AK, A_KVW), full(OFF_AV, A_KVW), smem, smem],
        out_specs=pl.BlockSpec((1, seq, A_W), lambda b: (b, 0, 0)),
        out_shape=jax.ShapeDtypeStruct((bsz, seq, A_W), jnp.bfloat16),
        compiler_params=_params(1),
        name="swa",
    )(proj, proj, proj, proj, slopes, sinks)


def _sb_kernel(q_ref, z_ref, k_ref, v_ref, o_ref, acc_ref, run_ref, zz_ref, a_ref, *, seq):
    f32 = jnp.float32
    bf16 = jnp.bfloat16
    T = SB_TILE
    lane = lax.broadcasted_iota(jnp.int32, (T, LANES), 1)
    halves = (lane < S_DH, lane >= S_DH)
    ti = lax.broadcasted_iota(jnp.int32, (T, T), 0)
    si = lax.broadcasted_iota(jnp.int32, (T, T), 1)
    strict = si < ti
    usum = jnp.where(ti > si, 1.0, 0.0).astype(bf16)
    heads = range(S_HEADS)
    cols = [slice((h // 2) * LANES, (h // 2 + 1) * LANES) for h in heads]

    def qblock(n, carry):
        r0 = pl.multiple_of(n * T, T)
        q = q_ref[0, pl.ds(r0, T), :]
        q = q * jnp.asarray(S_DH ** -0.5, q.dtype)
        qm = [jnp.where(halves[h % 2], q[:, cols[h]], jnp.zeros((T, LANES), q.dtype)) for h in heads]
        acc_ref[...] = jnp.zeros(acc_ref.shape, f32)
        run_ref[...] = jnp.zeros(run_ref.shape, f32)

        def score(j):
            k0 = pl.multiple_of(j * T, T)
            kb = k_ref[0, pl.ds(k0, T), :]
            for h in heads:
                zz_ref[h] = lax.dot_general(qm[h], kb[:, cols[h]], _NT, preferred_element_type=f32)

        def apply(j):
            k0 = pl.multiple_of(j * T, T)
            vb = v_ref[0, pl.ds(k0, T), :]
            for p in range(S_PAIRS):
                pv = None
                for h in (2 * p, 2 * p + 1):
                    vm = jnp.where(halves[h % 2], vb[:, cols[h]], jnp.zeros((T, LANES), vb.dtype))
                    d = jnp.dot(a_ref[h], vm, preferred_element_type=f32)
                    pv = d if pv is None else pv + d
                acc_ref[:, cols[2 * p]] += pv

        def weights(j_next, diag):
            ls, lk = [], []
            for h in heads:
                zz = zz_ref[h]
                soft = jnp.log(1.0 + jnp.exp2(jnp.abs(zz) * (-LOG2E)))
                ls.append(jnp.minimum(zz, 0.0) - soft)
                lkh = ls[h] - zz
                lk.append(jnp.where(strict, lkh, 0.0) if diag else lkh)
            suf = [jnp.dot(lk[h].astype(bf16), usum, preferred_element_type=f32) for h in heads]
            runs = [run_ref[h] for h in heads]
            new_runs = [runs[h] + jnp.sum(lk[h], axis=1, keepdims=True) for h in heads]
            live = jnp.max(functools.reduce(jnp.maximum, new_runs)) > SB_DEAD
            score(j_next)
            for h in heads:
                ah = jnp.exp2((ls[h] + suf[h] + jnp.concatenate([runs[h]] * (T // LANES), axis=1)) * LOG2E)
                a_ref[h] = (jnp.where(strict, ah, 0.0) if diag else ah).astype(bf16)
                run_ref[h] = new_runs[h]
            return live.astype(jnp.int32)

        score(n)
        live0 = weights(max(n - 1, 0), True)

        def cond(c):
            i, live = c
            return jnp.logical_and(i <= n, live > 0)

        def body(c):
            i, _ = c
            apply(n - i + 1)
            return i + 1, weights(jnp.maximum(n - i - 1, 0), False)

        i_end, _ = lax.while_loop(cond, body, (jnp.int32(1), live0))
        apply(n - i_end + 1)
        zg = z_ref[0, pl.ds(r0, T), :].astype(f32)
        o_ref[0, pl.ds(r0, T), :] = (acc_ref[...] * _silu(zg)).astype(o_ref.dtype)
        return carry

    for n in range(seq // T):
        qblock(n, 0)


def _stickbreak(proj):
    bsz, seq, _ = proj.shape
    kern = functools.partial(_sb_kernel, seq=seq)
    full = lambda off: pl.BlockSpec((1, seq, S_W), lambda b: (b, 0, off // S_W))
    return pl.pallas_call(
        kern,
        grid=(bsz,),
        in_specs=[full(OFF_SQ), full(OFF_SZ), full(OFF_SK), full(OFF_SV)],
        out_specs=pl.BlockSpec((1, seq, S_W), lambda b: (b, 0, 0)),
        out_shape=jax.ShapeDtypeStruct((bsz, seq, S_W), jnp.bfloat16),
        scratch_shapes=[pltpu.VMEM((SB_TILE, S_W), jnp.float32),
                        pltpu.VMEM((S_HEADS, SB_TILE, LANES), jnp.float32),
                        pltpu.VMEM((S_HEADS, SB_TILE, SB_TILE), jnp.float32),
                        pltpu.VMEM((S_HEADS, SB_TILE, SB_TILE), jnp.bfloat16)],
        compiler_params=_params(1),
        name="stickbreak",
    )(proj, proj, proj, proj)


def _outproj_kernel(x_ref, ym_ref, ya_ref, ys_ref, wm_ref, wa_ref, ws_ref, g_ref, gate_ref, o_ref, *, tm, sub):
    f32 = jnp.float32
    gain = gate_ref[...] * g_ref[...]
    for i in range(tm // sub):
        r = slice(i * sub, (i + 1) * sub)
        y = (jnp.dot(ym_ref[0, r, :], wm_ref[...], preferred_element_type=f32)
             + jnp.dot(ya_ref[0, r, :], wa_ref[...], preferred_element_type=f32)
             + jnp.dot(ys_ref[0, r, :], ws_ref[...], preferred_element_type=f32))
        o_ref[0, r, :] = x_ref[0, r, :] + y * lax.rsqrt(jnp.mean(y * y, axis=-1, keepdims=True) + EPS) * gain


def _outproj(x, ym, ya, ys, wm, wa, ws, g_post, mod4, layer):
    bsz, seq, d = x.shape
    tm = OUTPROJ_ROWS
    kern = functools.partial(_outproj_kernel, tm=tm, sub=OUTPROJ_SUB)
    row = lambda w: pl.BlockSpec((1, tm, w), lambda b, s: (b, s, 0))
    const = lambda shape: pl.BlockSpec(shape, lambda b, s: (0,) * len(shape), pipeline_mode=pl.Buffered(1))
    return pl.pallas_call(
        kern,
        grid=(bsz, seq // tm),
        in_specs=[row(d), row(M_WP), row(A_W), row(S_W),
                  const((M_WP, d)), const((A_W, d)), const((S_W, d)), const((1, d)),
                  pl.BlockSpec((None, None, 1, d), lambda b, s: (layer, b, 0, 2))],
        out_specs=row(d),
        out_shape=jax.ShapeDtypeStruct((bsz, seq, d), jnp.float32),
        compiler_params=_params(2),
        name="outproj",
    )(x, ym, ya, ys, wm, wa, ws, g_post, mod4)


def _pad_heads(w, axis):
    shape = list(w.shape)
    shape[axis:axis + 1] = [M_HEADS, M_DH]
    w = w.reshape(shape)
    pad = [(0, 0)] * w.ndim
    pad[axis + 1] = (0, LANES - M_DH)
    w = jnp.pad(w, pad)
    shape[axis:axis + 2] = [M_WP]
    return w.reshape(shape)


def _pair_heads(w, axis):
    shape = list(w.shape)
    shape[axis:axis + 1] = [A_KV, A_G, A_DH]
    w = jnp.swapaxes(w.reshape(shape), axis, axis + 1)
    shape[axis:axis + 3] = [A_W]
    return w.reshape(shape)


def _pack_w_in(w):
    o = np.cumsum([0, M_W, M_W, M_HEADS, M_HEADS, M_W, A_W, A_KVW, A_KVW, A_W, S_W, S_W, S_W, S_W])
    wt = w.astype(jnp.bfloat16).T
    seg = lambda i: wt[int(o[i]):int(o[i + 1])]
    packed_t = jnp.concatenate(
        [_pad_heads(seg(0), 0), _pad_heads(seg(1), 0), _pad_heads(seg(4), 0),
         _pair_heads(seg(5), 0), _pair_heads(seg(8), 0), seg(6), seg(7),
         seg(9), seg(10), seg(11), seg(12)], axis=0)
    gates_t = jnp.concatenate([seg(2), seg(3)], axis=0)
    return packed_t, gates_t


def _pack_w_out(w):
    w = w.astype(jnp.bfloat16)
    return _pad_heads(w[:M_W], 0), _pair_heads(w[M_W:M_W + A_W], 0), w[M_W + A_W:]


def _pad_qk(w):
    return jnp.pad(w, ((0, 0), (0, LANES - M_DH), (0, LANES - M_DH))).astype(jnp.bfloat16)


def kernel(x, c, w_mod, b_mod, g_pre, g_post, w_in, m_conv_w, m_conv_b, m_wq, m_wk, m_b_i, m_b_f,
           m_norm_g, m_skip, a_sinks, w_out):
    bsz = x.shape[0]
    mod4 = _modulation(c, w_mod, b_mod).reshape(DEPTH, bsz, 1, 3 * D_MODEL)
    slopes = jnp.asarray(2.0 ** (-8.0 * np.arange(1, A_HEADS + 1) / A_HEADS), dtype=jnp.float32)
    for l in range(DEPTH):
        w_pack, wg_t = _pack_w_in(w_in[l])
        wm, wa, ws = _pack_w_out(w_out[l])
        gbias = jnp.broadcast_to(jnp.concatenate([m_b_i[l], m_b_f[l]])[:, None], (2 * M_HEADS, M_CHUNK))
        proj, gates = _inproj(x, mod4, l, g_pre[l][None], w_pack, wg_t,
                              _pad_heads(m_conv_w[l], 1), _pad_heads(m_conv_b[l][None], 1))
        ym = _mlstm(proj, gates, gbias, _pad_qk(m_wq[l]), _pad_qk(jnp.swapaxes(m_wk[l], 1, 2)),
                    _pad_heads(m_norm_g[l][None], 1), _pad_heads(m_skip[l][None], 1))
        ya = _swa(proj, slopes, a_sinks[l])
        ys = _stickbreak(proj)
        x = _outproj(x, ym, ya, ys, wm, wa, ws, g_post[l][None], mod4, l)
    return x
```

```python
import functools

import jax
import jax.numpy as jnp
import numpy as np
from jax import lax
from jax.experimental import pallas as pl
from jax.experimental.pallas import tpu as pltpu

D_MODEL = 1024
DEPTH = 2
M_HEADS = 4
M_DH = 96
M_W = M_HEADS * M_DH
M_CONV = 4
A_HEADS = 6
A_KV = 2
A_G = A_HEADS // A_KV
A_DH = 64
A_W = A_HEADS * A_DH
A_KVW = A_KV * A_DH
WINDOW = 128
S_HEADS = 4
S_DH = 64
S_W = S_HEADS * S_DH
EPS = 1e-6

LANES = 128
SUBLANES = 8
GATE_ROWS = 2 * M_HEADS
M_CHUNK = 128
M_WP = M_HEADS * LANES
SB_TILE = 256
S_PAIRS = S_W // LANES
INPROJ_ROWS, INPROJ_SUB, INPROJ_COLS = 1024, 128, 512
OUTPROJ_ROWS, OUTPROJ_SUB = 2048, 256
SB_DEAD = -93.0

OFF_MU, OFF_MV, OFF_MZ = 0, M_WP, 2 * M_WP
OFF_AQ = 3 * M_WP
OFF_AZ = OFF_AQ + A_W
OFF_AK = OFF_AZ + A_W
OFF_AV = OFF_AK + A_KVW
OFF_SQ = OFF_AV + A_KVW
OFF_SK = OFF_SQ + S_W
OFF_SV = OFF_SK + S_W
OFF_SZ = OFF_SV + S_W
N_PACK = OFF_SZ + S_W

VMEM_LIMIT = 48 * 1024 * 1024

_NT = (((1,), (1,)), ((), ()))
LOG2E = 1.4426950408889634


def _log_sigmoid(x):
    return jnp.minimum(x, 0.0) - jnp.log(1.0 + jnp.exp(-jnp.abs(x)))


def _silu(x):
    u = 0.5 * x
    return u + u * jnp.tanh(u)


def _params(n_axes):
    return pltpu.CompilerParams(dimension_semantics=("arbitrary",) * n_axes, vmem_limit_bytes=VMEM_LIMIT)


def _mod_kernel(c_ref, w_ref, b_ref, o_ref):
    c_act = _silu(c_ref[...]).astype(jnp.bfloat16)
    o_ref[0] = jnp.dot(c_act, w_ref[0].astype(jnp.bfloat16), preferred_element_type=jnp.float32) + b_ref[0]


def _modulation(c, w_mod, b_mod):
    depth, d, n = w_mod.shape
    bsz = c.shape[0]
    tn = 1024
    return pl.pallas_call(
        _mod_kernel,
        grid=(depth, n // tn),
        in_specs=[pl.BlockSpec((bsz, d), lambda l, j: (0, 0)),
                  pl.BlockSpec((1, d, tn), lambda l, j: (l, 0, j)),
                  pl.BlockSpec((1, 1, tn), lambda l, j: (l, 0, j))],
        out_specs=pl.BlockSpec((1, bsz, tn), lambda l, j: (l, 0, j)),
        out_shape=jax.ShapeDtypeStruct((depth, bsz, n), jnp.float32),
        compiler_params=_params(2),
        name="modulation",
    )(c, w_mod, b_mod.reshape(depth, 1, n))


def _inproj_kernel(x_ref, shift_ref, scale_ref, g_ref, wt_ref, wg_ref, cw_ref, cb_ref, slope_ref, sink_ref,
                   proj_ref, gates_ref, ya_ref, conv_ref, w_ref, kv_ref, *, tm, sub, nchunk):
    s = pl.program_id(1)
    f32 = jnp.float32

    @pl.when((pl.program_id(0) == 0) & (s == 0))
    def _():
        for c0 in range(0, N_PACK, 2 * LANES):
            w_ref[:, c0:c0 + 2 * LANES] = wt_ref[c0:c0 + 2 * LANES, :].T

    @pl.when(s == 0)
    def _():
        conv_ref[0:SUBLANES, :] = jnp.zeros((SUBLANES, M_WP), f32)
        kv_ref[...] = jnp.zeros(kv_ref.shape, kv_ref.dtype)

    subs = [slice(i * sub, (i + 1) * sub) for i in range(tm // sub)]
    gain = g_ref[...] * (1.0 + scale_ref[...])
    hs = []
    for r in subs:
        x = x_ref[0, r, :]
        ms = jnp.mean(x * x, axis=-1, keepdims=True)
        hs.append((x * lax.rsqrt(ms + EPS) * gain + shift_ref[...]).astype(jnp.bfloat16))

    swa_consts = _swa_consts()
    for i, (r, h) in enumerate(zip(subs, hs)):
        gates_ref[0, :, r] = lax.dot_general(wg_ref[...], h, _NT, preferred_element_type=f32)

        base = SUBLANES + i * sub
        conv_ref[base:base + sub, :] = jnp.dot(h, w_ref[:, OFF_MU:OFF_MU + M_WP], preferred_element_type=f32)
        acc = cb_ref[...] + cw_ref[M_CONV - 1:M_CONV, :] * conv_ref[base:base + sub, :]
        for j in range(M_CONV - 1):
            lag = M_CONV - 1 - j
            acc = acc + cw_ref[j:j + 1, :] * conv_ref[base - lag:base - lag + sub, :]
        proj_ref[0, r, OFF_MU:OFF_MU + M_WP] = _silu(acc).astype(jnp.bfloat16)

        for c0 in range(OFF_MV, N_PACK, nchunk):
            proj_ref[0, r, c0:c0 + nchunk] = jnp.dot(
                h, w_ref[:, c0:c0 + nchunk], preferred_element_type=f32).astype(jnp.bfloat16)

        grab = lambda rows, off, w: proj_ref[0, rows, off:off + w]
        kp = kv_ref[0] if i == 0 else grab(subs[i - 1], OFF_AK, A_KVW)
        vp = kv_ref[1] if i == 0 else grab(subs[i - 1], OFF_AV, A_KVW)
        ya_ref[0, r, :] = _swa_block(swa_consts, s * (tm // sub) + i, grab(r, OFF_AQ, A_W), grab(r, OFF_AZ, A_W),
                                     grab(r, OFF_AK, A_KVW), kp, grab(r, OFF_AV, A_KVW), vp, slope_ref, sink_ref)

    conv_ref[0:SUBLANES, :] = conv_ref[tm:tm + SUBLANES, :]
    kv_ref[0] = proj_ref[0, subs[-1], OFF_AK:OFF_AK + A_KVW]
    kv_ref[1] = proj_ref[0, subs[-1], OFF_AV:OFF_AV + A_KVW]


def _inproj(x, mod4, layer, g_pre, w_pack, wg_t, conv_w, conv_b, slopes, sinks):
    bsz, seq, d = x.shape
    tm = INPROJ_ROWS
    assert INPROJ_SUB == WINDOW
    kern = functools.partial(_inproj_kernel, tm=tm, sub=INPROJ_SUB, nchunk=INPROJ_COLS)
    const = lambda shape: pl.BlockSpec(shape, lambda b, s: (0,) * len(shape), pipeline_mode=pl.Buffered(1))
    smem = pl.BlockSpec(memory_space=pltpu.SMEM)
    return pl.pallas_call(
        kern,
        grid=(bsz, seq // tm),
        in_specs=[pl.BlockSpec((1, tm, d), lambda b, s: (b, s, 0)),
                  pl.BlockSpec((None, None, 1, d), lambda b, s: (layer, b, 0, 0)),
                  pl.BlockSpec((None, None, 1, d), lambda b, s: (layer, b, 0, 1)),
                  const((1, d)), const((N_PACK, d)), const((GATE_ROWS, d)), const((M_CONV, M_WP)), const((1, M_WP)),
                  smem, smem],
        out_specs=[pl.BlockSpec((1, tm, N_PACK), lambda b, s: (b, s, 0)),
                   pl.BlockSpec((1, GATE_ROWS, tm), lambda b, s: (b, 0, s)),
                   pl.BlockSpec((1, tm, A_W), lambda b, s: (b, s, 0))],
        out_shape=[jax.ShapeDtypeStruct((bsz, seq, N_PACK), jnp.bfloat16),
                   jax.ShapeDtypeStruct((bsz, GATE_ROWS, seq), jnp.float32),
                   jax.ShapeDtypeStruct((bsz, seq, A_W), jnp.bfloat16)],
        scratch_shapes=[pltpu.VMEM((tm + SUBLANES, M_WP), jnp.float32), pltpu.VMEM((d, N_PACK), jnp.bfloat16),
                        pltpu.VMEM((2, WINDOW, A_KVW), jnp.bfloat16)],
        compiler_params=_params(2),
        name="inproj",
    )(x, mod4, mod4, g_pre, w_pack, wg_t, conv_w, conv_b, slopes, sinks)


def _mlstm_kernel(cu_ref, v_ref, z_ref, gates_ref, gbias_ref, wq_ref, wkt_ref, hg_ref, skip_ref,
                  o_ref, st_ref, a_ref, b_ref, mp_ref, ml_ref, *, seq):
    L = M_CHUNK
    nchunks = seq // L
    f32 = jnp.float32
    bf16 = jnp.bfloat16
    lane8 = lax.broadcasted_iota(jnp.int32, (GATE_ROWS, L), 1)
    row8 = lax.broadcasted_iota(jnp.int32, (GATE_ROWS, L), 0)

    amax, gsum = [], []
    for c in range(nchunks):
        g = gates_ref[0, :, c * L:(c + 1) * L] + gbias_ref[...]
        cum = jnp.where(row8 >= M_HEADS, _log_sigmoid(g), 0.0)
        k = 1
        while k < L:
            cum = cum + jnp.where(lane8 >= k, pltpu.roll(cum, k, axis=1), 0.0)
            k *= 2
        b8 = pltpu.roll(cum, M_HEADS, axis=0) * LOG2E
        a8 = g * LOG2E - b8
        a_ref[:, c * L:(c + 1) * L] = a8
        b_ref[:, c * L:(c + 1) * L] = b8
        amax.append(jnp.broadcast_to(jnp.max(a8, axis=1, keepdims=True), (GATE_ROWS, L)))
        gsum.append(jnp.broadcast_to(jnp.max(jnp.where(lane8 == L - 1, b8, -jnp.inf), axis=1, keepdims=True),
                                     (GATE_ROWS, L)))
    m_prev = jnp.zeros((GATE_ROWS, L), f32)
    for c in range(nchunks):
        m_last = jnp.maximum(m_prev, amax[c])
        mp_ref[:, c * L:(c + 1) * L] = m_prev
        ml_ref[:, c * L:(c + 1) * L] = m_last
        m_prev = gsum[c] + m_last

    st_ref[...] = jnp.zeros(st_ref.shape, f32)

    t_idx = lax.broadcasted_iota(jnp.int32, (L, L), 0)
    s_idx = lax.broadcasted_iota(jnp.int32, (L, L), 1)
    tri = s_idx <= t_idx
    eye = s_idx == t_idx
    ones = jnp.ones((L, LANES), bf16)
    scale = M_DH ** -0.5
    heads = range(M_HEADS)
    cols = [slice(h * LANES, (h + 1) * LANES) for h in heads]

    def chunk(c, carry):
        r0 = pl.multiple_of(c * L, L)
        rows = pl.ds(r0, L)
        cu = [cu_ref[0, rows, cols[h]] for h in heads]
        q = [(jnp.dot(cu[h], wq_ref[h], preferred_element_type=f32) * scale).astype(bf16) for h in heads]
        kt = [lax.dot_general(wkt_ref[h], cu[h], _NT, preferred_element_type=f32) for h in heads]
        st = [st_ref[h] for h in heads]
        qk = [jnp.dot(q[h], kt[h].astype(bf16), preferred_element_type=f32) for h in heads]
        qs = [jnp.dot(q[h], st[h].astype(bf16), preferred_element_type=f32) for h in heads]
        smat, mmb, bb = [], [], []
        for h in heads:
            a_mat = jnp.broadcast_to(a_ref[h:h + 1, rows], (L, L))
            m1 = jnp.max(jnp.where(tri, a_mat, -jnp.inf), axis=1, keepdims=True)
            mm = jnp.maximum(mp_ref[h:h + 1, rows], m1)
            p = jnp.where(tri, jnp.exp2(a_mat - mm), 0.0)
            smat.append((qk[h] * p).astype(bf16))
            mmb.append(mm)
            b_mat = jnp.broadcast_to(b_ref[h:h + 1, rows], (L, L))
            bb.append(jnp.broadcast_to(jnp.sum(jnp.where(eye, b_mat, 0.0), axis=1, keepdims=True), (L, LANES)))
        v2 = [jnp.concatenate([v_ref[0, rows, cols[h]], ones], axis=1) for h in heads]
        sv = [jnp.dot(smat[h], v2[h], preferred_element_type=f32) for h in heads]
        upd = []
        for h in heads:
            ws = jnp.exp2(a_ref[h:h + 1, rows] - ml_ref[h:h + 1, rows])
            upd.append(jnp.dot((kt[h] * ws).astype(bf16), v2[h], preferred_element_type=f32))
        for h in heads:
            mp_row = mp_ref[h:h + 1, rows]
            w_inter = jnp.exp2(mp_row - mmb[h])
            num = w_inter * qs[h][:, :LANES] + sv[h][:, :LANES]
            den = w_inter * qs[h][:, LANES:] + sv[h][:, LANES:]
            hh = num / jnp.maximum(jnp.abs(den), jnp.exp2(-(bb[h] + mmb[h])))
            hh = hh * lax.rsqrt(jnp.sum(hh * hh, axis=1, keepdims=True) * (1.0 / M_DH) + EPS)
            y = hh * hg_ref[:, cols[h]] + skip_ref[:, cols[h]] * cu[h].astype(f32)
            y = y * _silu(z_ref[0, rows, cols[h]].astype(f32))
            o_ref[0, rows, cols[h]] = y.astype(o_ref.dtype)
            cs = jnp.exp2(mp_row - ml_ref[h:h + 1, rows])
            st_ref[h] = jnp.concatenate([cs, cs], axis=1) * st[h] + upd[h]
        return carry

    lax.fori_loop(0, nchunks, chunk, 0, unroll=4)


def _mlstm(proj, gates, gbias, wq, wkt, hn_g, skip):
    bsz, seq, _ = proj.shape
    kern = functools.partial(_mlstm_kernel, seq=seq)
    blk = lambda j: pl.BlockSpec((1, seq, M_WP), lambda b: (b, 0, j))
    full = lambda shape: pl.BlockSpec(shape, lambda b: (0,) * len(shape))
    rows = pltpu.VMEM((GATE_ROWS, seq), jnp.float32)
    return pl.pallas_call(
        kern,
        grid=(bsz,),
        in_specs=[blk(OFF_MU // M_WP), blk(OFF_MV // M_WP), blk(OFF_MZ // M_WP),
                  pl.BlockSpec((1, GATE_ROWS, seq), lambda b: (b, 0, 0)),
                  full((GATE_ROWS, M_CHUNK)), full((M_HEADS, LANES, LANES)), full((M_HEADS, LANES, LANES)),
                  full((1, M_WP)), full((1, M_WP))],
        out_specs=pl.BlockSpec((1, seq, M_WP), lambda b: (b, 0, 0)),
        out_shape=jax.ShapeDtypeStruct((bsz, seq, M_WP), jnp.bfloat16),
        scratch_shapes=[pltpu.VMEM((M_HEADS, LANES, 2 * LANES), jnp.float32), rows, rows, rows, rows],
        compiler_params=_params(1),
        name="mlstm",
    )(proj, proj, proj, gates, gbias, wq, wkt, hn_g, skip)


def _swa_consts():
    W = WINDOW
    ti = lax.broadcasted_iota(jnp.int32, (W, W), 0)
    si = lax.broadcasted_iota(jnp.int32, (W, W), 1)
    cur = si <= ti
    neg_rel = -jnp.where(cur, ti - si, W + ti - si).astype(jnp.float32)
    lane = lax.broadcasted_iota(jnp.int32, (W, LANES), 1)
    lane2 = lax.broadcasted_iota(jnp.int32, (2 * W, LANES), 1)
    return cur, neg_rel, (lane < A_DH, lane >= A_DH), (lane2 < A_DH, lane2 >= A_DH)


def _swa_block(consts, n, q_all, z, kc, kp, vc, vp, slope_ref, sink_ref):
    f32 = jnp.float32
    bf16 = jnp.bfloat16
    W = WINDOW
    cur, neg_rel, halves, halves2 = consts
    bias = jnp.where((cur.astype(jnp.int32) + n) > 0, neg_rel, -jnp.inf)
    kk = jnp.concatenate([kc, kp], axis=0)
    vv = jnp.concatenate([vc, vp], axis=0)
    vms = [jnp.where(hm, vv, jnp.zeros_like(vv)) for hm in halves2]
    q_all = q_all * jnp.asarray(A_DH ** -0.5, q_all.dtype)
    pairs = [(g, kv) for g in range(A_G) for kv in range(A_KV)]
    sc = []
    for g, kv in pairs:
        qg = q_all[:, g * LANES:(g + 1) * LANES]
        qm = jnp.where(halves[kv], qg, jnp.zeros_like(qg))
        sc.append(lax.dot_general(qm, kk, _NT, preferred_element_type=f32))
    pc = []
    for i, (g, kv) in enumerate(pairs):
        head = kv * A_G + g
        s2 = jnp.where(cur, sc[i][:, :W], sc[i][:, W:]) + slope_ref[head] * bias
        sink = sink_ref[head]
        mx = jnp.maximum(jnp.max(s2, axis=1, keepdims=True), sink)
        e = jnp.exp(s2 - mx)
        den = jnp.sum(e, axis=1, keepdims=True) + jnp.exp(sink - mx)
        probs = (e * (1.0 / den)).astype(bf16)
        zero = jnp.zeros_like(probs)
        pc.append(jnp.concatenate([jnp.where(cur, probs, zero), jnp.where(cur, zero, probs)], axis=1))
    outs = []
    for g in range(A_G):
        o = (jnp.dot(pc[2 * g], vms[0], preferred_element_type=f32)
             + jnp.dot(pc[2 * g + 1], vms[1], preferred_element_type=f32))
        outs.append((o * _silu(z[:, g * LANES:(g + 1) * LANES].astype(f32))).astype(bf16))
    return jnp.concatenate(outs, axis=1)


def _sb_kernel(q_ref, z_ref, k_ref, v_ref, o_ref, acc_ref, run_ref, zz_ref, a_ref, *, seq):
    f32 = jnp.float32
    bf16 = jnp.bfloat16
    T = SB_TILE
    lane = lax.broadcasted_iota(jnp.int32, (T, LANES), 1)
    halves = (lane < S_DH, lane >= S_DH)
    ti = lax.broadcasted_iota(jnp.int32, (T, T), 0)
    si = lax.broadcasted_iota(jnp.int32, (T, T), 1)
    strict = si < ti
    usum = jnp.where(ti > si, 1.0, 0.0).astype(bf16)
    heads = range(S_HEADS)
    cols = [slice((h // 2) * LANES, (h // 2 + 1) * LANES) for h in heads]

    def qblock(n, carry):
        r0 = pl.multiple_of(n * T, T)
        q = q_ref[0, pl.ds(r0, T), :]
        q = q * jnp.asarray(S_DH ** -0.5, q.dtype)
        qm = [jnp.where(halves[h % 2], q[:, cols[h]], jnp.zeros((T, LANES), q.dtype)) for h in heads]
        acc_ref[...] = jnp.zeros(acc_ref.shape, f32)
        run_ref[...] = jnp.zeros(run_ref.shape, f32)

        def score(j):
            k0 = pl.multiple_of(j * T, T)
            kb = k_ref[0, pl.ds(k0, T), :]
            for h in heads:
                zz_ref[h] = lax.dot_general(qm[h], kb[:, cols[h]], _NT, preferred_element_type=f32)

        def apply(j):
            k0 = pl.multiple_of(j * T, T)
            vb = v_ref[0, pl.ds(k0, T), :]
            for p in range(S_PAIRS):
                pv = None
                for h in (2 * p, 2 * p + 1):
                    vm = jnp.where(halves[h % 2], vb[:, cols[h]], jnp.zeros((T, LANES), vb.dtype))
                    d = jnp.dot(a_ref[h], vm, preferred_element_type=f32)
                    pv = d if pv is None else pv + d
                acc_ref[:, cols[2 * p]] += pv

        def weights(j_next, diag):
            ls, lk = [], []
            for h in heads:
                zz = zz_ref[h]
                soft = jnp.log(1.0 + jnp.exp2(jnp.abs(zz) * (-LOG2E)))
                ls.append(jnp.minimum(zz, 0.0) - soft)
                lkh = ls[h] - zz
                lk.append(jnp.where(strict, lkh, 0.0) if diag else lkh)
            suf = [jnp.dot(lk[h].astype(bf16), usum, preferred_element_type=f32) for h in heads]
            runs = [run_ref[h] for h in heads]
            new_runs = [runs[h] + jnp.sum(lk[h], axis=1, keepdims=True) for h in heads]
            live = jnp.max(functools.reduce(jnp.maximum, new_runs)) > SB_DEAD
            score(j_next)
            for h in heads:
                ah = jnp.exp2((ls[h] + suf[h] + jnp.concatenate([runs[h]] * (T // LANES), axis=1)) * LOG2E)
                a_ref[h] = (jnp.where(strict, ah, 0.0) if diag else ah).astype(bf16)
                run_ref[h] = new_runs[h]
            return live.astype(jnp.int32)

        score(n)
        live0 = weights(max(n - 1, 0), True)

        def cond(c):
            i, live = c
            return jnp.logical_and(i <= n, live > 0)

        def body(c):
            i, _ = c
            apply(n - i + 1)
            return i + 1, weights(jnp.maximum(n - i - 1, 0), False)

        i_end, _ = lax.while_loop(cond, body, (jnp.int32(1), live0))
        apply(n - i_end + 1)
        zg = z_ref[0, pl.ds(r0, T), :].astype(f32)
        o_ref[0, pl.ds(r0, T), :] = (acc_ref[...] * _silu(zg)).astype(o_ref.dtype)
        return carry

    for n in range(seq // T):
        qblock(n, 0)


def _stickbreak(proj):
    bsz, seq, _ = proj.shape
    kern = functools.partial(_sb_kernel, seq=seq)
    full = lambda off: pl.BlockSpec((1, seq, S_W), lambda b: (b, 0, off // S_W))
    return pl.pallas_call(
        kern,
        grid=(bsz,),
        in_specs=[full(OFF_SQ), full(OFF_SZ), full(OFF_SK), full(OFF_SV)],
        out_specs=pl.BlockSpec((1, seq, S_W), lambda b: (b, 0, 0)),
        out_shape=jax.ShapeDtypeStruct((bsz, seq, S_W), jnp.bfloat16),
        scratch_shapes=[pltpu.VMEM((SB_TILE, S_W), jnp.float32),
                        pltpu.VMEM((S_HEADS, SB_TILE, LANES), jnp.float32),
                        pltpu.VMEM((S_HEADS, SB_TILE, SB_TILE), jnp.float32),
                        pltpu.VMEM((S_HEADS, SB_TILE, SB_TILE), jnp.bfloat16)],
        compiler_params=_params(1),
        name="stickbreak",
    )(proj, proj, proj, proj)


def _outproj_kernel(x_ref, ym_ref, ya_ref, ys_ref, wm_ref, wa_ref, ws_ref, g_ref, gate_ref, o_ref, *, tm, sub):
    f32 = jnp.float32
    gain = gate_ref[...] * g_ref[...]
    for i in range(tm // sub):
        r = slice(i * sub, (i + 1) * sub)
        y = (jnp.dot(ym_ref[0, r, :], wm_ref[...], preferred_element_type=f32)
             + jnp.dot(ya_ref[0, r, :], wa_ref[...], preferred_element_type=f32)
             + jnp.dot(ys_ref[0, r, :], ws_ref[...], preferred_element_type=f32))
        o_ref[0, r, :] = x_ref[0, r, :] + y * lax.rsqrt(jnp.mean(y * y, axis=-1, keepdims=True) + EPS) * gain


def _outproj(x, ym, ya, ys, wm, wa, ws, g_post, mod4, layer):
    bsz, seq, d = x.shape
    tm = OUTPROJ_ROWS
    kern = functools.partial(_outproj_kernel, tm=tm, sub=OUTPROJ_SUB)
    row = lambda w: pl.BlockSpec((1, tm, w), lambda b, s: (b, s, 0))
    const = lambda shape: pl.BlockSpec(shape, lambda b, s: (0,) * len(shape), pipeline_mode=pl.Buffered(1))
    return pl.pallas_call(
        kern,
        grid=(bsz, seq // tm),
        in_specs=[row(d), row(M_WP), row(A_W), row(S_W),
                  const((M_WP, d)), const((A_W, d)), const((S_W, d)), const((1, d)),
                  pl.BlockSpec((None, None, 1, d), lambda b, s: (layer, b, 0, 2))],
        out_specs=row(d),
        out_shape=jax.ShapeDtypeStruct((bsz, seq, d), jnp.float32),
        compiler_params=_params(2),
        name="outproj",
    )(x, ym, ya, ys, wm, wa, ws, g_post, mod4)


def _pad_heads(w, axis):
    shape = list(w.shape)
    shape[axis:axis + 1] = [M_HEADS, M_DH]
    w = w.reshape(shape)
    pad = [(0, 0)] * w.ndim
    pad[axis + 1] = (0, LANES - M_DH)
    w = jnp.pad(w, pad)
    shape[axis:axis + 2] = [M_WP]
    return w.reshape(shape)


def _pair_heads(w, axis):
    shape = list(w.shape)
    shape[axis:axis + 1] = [A_KV, A_G, A_DH]
    w = jnp.swapaxes(w.reshape(shape), axis, axis + 1)
    shape[axis:axis + 3] = [A_W]
    return w.reshape(shape)


def _pack_w_in(w):
    o = np.cumsum([0, M_W, M_W, M_HEADS, M_HEADS, M_W, A_W, A_KVW, A_KVW, A_W, S_W, S_W, S_W, S_W])
    wt = w.astype(jnp.bfloat16).T
    seg = lambda i: wt[int(o[i]):int(o[i + 1])]
    packed_t = jnp.concatenate(
        [_pad_heads(seg(0), 0), _pad_heads(seg(1), 0), _pad_heads(seg(4), 0),
         _pair_heads(seg(5), 0), _pair_heads(seg(8), 0), seg(6), seg(7),
         seg(9), seg(10), seg(11), seg(12)], axis=0)
    gates_t = jnp.concatenate([seg(2), seg(3)], axis=0)
    return packed_t, gates_t


def _pack_w_out(w):
    w = w.astype(jnp.bfloat16)
    return _pad_heads(w[:M_W], 0), _pair_heads(w[M_W:M_W + A_W], 0), w[M_W + A_W:]


def _pad_qk(w):
    return jnp.pad(w, ((0, 0), (0, LANES - M_DH), (0, LANES - M_DH))).astype(jnp.bfloat16)


def kernel(x, c, w_mod, b_mod, g_pre, g_post, w_in, m_conv_w, m_conv_b, m_wq, m_wk, m_b_i, m_b_f,
           m_norm_g, m_skip, a_sinks, w_out):
    bsz = x.shape[0]
    mod4 = _modulation(c, w_mod, b_mod).reshape(DEPTH, bsz, 1, 3 * D_MODEL)
    slopes = jnp.asarray(2.0 ** (-8.0 * np.arange(1, A_HEADS + 1) / A_HEADS), dtype=jnp.float32)
    for l in range(DEPTH):
        w_pack, wg_t = _pack_w_in(w_in[l])
        wm, wa, ws = _pack_w_out(w_out[l])
        gbias = jnp.broadcast_to(jnp.concatenate([m_b_i[l], m_b_f[l]])[:, None], (2 * M_HEADS, M_CHUNK))
        proj, gates, ya = _inproj(x, mod4, l, g_pre[l][None], w_pack, wg_t,
                                  _pad_heads(m_conv_w[l], 1), _pad_heads(m_conv_b[l][None], 1), slopes, a_sinks[l])
        ym = _mlstm(proj, gates, gbias, _pad_qk(m_wq[l]), _pad_qk(jnp.swapaxes(m_wk[l], 1, 2)),
                    _pad_heads(m_norm_g[l][None], 1), _pad_heads(m_skip[l][None], 1))
        ys = _stickbreak(proj)
        x = _outproj(x, ym, ya, ys, wm, wa, ws, g_post[l][None], mod4, l)
    return x
```

```python
import functools

import jax
import jax.numpy as jnp
import numpy as np
from jax import lax
from jax.experimental import pallas as pl
from jax.experimental.pallas import tpu as pltpu

D_MODEL = 1024
DEPTH = 2
M_HEADS = 4
M_DH = 96
M_W = M_HEADS * M_DH
M_CONV = 4
A_HEADS = 6
A_KV = 2
A_G = A_HEADS // A_KV
A_DH = 64
A_W = A_HEADS * A_DH
A_KVW = A_KV * A_DH
WINDOW = 128
S_HEADS = 4
S_DH = 64
S_W = S_HEADS * S_DH
EPS = 1e-6

LANES = 128
SUBLANES = 8
GATE_ROWS = 2 * M_HEADS
M_CHUNK = 128
M_WP = M_HEADS * LANES
SB_TILE = 256
S_PAIRS = S_W // LANES
INPROJ_ROWS, INPROJ_SUB, INPROJ_COLS = 1024, 128, 512
OUTPROJ_ROWS, OUTPROJ_SUB = 2048, 256
SB_DEAD = -93.0

OFF_MU, OFF_MV, OFF_MZ = 0, M_WP, 2 * M_WP
OFF_AQ = 3 * M_WP
OFF_AZ = OFF_AQ + A_W
OFF_AK = OFF_AZ + A_W
OFF_AV = OFF_AK + A_KVW
OFF_SQ = OFF_AV + A_KVW
OFF_SK = OFF_SQ + S_W
OFF_SV = OFF_SK + S_W
OFF_SZ = OFF_SV + S_W
N_PACK = OFF_SZ + S_W

VMEM_LIMIT = 48 * 1024 * 1024

_NT = (((1,), (1,)), ((), ()))
LOG2E = 1.4426950408889634


def _log_sigmoid(x):
    return jnp.minimum(x, 0.0) - jnp.log(1.0 + jnp.exp(-jnp.abs(x)))


def _silu(x):
    u = 0.5 * x
    return u + u * jnp.tanh(u)


def _params(n_axes):
    return pltpu.CompilerParams(dimension_semantics=("arbitrary",) * n_axes, vmem_limit_bytes=VMEM_LIMIT)


def _mod_kernel(c_ref, w_ref, b_ref, o_ref):
    c_act = _silu(c_ref[...]).astype(jnp.bfloat16)
    o_ref[0] = jnp.dot(c_act, w_ref[0].astype(jnp.bfloat16), preferred_element_type=jnp.float32) + b_ref[0]


def _modulation(c, w_mod, b_mod):
    depth, d, n = w_mod.shape
    bsz = c.shape[0]
    tn = 1024
    return pl.pallas_call(
        _mod_kernel,
        grid=(depth, n // tn),
        in_specs=[pl.BlockSpec((bsz, d), lambda l, j: (0, 0)),
                  pl.BlockSpec((1, d, tn), lambda l, j: (l, 0, j)),
                  pl.BlockSpec((1, 1, tn), lambda l, j: (l, 0, j))],
        out_specs=pl.BlockSpec((1, bsz, tn), lambda l, j: (l, 0, j)),
        out_shape=jax.ShapeDtypeStruct((depth, bsz, n), jnp.float32),
        compiler_params=_params(2),
        name="modulation",
    )(c, w_mod, b_mod.reshape(depth, 1, n))


def _inproj_kernel(x_ref, shift_ref, scale_ref, g_ref, wt_ref, wg_ref, cw_ref, cb_ref, slope_ref, sink_ref,
                   proj_ref, gates_ref, ya_ref, conv_ref, w_ref, kv_ref, *, tm, sub, nchunk):
    s = pl.program_id(1)
    f32 = jnp.float32

    @pl.when((pl.program_id(0) == 0) & (s == 0))
    def _():
        for c0 in range(0, N_PACK, 2 * LANES):
            w_ref[:, c0:c0 + 2 * LANES] = wt_ref[c0:c0 + 2 * LANES, :].T

    @pl.when(s == 0)
    def _():
        conv_ref[0:SUBLANES, :] = jnp.zeros((SUBLANES, M_WP), f32)
        kv_ref[...] = jnp.zeros(kv_ref.shape, kv_ref.dtype)

    subs = [slice(i * sub, (i + 1) * sub) for i in range(tm // sub)]
    gain = g_ref[...] * (1.0 + scale_ref[...])
    hs = []
    for r in subs:
        x = x_ref[0, r, :]
        ms = jnp.mean(x * x, axis=-1, keepdims=True)
        hs.append((x * lax.rsqrt(ms + EPS) * gain + shift_ref[...]).astype(jnp.bfloat16))

    swa_consts = _swa_consts()
    grab = lambda rows, off, w: proj_ref[0, rows, off:off + w]

    def finish(item):
        i, pc = item
        vp = kv_ref[1] if i == 0 else grab(subs[i - 1], OFF_AV, A_KVW)
        ya_ref[0, subs[i], :] = _swa_apply(swa_consts, pc, grab(subs[i], OFF_AZ, A_W), grab(subs[i], OFF_AV, A_KVW), vp)

    pending = None
    for i, (r, h) in enumerate(zip(subs, hs)):
        gates_ref[0, :, r] = lax.dot_general(wg_ref[...], h, _NT, preferred_element_type=f32)

        base = SUBLANES + i * sub
        conv_ref[base:base + sub, :] = jnp.dot(h, w_ref[:, OFF_MU:OFF_MU + M_WP], preferred_element_type=f32)
        acc = cb_ref[...] + cw_ref[M_CONV - 1:M_CONV, :] * conv_ref[base:base + sub, :]
        for j in range(M_CONV - 1):
            lag = M_CONV - 1 - j
            acc = acc + cw_ref[j:j + 1, :] * conv_ref[base - lag:base - lag + sub, :]
        proj_ref[0, r, OFF_MU:OFF_MU + M_WP] = _silu(acc).astype(jnp.bfloat16)

        for c0 in range(OFF_MV, N_PACK, nchunk):
            proj_ref[0, r, c0:c0 + nchunk] = jnp.dot(
                h, w_ref[:, c0:c0 + nchunk], preferred_element_type=f32).astype(jnp.bfloat16)
            if c0 + nchunk == OFF_MZ + M_WP and pending is not None:
                finish(pending)
                pending = None
            if c0 <= OFF_AV < c0 + nchunk:
                kp = kv_ref[0] if i == 0 else grab(subs[i - 1], OFF_AK, A_KVW)
                pending = (i, _swa_probs(swa_consts, s * (tm // sub) + i, grab(r, OFF_AQ, A_W),
                                         grab(r, OFF_AK, A_KVW), kp, slope_ref, sink_ref))

    finish(pending)
    conv_ref[0:SUBLANES, :] = conv_ref[tm:tm + SUBLANES, :]
    kv_ref[0] = proj_ref[0, subs[-1], OFF_AK:OFF_AK + A_KVW]
    kv_ref[1] = proj_ref[0, subs[-1], OFF_AV:OFF_AV + A_KVW]


def _inproj(x, mod4, layer, g_pre, w_pack, wg_t, conv_w, conv_b, slopes, sinks):
    bsz, seq, d = x.shape
    tm = INPROJ_ROWS
    assert INPROJ_SUB == WINDOW
    kern = functools.partial(_inproj_kernel, tm=tm, sub=INPROJ_SUB, nchunk=INPROJ_COLS)
    const = lambda shape: pl.BlockSpec(shape, lambda b, s: (0,) * len(shape), pipeline_mode=pl.Buffered(1))
    smem = pl.BlockSpec(memory_space=pltpu.SMEM)
    return pl.pallas_call(
        kern,
        grid=(bsz, seq // tm),
        in_specs=[pl.BlockSpec((1, tm, d), lambda b, s: (b, s, 0)),
                  pl.BlockSpec((None, None, 1, d), lambda b, s: (layer, b, 0, 0)),
                  pl.BlockSpec((None, None, 1, d), lambda b, s: (layer, b, 0, 1)),
                  const((1, d)), const((N_PACK, d)), const((GATE_ROWS, d)), const((M_CONV, M_WP)), const((1, M_WP)),
                  smem, smem],
        out_specs=[pl.BlockSpec((1, tm, N_PACK), lambda b, s: (b, s, 0)),
                   pl.BlockSpec((1, GATE_ROWS, tm), lambda b, s: (b, 0, s)),
                   pl.BlockSpec((1, tm, A_W), lambda b, s: (b, s, 0))],
        out_shape=[jax.ShapeDtypeStruct((bsz, seq, N_PACK), jnp.bfloat16),
                   jax.ShapeDtypeStruct((bsz, GATE_ROWS, seq), jnp.float32),
                   jax.ShapeDtypeStruct((bsz, seq, A_W), jnp.bfloat16)],
        scratch_shapes=[pltpu.VMEM((tm + SUBLANES, M_WP), jnp.float32), pltpu.VMEM((d, N_PACK), jnp.bfloat16),
                        pltpu.VMEM((2, WINDOW, A_KVW), jnp.bfloat16)],
        compiler_params=_params(2),
        name="inproj",
    )(x, mod4, mod4, g_pre, w_pack, wg_t, conv_w, conv_b, slopes, sinks)


def _mlstm_kernel(cu_ref, v_ref, z_ref, gates_ref, gbias_ref, wq_ref, wkt_ref, hg_ref, skip_ref,
                  o_ref, st_ref, a_ref, b_ref, mp_ref, ml_ref, *, seq):
    L = M_CHUNK
    nchunks = seq // L
    f32 = jnp.float32
    bf16 = jnp.bfloat16
    lane8 = lax.broadcasted_iota(jnp.int32, (GATE_ROWS, L), 1)
    row8 = lax.broadcasted_iota(jnp.int32, (GATE_ROWS, L), 0)

    amax, gsum = [], []
    for c in range(nchunks):
        g = gates_ref[0, :, c * L:(c + 1) * L] + gbias_ref[...]
        cum = jnp.where(row8 >= M_HEADS, _log_sigmoid(g), 0.0)
        k = 1
        while k < L:
            cum = cum + jnp.where(lane8 >= k, pltpu.roll(cum, k, axis=1), 0.0)
            k *= 2
        b8 = pltpu.roll(cum, M_HEADS, axis=0) * LOG2E
        a8 = g * LOG2E - b8
        a_ref[:, c * L:(c + 1) * L] = a8
        b_ref[:, c * L:(c + 1) * L] = b8
        amax.append(jnp.broadcast_to(jnp.max(a8, axis=1, keepdims=True), (GATE_ROWS, L)))
        gsum.append(jnp.broadcast_to(jnp.max(jnp.where(lane8 == L - 1, b8, -jnp.inf), axis=1, keepdims=True),
                                     (GATE_ROWS, L)))
    m_prev = jnp.zeros((GATE_ROWS, L), f32)
    for c in range(nchunks):
        m_last = jnp.maximum(m_prev, amax[c])
        mp_ref[:, c * L:(c + 1) * L] = m_prev
        ml_ref[:, c * L:(c + 1) * L] = m_last
        m_prev = gsum[c] + m_last

    st_ref[...] = jnp.zeros(st_ref.shape, f32)

    t_idx = lax.broadcasted_iota(jnp.int32, (L, L), 0)
    s_idx = lax.broadcasted_iota(jnp.int32, (L, L), 1)
    tri = s_idx <= t_idx
    eye = s_idx == t_idx
    ones = jnp.ones((L, LANES), bf16)
    scale = M_DH ** -0.5
    heads = range(M_HEADS)
    cols = [slice(h * LANES, (h + 1) * LANES) for h in heads]

    def chunk(c, carry):
        r0 = pl.multiple_of(c * L, L)
        rows = pl.ds(r0, L)
        cu = [cu_ref[0, rows, cols[h]] for h in heads]
        q = [(jnp.dot(cu[h], wq_ref[h], preferred_element_type=f32) * scale).astype(bf16) for h in heads]
        kt = [lax.dot_general(wkt_ref[h], cu[h], _NT, preferred_element_type=f32) for h in heads]
        st = [st_ref[h] for h in heads]
        qk = [jnp.dot(q[h], kt[h].astype(bf16), preferred_element_type=f32) for h in heads]
        qs = [jnp.dot(q[h], st[h].astype(bf16), preferred_element_type=f32) for h in heads]
        smat, mmb, bb = [], [], []
        for h in heads:
            a_mat = jnp.broadcast_to(a_ref[h:h + 1, rows], (L, L))
            m1 = jnp.max(jnp.where(tri, a_mat, -jnp.inf), axis=1, keepdims=True)
            mm = jnp.maximum(mp_ref[h:h + 1, rows], m1)
            p = jnp.where(tri, jnp.exp2(a_mat - mm), 0.0)
            smat.append((qk[h] * p).astype(bf16))
            mmb.append(mm)
            b_mat = jnp.broadcast_to(b_ref[h:h + 1, rows], (L, L))
            bb.append(jnp.broadcast_to(jnp.sum(jnp.where(eye, b_mat, 0.0), axis=1, keepdims=True), (L, LANES)))
        v2 = [jnp.concatenate([v_ref[0, rows, cols[h]], ones], axis=1) for h in heads]
        sv = [jnp.dot(smat[h], v2[h], preferred_element_type=f32) for h in heads]
        upd = []
        for h in heads:
            ws = jnp.exp2(a_ref[h:h + 1, rows] - ml_ref[h:h + 1, rows])
            upd.append(jnp.dot((kt[h] * ws).astype(bf16), v2[h], preferred_element_type=f32))
        for h in heads:
            mp_row = mp_ref[h:h + 1, rows]
            w_inter = jnp.exp2(mp_row - mmb[h])
            num = w_inter * qs[h][:, :LANES] + sv[h][:, :LANES]
            den = w_inter * qs[h][:, LANES:] + sv[h][:, LANES:]
            hh = num / jnp.maximum(jnp.abs(den), jnp.exp2(-(bb[h] + mmb[h])))
            hh = hh * lax.rsqrt(jnp.sum(hh * hh, axis=1, keepdims=True) * (1.0 / M_DH) + EPS)
            y = hh * hg_ref[:, cols[h]] + skip_ref[:, cols[h]] * cu[h].astype(f32)
            y = y * _silu(z_ref[0, rows, cols[h]].astype(f32))
            o_ref[0, rows, cols[h]] = y.astype(o_ref.dtype)
            cs = jnp.exp2(mp_row - ml_ref[h:h + 1, rows])
            st_ref[h] = jnp.concatenate([cs, cs], axis=1) * st[h] + upd[h]
        return carry

    lax.fori_loop(0, nchunks, chunk, 0, unroll=4)


def _mlstm(proj, gates, gbias, wq, wkt, hn_g, skip):
    bsz, seq, _ = proj.shape
    kern = functools.partial(_mlstm_kernel, seq=seq)
    blk = lambda j: pl.BlockSpec((1, seq, M_WP), lambda b: (b, 0, j))
    full = lambda shape: pl.BlockSpec(shape, lambda b: (0,) * len(shape))
    rows = pltpu.VMEM((GATE_ROWS, seq), jnp.float32)
    return pl.pallas_call(
        kern,
        grid=(bsz,),
        in_specs=[blk(OFF_MU // M_WP), blk(OFF_MV // M_WP), blk(OFF_MZ // M_WP),
                  pl.BlockSpec((1, GATE_ROWS, seq), lambda b: (b, 0, 0)),
                  full((GATE_ROWS, M_CHUNK)), full((M_HEADS, LANES, LANES)), full((M_HEADS, LANES, LANES)),
                  full((1, M_WP)), full((1, M_WP))],
        out_specs=pl.BlockSpec((1, seq, M_WP), lambda b: (b, 0, 0)),
        out_shape=jax.ShapeDtypeStruct((bsz, seq, M_WP), jnp.bfloat16),
        scratch_shapes=[pltpu.VMEM((M_HEADS, LANES, 2 * LANES), jnp.float32), rows, rows, rows, rows],
        compiler_params=_params(1),
        name="mlstm",
    )(proj, proj, proj, gates, gbias, wq, wkt, hn_g, skip)


def _swa_consts():
    W = WINDOW
    ti = lax.broadcasted_iota(jnp.int32, (W, W), 0)
    si = lax.broadcasted_iota(jnp.int32, (W, W), 1)
    cur = si <= ti
    neg_rel = -jnp.where(cur, ti - si, W + ti - si).astype(jnp.float32)
    lane = lax.broadcasted_iota(jnp.int32, (W, LANES), 1)
    lane2 = lax.broadcasted_iota(jnp.int32, (2 * W, LANES), 1)
    return cur, neg_rel, (lane < A_DH, lane >= A_DH), (lane2 < A_DH, lane2 >= A_DH)


def _swa_probs(consts, n, q_all, kc, kp, slope_ref, sink_ref):
    f32 = jnp.float32
    W = WINDOW
    cur, neg_rel, halves, _ = consts
    bias = jnp.where((cur.astype(jnp.int32) + n) > 0, neg_rel, -jnp.inf)
    kk = jnp.concatenate([kc, kp], axis=0)
    q_all = q_all * jnp.asarray(A_DH ** -0.5, q_all.dtype)
    pairs = [(g, kv) for g in range(A_G) for kv in range(A_KV)]
    sc = []
    for g, kv in pairs:
        qg = q_all[:, g * LANES:(g + 1) * LANES]
        qm = jnp.where(halves[kv], qg, jnp.zeros_like(qg))
        sc.append(lax.dot_general(qm, kk, _NT, preferred_element_type=f32))
    pc = []
    for i, (g, kv) in enumerate(pairs):
        head = kv * A_G + g
        s2 = jnp.where(cur, sc[i][:, :W], sc[i][:, W:]) + slope_ref[head] * bias
        sink = sink_ref[head]
        mx = jnp.maximum(jnp.max(s2, axis=1, keepdims=True), sink)
        e = jnp.exp(s2 - mx)
        den = jnp.sum(e, axis=1, keepdims=True) + jnp.exp(sink - mx)
        probs = (e * (1.0 / den)).astype(jnp.bfloat16)
        zero = jnp.zeros_like(probs)
        pc.append(jnp.concatenate([jnp.where(cur, probs, zero), jnp.where(cur, zero, probs)], axis=1))
    return pc


def _swa_apply(consts, pc, z, vc, vp):
    f32 = jnp.float32
    halves2 = consts[3]
    vv = jnp.concatenate([vc, vp], axis=0)
    vms = [jnp.where(hm, vv, jnp.zeros_like(vv)) for hm in halves2]
    outs = []
    for g in range(A_G):
        o = (jnp.dot(pc[2 * g], vms[0], preferred_element_type=f32)
             + jnp.dot(pc[2 * g + 1], vms[1], preferred_element_type=f32))
        outs.append((o * _silu(z[:, g * LANES:(g + 1) * LANES].astype(f32))).astype(jnp.bfloat16))
    return jnp.concatenate(outs, axis=1)


def _sb_kernel(q_ref, z_ref, k_ref, v_ref, o_ref, acc_ref, run_ref, zz_ref, a_ref, *, seq):
    f32 = jnp.float32
    bf16 = jnp.bfloat16
    T = SB_TILE
    lane = lax.broadcasted_iota(jnp.int32, (T, LANES), 1)
    halves = (lane < S_DH, lane >= S_DH)
    ti = lax.broadcasted_iota(jnp.int32, (T, T), 0)
    si = lax.broadcasted_iota(jnp.int32, (T, T), 1)
    strict = si < ti
    usum = jnp.where(ti > si, 1.0, 0.0).astype(bf16)
    heads = range(S_HEADS)
    cols = [slice((h // 2) * LANES, (h // 2 + 1) * LANES) for h in heads]

    def qblock(n, carry):
        r0 = pl.multiple_of(n * T, T)
        q = q_ref[0, pl.ds(r0, T), :]
        q = q * jnp.asarray(S_DH ** -0.5, q.dtype)
        qm = [jnp.where(halves[h % 2], q[:, cols[h]], jnp.zeros((T, LANES), q.dtype)) for h in heads]
        acc_ref[...] = jnp.zeros(acc_ref.shape, f32)
        run_ref[...] = jnp.zeros(run_ref.shape, f32)

        def score(j):
            k0 = pl.multiple_of(j * T, T)
            kb = k_ref[0, pl.ds(k0, T), :]
            for h in heads:
                zz_ref[h] = lax.dot_general(qm[h], kb[:, cols[h]], _NT, preferred_element_type=f32)

        def apply(j):
            k0 = pl.multiple_of(j * T, T)
            vb = v_ref[0, pl.ds(k0, T), :]
            for p in range(S_PAIRS):
                pv = None
                for h in (2 * p, 2 * p + 1):
                    vm = jnp.where(halves[h % 2], vb[:, cols[h]], jnp.zeros((T, LANES), vb.dtype))
                    d = jnp.dot(a_ref[h], vm, preferred_element_type=f32)
                    pv = d if pv is None else pv + d
                acc_ref[:, cols[2 * p]] += pv

        def weights(j_next, diag):
            ls, lk = [], []
            for h in heads:
                zz = zz_ref[h]
                soft = jnp.log(1.0 + jnp.exp2(jnp.abs(zz) * (-LOG2E)))
                ls.append(jnp.minimum(zz, 0.0) - soft)
                lkh = ls[h] - zz
                lk.append(jnp.where(strict, lkh, 0.0) if diag else lkh)
            suf = [jnp.dot(lk[h].astype(bf16), usum, preferred_element_type=f32) for h in heads]
            runs = [run_ref[h] for h in heads]
            new_runs = [runs[h] + jnp.sum(lk[h], axis=1, keepdims=True) for h in heads]
            live = jnp.max(functools.reduce(jnp.maximum, new_runs)) > SB_DEAD
            score(j_next)
            for h in heads:
                ah = jnp.exp2((ls[h] + suf[h] + jnp.concatenate([runs[h]] * (T // LANES), axis=1)) * LOG2E)
                a_ref[h] = (jnp.where(strict, ah, 0.0) if diag else ah).astype(bf16)
                run_ref[h] = new_runs[h]
            return live.astype(jnp.int32)

        score(n)
        live0 = weights(max(n - 1, 0), True)

        def cond(c):
            i, live = c
            return jnp.logical_and(i <= n, live > 0)

        def body(c):
            i, _ = c
            apply(n - i + 1)
            return i + 1, weights(jnp.maximum(n - i - 1, 0), False)

        i_end, _ = lax.while_loop(cond, body, (jnp.int32(1), live0))
        apply(n - i_end + 1)
        zg = z_ref[0, pl.ds(r0, T), :].astype(f32)
        o_ref[0, pl.ds(r0, T), :] = (acc_ref[...] * _silu(zg)).astype(o_ref.dtype)
        return carry

    for n in range(seq // T):
        qblock(n, 0)


def _stickbreak(proj):
    bsz, seq, _ = proj.shape
    kern = functools.partial(_sb_kernel, seq=seq)
    full = lambda off: pl.BlockSpec((1, seq, S_W), lambda b: (b, 0, off // S_W))
    return pl.pallas_call(
        kern,
        grid=(bsz,),
        in_specs=[full(OFF_SQ), full(OFF_SZ), full(OFF_SK), full(OFF_SV)],
        out_specs=pl.BlockSpec((1, seq, S_W), lambda b: (b, 0, 0)),
        out_shape=jax.ShapeDtypeStruct((bsz, seq, S_W), jnp.bfloat16),
        scratch_shapes=[pltpu.VMEM((SB_TILE, S_W), jnp.float32),
                        pltpu.VMEM((S_HEADS, SB_TILE, LANES), jnp.float32),
                        pltpu.VMEM((S_HEADS, SB_TILE, SB_TILE), jnp.float32),
                        pltpu.VMEM((S_HEADS, SB_TILE, SB_TILE), jnp.bfloat16)],
        compiler_params=_params(1),
        name="stickbreak",
    )(proj, proj, proj, proj)


def _outproj_kernel(x_ref, ym_ref, ya_ref, ys_ref, wm_ref, wa_ref, ws_ref, g_ref, gate_ref, o_ref, *, tm, sub):
    f32 = jnp.float32
    gain = gate_ref[...] * g_ref[...]
    for i in range(tm // sub):
        r = slice(i * sub, (i + 1) * sub)
        y = (jnp.dot(ym_ref[0, r, :], wm_ref[...], preferred_element_type=f32)
             + jnp.dot(ya_ref[0, r, :], wa_ref[...], preferred_element_type=f32)
             + jnp.dot(ys_ref[0, r, :], ws_ref[...], preferred_element_type=f32))
        o_ref[0, r, :] = x_ref[0, r, :] + y * lax.rsqrt(jnp.mean(y * y, axis=-1, keepdims=True) + EPS) * gain


def _outproj(x, ym, ya, ys, wm, wa, ws, g_post, mod4, layer):
    bsz, seq, d = x.shape
    tm = OUTPROJ_ROWS
    kern = functools.partial(_outproj_kernel, tm=tm, sub=OUTPROJ_SUB)
    row = lambda w: pl.BlockSpec((1, tm, w), lambda b, s: (b, s, 0))
    const = lambda shape: pl.BlockSpec(shape, lambda b, s: (0,) * len(shape), pipeline_mode=pl.Buffered(1))
    return pl.pallas_call(
        kern,
        grid=(bsz, seq // tm),
        in_specs=[row(d), row(M_WP), row(A_W), row(S_W),
                  const((M_WP, d)), const((A_W, d)), const((S_W, d)), const((1, d)),
                  pl.BlockSpec((None, None, 1, d), lambda b, s: (layer, b, 0, 2))],
        out_specs=row(d),
        out_shape=jax.ShapeDtypeStruct((bsz, seq, d), jnp.float32),
        compiler_params=_params(2),
        name="outproj",
    )(x, ym, ya, ys, wm, wa, ws, g_post, mod4)


def _pad_heads(w, axis):
    shape = list(w.shape)
    shape[axis:axis + 1] = [M_HEADS, M_DH]
    w = w.reshape(shape)
    pad = [(0, 0)] * w.ndim
    pad[axis + 1] = (0, LANES - M_DH)
    w = jnp.pad(w, pad)
    shape[axis:axis + 2] = [M_WP]
    return w.reshape(shape)


def _pair_heads(w, axis):
    shape = list(w.shape)
    shape[axis:axis + 1] = [A_KV, A_G, A_DH]
    w = jnp.swapaxes(w.reshape(shape), axis, axis + 1)
    shape[axis:axis + 3] = [A_W]
    return w.reshape(shape)


def _pack_w_in(w):
    o = np.cumsum([0, M_W, M_W, M_HEADS, M_HEADS, M_W, A_W, A_KVW, A_KVW, A_W, S_W, S_W, S_W, S_W])
    wt = w.astype(jnp.bfloat16).T
    seg = lambda i: wt[int(o[i]):int(o[i + 1])]
    packed_t = jnp.concatenate(
        [_pad_heads(seg(0), 0), _pad_heads(seg(1), 0), _pad_heads(seg(4), 0),
         _pair_heads(seg(5), 0), _pair_heads(seg(8), 0), seg(6), seg(7),
         seg(9), seg(10), seg(11), seg(12)], axis=0)
    gates_t = jnp.concatenate([seg(2), seg(3)], axis=0)
    return packed_t, gates_t


def _pack_w_out(w):
    w = w.astype(jnp.bfloat16)
    return _pad_heads(w[:M_W], 0), _pair_heads(w[M_W:M_W + A_W], 0), w[M_W + A_W:]


def _pad_qk(w):
    return jnp.pad(w, ((0, 0), (0, LANES - M_DH), (0, LANES - M_DH))).astype(jnp.bfloat16)


def kernel(x, c, w_mod, b_mod, g_pre, g_post, w_in, m_conv_w, m_conv_b, m_wq, m_wk, m_b_i, m_b_f,
           m_norm_g, m_skip, a_sinks, w_out):
    bsz = x.shape[0]
    mod4 = _modulation(c, w_mod, b_mod).reshape(DEPTH, bsz, 1, 3 * D_MODEL)
    slopes = jnp.asarray(2.0 ** (-8.0 * np.arange(1, A_HEADS + 1) / A_HEADS), dtype=jnp.float32)
    for l in range(DEPTH):
        w_pack, wg_t = _pack_w_in(w_in[l])
        wm, wa, ws = _pack_w_out(w_out[l])
        gbias = jnp.broadcast_to(jnp.concatenate([m_b_i[l], m_b_f[l]])[:, None], (2 * M_HEADS, M_CHUNK))
        proj, gates, ya = _inproj(x, mod4, l, g_pre[l][None], w_pack, wg_t,
                                  _pad_heads(m_conv_w[l], 1), _pad_heads(m_conv_b[l][None], 1), slopes, a_sinks[l])
        ym = _mlstm(proj, gates, gbias, _pad_qk(m_wq[l]), _pad_qk(jnp.swapaxes(m_wk[l], 1, 2)),
                    _pad_heads(m_norm_g[l][None], 1), _pad_heads(m_skip[l][None], 1))
        ys = _stickbreak(proj)
        x = _outproj(x, ym, ya, ys, wm, wa, ws, g_post[l][None], mod4, l)
    return x
```

```python
import functools

import jax
import jax.numpy as jnp
import numpy as np
from jax import lax
from jax.experimental import pallas as pl
from jax.experimental.pallas import tpu as pltpu

D_MODEL = 1024
DEPTH = 2
M_HEADS = 4
M_DH = 96
M_W = M_HEADS * M_DH
M_CONV = 4
A_HEADS = 6
A_KV = 2
A_G = A_HEADS // A_KV
A_DH = 64
A_W = A_HEADS * A_DH
A_KVW = A_KV * A_DH
WINDOW = 128
S_HEADS = 4
S_DH = 64
S_W = S_HEADS * S_DH
EPS = 1e-6

LANES = 128
SUBLANES = 8
GATE_ROWS = 2 * M_HEADS
M_CHUNK = 128
M_WP = M_HEADS * LANES
SB_TILE = 256
S_PAIRS = S_W // LANES
INPROJ_ROWS, INPROJ_SUB, INPROJ_COLS = 1024, 128, 512
OUTPROJ_ROWS, OUTPROJ_SUB = 2048, 256
SB_DEAD = -93.0

OFF_MU, OFF_MV, OFF_MZ = 0, M_WP, 2 * M_WP
OFF_AQ = 3 * M_WP
OFF_AZ = OFF_AQ + A_W
OFF_AK = OFF_AZ + A_W
OFF_AV = OFF_AK + A_KVW
OFF_SQ = OFF_AV + A_KVW
OFF_SK = OFF_SQ + S_W
OFF_SV = OFF_SK + S_W
OFF_SZ = OFF_SV + S_W
N_PACK = OFF_SZ + S_W

VMEM_LIMIT = 48 * 1024 * 1024

_NT = (((1,), (1,)), ((), ()))
LOG2E = 1.4426950408889634


def _log_sigmoid(x):
    return jnp.minimum(x, 0.0) - jnp.log(1.0 + jnp.exp(-jnp.abs(x)))


def _silu(x):
    u = 0.5 * x
    return u + u * jnp.tanh(u)


def _params(n_axes):
    return pltpu.CompilerParams(dimension_semantics=("arbitrary",) * n_axes, vmem_limit_bytes=VMEM_LIMIT)


def _mod_kernel(c_ref, w_ref, b_ref, o_ref):
    c_act = _silu(c_ref[...]).astype(jnp.bfloat16)
    o_ref[0] = jnp.dot(c_act, w_ref[0].astype(jnp.bfloat16), preferred_element_type=jnp.float32) + b_ref[0]


def _modulation(c, w_mod, b_mod):
    depth, d, n = w_mod.shape
    bsz = c.shape[0]
    tn = 1024
    return pl.pallas_call(
        _mod_kernel,
        grid=(depth, n // tn),
        in_specs=[pl.BlockSpec((bsz, d), lambda l, j: (0, 0)),
                  pl.BlockSpec((1, d, tn), lambda l, j: (l, 0, j)),
                  pl.BlockSpec((1, 1, tn), lambda l, j: (l, 0, j))],
        out_specs=pl.BlockSpec((1, bsz, tn), lambda l, j: (l, 0, j)),
        out_shape=jax.ShapeDtypeStruct((depth, bsz, n), jnp.float32),
        compiler_params=_params(2),
        name="modulation",
    )(c, w_mod, b_mod.reshape(depth, 1, n))


def _inproj_kernel(x_ref, shift_ref, scale_ref, g_ref, wt_ref, wg_ref, cw_ref, cb_ref, slope_ref, sink_ref,
                   proj_ref, gates_ref, ya_ref, conv_ref, w_ref, kv_ref, *, tm, sub, nchunk):
    s = pl.program_id(1)
    f32 = jnp.float32

    @pl.when((pl.program_id(0) == 0) & (s == 0))
    def _():
        for c0 in range(0, N_PACK, 2 * LANES):
            w_ref[:, c0:c0 + 2 * LANES] = wt_ref[c0:c0 + 2 * LANES, :].T

    @pl.when(s == 0)
    def _():
        conv_ref[0:SUBLANES, :] = jnp.zeros((SUBLANES, M_WP), f32)
        kv_ref[...] = jnp.zeros(kv_ref.shape, kv_ref.dtype)

    subs = [slice(i * sub, (i + 1) * sub) for i in range(tm // sub)]
    gain = g_ref[...] * (1.0 + scale_ref[...])
    hs = []
    for r in subs:
        x = x_ref[0, r, :]
        ms = jnp.mean(x * x, axis=-1, keepdims=True)
        hs.append((x * lax.rsqrt(ms + EPS) * gain + shift_ref[...]).astype(jnp.bfloat16))

    swa_consts = _swa_consts()
    for i, (r, h) in enumerate(zip(subs, hs)):
        gates_ref[0, :, r] = lax.dot_general(wg_ref[...], h, _NT, preferred_element_type=f32)

        base = SUBLANES + i * sub
        conv_ref[base:base + sub, :] = jnp.dot(h, w_ref[:, OFF_MU:OFF_MU + M_WP], preferred_element_type=f32)
        acc = cb_ref[...] + cw_ref[M_CONV - 1:M_CONV, :] * conv_ref[base:base + sub, :]
        for j in range(M_CONV - 1):
            lag = M_CONV - 1 - j
            acc = acc + cw_ref[j:j + 1, :] * conv_ref[base - lag:base - lag + sub, :]
        proj_ref[0, r, OFF_MU:OFF_MU + M_WP] = _silu(acc).astype(jnp.bfloat16)

        for c0 in range(OFF_MV, N_PACK, nchunk):
            proj_ref[0, r, c0:c0 + nchunk] = jnp.dot(
                h, w_ref[:, c0:c0 + nchunk], preferred_element_type=f32).astype(jnp.bfloat16)

        grab = lambda rows, off, w: proj_ref[0, rows, off:off + w]
        kp = kv_ref[0] if i == 0 else grab(subs[i - 1], OFF_AK, A_KVW)
        vp = kv_ref[1] if i == 0 else grab(subs[i - 1], OFF_AV, A_KVW)
        ya_ref[0, r, :] = _swa_block(swa_consts, s * (tm // sub) + i, grab(r, OFF_AQ, A_W), grab(r, OFF_AZ, A_W),
                                     grab(r, OFF_AK, A_KVW), kp, grab(r, OFF_AV, A_KVW), vp, slope_ref, sink_ref)

    conv_ref[0:SUBLANES, :] = conv_ref[tm:tm + SUBLANES, :]
    kv_ref[0] = proj_ref[0, subs[-1], OFF_AK:OFF_AK + A_KVW]
    kv_ref[1] = proj_ref[0, subs[-1], OFF_AV:OFF_AV + A_KVW]


def _inproj(x, mod4, layer, g_pre, w_pack, wg_t, conv_w, conv_b, slopes, sinks):
    bsz, seq, d = x.shape
    tm = INPROJ_ROWS
    assert INPROJ_SUB == WINDOW
    kern = functools.partial(_inproj_kernel, tm=tm, sub=INPROJ_SUB, nchunk=INPROJ_COLS)
    const = lambda shape: pl.BlockSpec(shape, lambda b, s: (0,) * len(shape), pipeline_mode=pl.Buffered(1))
    smem = pl.BlockSpec(memory_space=pltpu.SMEM)
    return pl.pallas_call(
        kern,
        grid=(bsz, seq // tm),
        in_specs=[pl.BlockSpec((1, tm, d), lambda b, s: (b, s, 0)),
                  pl.BlockSpec((None, None, 1, d), lambda b, s: (layer, b, 0, 0)),
                  pl.BlockSpec((None, None, 1, d), lambda b, s: (layer, b, 0, 1)),
                  const((1, d)), const((N_PACK, d)), const((GATE_ROWS, d)), const((M_CONV, M_WP)), const((1, M_WP)),
                  smem, smem],
        out_specs=[pl.BlockSpec((1, tm, N_PACK), lambda b, s: (b, s, 0)),
                   pl.BlockSpec((1, GATE_ROWS, tm), lambda b, s: (b, 0, s)),
                   pl.BlockSpec((1, tm, A_W), lambda b, s: (b, s, 0))],
        out_shape=[jax.ShapeDtypeStruct((bsz, seq, N_PACK), jnp.bfloat16),
                   jax.ShapeDtypeStruct((bsz, GATE_ROWS, seq), jnp.float32),
                   jax.ShapeDtypeStruct((bsz, seq, A_W), jnp.bfloat16)],
        scratch_shapes=[pltpu.VMEM((tm + SUBLANES, M_WP), jnp.float32), pltpu.VMEM((d, N_PACK), jnp.bfloat16),
                        pltpu.VMEM((2, WINDOW, A_KVW), jnp.bfloat16)],
        compiler_params=_params(2),
        name="inproj",
    )(x, mod4, mod4, g_pre, w_pack, wg_t, conv_w, conv_b, slopes, sinks)


def _mlstm_kernel(cu_ref, v_ref, z_ref, gates_ref, gbias_ref, wq_ref, wkt_ref, hg_ref, skip_ref,
                  o_ref, st_ref, a_ref, b_ref, mp_ref, ml_ref, *, seq):
    L = M_CHUNK
    nchunks = seq // L
    f32 = jnp.float32
    bf16 = jnp.bfloat16
    lane8 = lax.broadcasted_iota(jnp.int32, (GATE_ROWS, L), 1)
    row8 = lax.broadcasted_iota(jnp.int32, (GATE_ROWS, L), 0)

    amax, gsum = [], []
    for c in range(nchunks):
        g = gates_ref[0, :, c * L:(c + 1) * L] + gbias_ref[...]
        cum = jnp.where(row8 >= M_HEADS, _log_sigmoid(g), 0.0)
        k = 1
        while k < L:
            cum = cum + jnp.where(lane8 >= k, pltpu.roll(cum, k, axis=1), 0.0)
            k *= 2
        b8 = pltpu.roll(cum, M_HEADS, axis=0) * LOG2E
        a8 = g * LOG2E - b8
        a_ref[:, c * L:(c + 1) * L] = a8
        b_ref[:, c * L:(c + 1) * L] = b8
        amax.append(jnp.broadcast_to(jnp.max(a8, axis=1, keepdims=True), (GATE_ROWS, L)))
        gsum.append(jnp.broadcast_to(jnp.max(jnp.where(lane8 == L - 1, b8, -jnp.inf), axis=1, keepdims=True),
                                     (GATE_ROWS, L)))
    m_prev = jnp.zeros((GATE_ROWS, L), f32)
    for c in range(nchunks):
        m_last = jnp.maximum(m_prev, amax[c])
        mp_ref[:, c * L:(c + 1) * L] = m_prev
        ml_ref[:, c * L:(c + 1) * L] = m_last
        m_prev = gsum[c] + m_last

    st_ref[...] = jnp.zeros(st_ref.shape, f32)

    t_idx = lax.broadcasted_iota(jnp.int32, (L, L), 0)
    s_idx = lax.broadcasted_iota(jnp.int32, (L, L), 1)
    tri = s_idx <= t_idx
    eye = s_idx == t_idx
    ones = jnp.ones((L, LANES), bf16)
    scale = M_DH ** -0.5
    heads = range(M_HEADS)
    cols = [slice(h * LANES, (h + 1) * LANES) for h in heads]

    def chunk(c, carry):
        r0 = pl.multiple_of(c * L, L)
        rows = pl.ds(r0, L)
        cu = [cu_ref[0, rows, cols[h]] for h in heads]
        q = [(jnp.dot(cu[h], wq_ref[h], preferred_element_type=f32) * scale).astype(bf16) for h in heads]
        kt = [lax.dot_general(wkt_ref[h], cu[h], _NT, preferred_element_type=f32) for h in heads]
        st = [st_ref[h] for h in heads]
        qk = [jnp.dot(q[h], kt[h].astype(bf16), preferred_element_type=f32) for h in heads]
        qs = [jnp.dot(q[h], st[h].astype(bf16), preferred_element_type=f32) for h in heads]
        smat, mmb, bb = [], [], []
        for h in heads:
            a_mat = jnp.broadcast_to(a_ref[h:h + 1, rows], (L, L))
            m1 = jnp.max(jnp.where(tri, a_mat, -jnp.inf), axis=1, keepdims=True)
            mm = jnp.maximum(mp_ref[h:h + 1, rows], m1)
            p = jnp.where(tri, jnp.exp2(a_mat - mm), 0.0)
            smat.append((qk[h] * p).astype(bf16))
            mmb.append(mm)
            b_mat = jnp.broadcast_to(b_ref[h:h + 1, rows], (L, L))
            bb.append(jnp.broadcast_to(jnp.sum(jnp.where(eye, b_mat, 0.0), axis=1, keepdims=True), (L, LANES)))
        v2 = [jnp.concatenate([v_ref[0, rows, cols[h]], ones], axis=1) for h in heads]
        sv = [jnp.dot(smat[h], v2[h], preferred_element_type=f32) for h in heads]
        upd = []
        for h in heads:
            ws = jnp.exp2(a_ref[h:h + 1, rows] - ml_ref[h:h + 1, rows])
            upd.append(jnp.dot((kt[h] * ws).astype(bf16), v2[h], preferred_element_type=f32))
        for h in heads:
            mp_row = mp_ref[h:h + 1, rows]
            w_inter = jnp.exp2(mp_row - mmb[h])
            num = w_inter * qs[h][:, :LANES] + sv[h][:, :LANES]
            den = w_inter * qs[h][:, LANES:] + sv[h][:, LANES:]
            hh = num / jnp.maximum(jnp.abs(den), jnp.exp2(-(bb[h] + mmb[h])))
            hh = hh * lax.rsqrt(jnp.sum(hh * hh, axis=1, keepdims=True) * (1.0 / M_DH) + EPS)
            y = hh * hg_ref[:, cols[h]] + skip_ref[:, cols[h]] * cu[h].astype(f32)
            y = y * _silu(z_ref[0, rows, cols[h]].astype(f32))
            o_ref[0, rows, cols[h]] = y.astype(o_ref.dtype)
            cs = jnp.exp2(mp_row - ml_ref[h:h + 1, rows])
            st_ref[h] = jnp.concatenate([cs, cs], axis=1) * st[h] + upd[h]
        return carry

    lax.fori_loop(0, nchunks, chunk, 0, unroll=2)


def _mlstm(proj, gates, gbias, wq, wkt, hn_g, skip):
    bsz, seq, _ = proj.shape
    kern = functools.partial(_mlstm_kernel, seq=seq)
    blk = lambda j: pl.BlockSpec((1, seq, M_WP), lambda b: (b, 0, j))
    full = lambda shape: pl.BlockSpec(shape, lambda b: (0,) * len(shape))
    rows = pltpu.VMEM((GATE_ROWS, seq), jnp.float32)
    return pl.pallas_call(
        kern,
        grid=(bsz,),
        in_specs=[blk(OFF_MU // M_WP), blk(OFF_MV // M_WP), blk(OFF_MZ // M_WP),
                  pl.BlockSpec((1, GATE_ROWS, seq), lambda b: (b, 0, 0)),
                  full((GATE_ROWS, M_CHUNK)), full((M_HEADS, LANES, LANES)), full((M_HEADS, LANES, LANES)),
                  full((1, M_WP)), full((1, M_WP))],
        out_specs=pl.BlockSpec((1, seq, M_WP), lambda b: (b, 0, 0)),
        out_shape=jax.ShapeDtypeStruct((bsz, seq, M_WP), jnp.bfloat16),
        scratch_shapes=[pltpu.VMEM((M_HEADS, LANES, 2 * LANES), jnp.float32), rows, rows, rows, rows],
        compiler_params=_params(1),
        name="mlstm",
    )(proj, proj, proj, gates, gbias, wq, wkt, hn_g, skip)


def _swa_consts():
    W = WINDOW
    ti = lax.broadcasted_iota(jnp.int32, (W, W), 0)
    si = lax.broadcasted_iota(jnp.int32, (W, W), 1)
    cur = si <= ti
    neg_rel = -jnp.where(cur, ti - si, W + ti - si).astype(jnp.float32)
    lane = lax.broadcasted_iota(jnp.int32, (W, LANES), 1)
    lane2 = lax.broadcasted_iota(jnp.int32, (2 * W, LANES), 1)
    return cur, neg_rel, (lane < A_DH, lane >= A_DH), (lane2 < A_DH, lane2 >= A_DH)


def _swa_block(consts, n, q_all, z, kc, kp, vc, vp, slope_ref, sink_ref):
    f32 = jnp.float32
    bf16 = jnp.bfloat16
    W = WINDOW
    cur, neg_rel, halves, halves2 = consts
    bias = jnp.where((cur.astype(jnp.int32) + n) > 0, neg_rel, -jnp.inf)
    kk = jnp.concatenate([kc, kp], axis=0)
    vv = jnp.concatenate([vc, vp], axis=0)
    vms = [jnp.where(hm, vv, jnp.zeros_like(vv)) for hm in halves2]
    q_all = q_all * jnp.asarray(A_DH ** -0.5, q_all.dtype)
    pairs = [(g, kv) for g in range(A_G) for kv in range(A_KV)]
    sc = []
    for g, kv in pairs:
        qg = q_all[:, g * LANES:(g + 1) * LANES]
        qm = jnp.where(halves[kv], qg, jnp.zeros_like(qg))
        sc.append(lax.dot_general(qm, kk, _NT, preferred_element_type=f32))
    pc = []
    for i, (g, kv) in enumerate(pairs):
        head = kv * A_G + g
        s2 = jnp.where(cur, sc[i][:, :W], sc[i][:, W:]) + slope_ref[head] * bias
        sink = sink_ref[head]
        mx = jnp.maximum(jnp.max(s2, axis=1, keepdims=True), sink)
        e = jnp.exp(s2 - mx)
        den = jnp.sum(e, axis=1, keepdims=True) + jnp.exp(sink - mx)
        probs = (e * (1.0 / den)).astype(bf16)
        zero = jnp.zeros_like(probs)
        pc.append(jnp.concatenate([jnp.where(cur, probs, zero), jnp.where(cur, zero, probs)], axis=1))
    outs = []
    for g in range(A_G):
        o = (jnp.dot(pc[2 * g], vms[0], preferred_element_type=f32)
             + jnp.dot(pc[2 * g + 1], vms[1], preferred_element_type=f32))
        outs.append((o * _silu(z[:, g * LANES:(g + 1) * LANES].astype(f32))).astype(bf16))
    return jnp.concatenate(outs, axis=1)


def _sb_kernel(q_ref, z_ref, k_ref, v_ref, o_ref, acc_ref, run_ref, zz_ref, a_ref, *, seq):
    f32 = jnp.float32
    bf16 = jnp.bfloat16
    T = SB_TILE
    lane = lax.broadcasted_iota(jnp.int32, (T, LANES), 1)
    halves = (lane < S_DH, lane >= S_DH)
    ti = lax.broadcasted_iota(jnp.int32, (T, T), 0)
    si = lax.broadcasted_iota(jnp.int32, (T, T), 1)
    strict = si < ti
    usum = jnp.where(ti > si, 1.0, 0.0).astype(bf16)
    heads = range(S_HEADS)
    cols = [slice((h // 2) * LANES, (h // 2 + 1) * LANES) for h in heads]

    def qblock(n, carry):
        r0 = pl.multiple_of(n * T, T)
        q = q_ref[0, pl.ds(r0, T), :]
        q = q * jnp.asarray(S_DH ** -0.5, q.dtype)
        qm = [jnp.where(halves[h % 2], q[:, cols[h]], jnp.zeros((T, LANES), q.dtype)) for h in heads]
        acc_ref[...] = jnp.zeros(acc_ref.shape, f32)
        run_ref[...] = jnp.zeros(run_ref.shape, f32)

        def score(j):
            k0 = pl.multiple_of(j * T, T)
            kb = k_ref[0, pl.ds(k0, T), :]
            for h in heads:
                zz_ref[h] = lax.dot_general(qm[h], kb[:, cols[h]], _NT, preferred_element_type=f32)

        def apply(j):
            k0 = pl.multiple_of(j * T, T)
            vb = v_ref[0, pl.ds(k0, T), :]
            for p in range(S_PAIRS):
                pv = None
                for h in (2 * p, 2 * p + 1):
                    vm = jnp.where(halves[h % 2], vb[:, cols[h]], jnp.zeros((T, LANES), vb.dtype))
                    d = jnp.dot(a_ref[h], vm, preferred_element_type=f32)
                    pv = d if pv is None else pv + d
                acc_ref[:, cols[2 * p]] += pv

        def weights(j_next, diag):
            ls, lk = [], []
            for h in heads:
                zz = zz_ref[h]
                soft = jnp.log(1.0 + jnp.exp2(jnp.abs(zz) * (-LOG2E)))
                ls.append(jnp.minimum(zz, 0.0) - soft)
                lkh = ls[h] - zz
                lk.append(jnp.where(strict, lkh, 0.0) if diag else lkh)
            suf = [jnp.dot(lk[h].astype(bf16), usum, preferred_element_type=f32) for h in heads]
            runs = [run_ref[h] for h in heads]
            new_runs = [runs[h] + jnp.sum(lk[h], axis=1, keepdims=True) for h in heads]
            live = jnp.max(functools.reduce(jnp.maximum, new_runs)) > SB_DEAD
            score(j_next)
            for h in heads:
                ah = jnp.exp2((ls[h] + suf[h] + jnp.concatenate([runs[h]] * (T // LANES), axis=1)) * LOG2E)
                a_ref[h] = (jnp.where(strict, ah, 0.0) if diag else ah).astype(bf16)
                run_ref[h] = new_runs[h]
            return live.astype(jnp.int32)

        score(n)
        live0 = weights(max(n - 1, 0), True)

        def cond(c):
            i, live = c
            return jnp.logical_and(i <= n, live > 0)

        def body(c):
            i, _ = c
            apply(n - i + 1)
            return i + 1, weights(jnp.maximum(n - i - 1, 0), False)

        i_end, _ = lax.while_loop(cond, body, (jnp.int32(1), live0))
        apply(n - i_end + 1)
        zg = z_ref[0, pl.ds(r0, T), :].astype(f32)
        o_ref[0, pl.ds(r0, T), :] = (acc_ref[...] * _silu(zg)).astype(o_ref.dtype)
        return carry

    for n in range(seq // T):
        qblock(n, 0)


def _stickbreak(proj):
    bsz, seq, _ = proj.shape
    kern = functools.partial(_sb_kernel, seq=seq)
    full = lambda off: pl.BlockSpec((1, seq, S_W), lambda b: (b, 0, off // S_W))
    return pl.pallas_call(
        kern,
        grid=(bsz,),
        in_specs=[full(OFF_SQ), full(OFF_SZ), full(OFF_SK), full(OFF_SV)],
        out_specs=pl.BlockSpec((1, seq, S_W), lambda b: (b, 0, 0)),
        out_shape=jax.ShapeDtypeStruct((bsz, seq, S_W), jnp.bfloat16),
        scratch_shapes=[pltpu.VMEM((SB_TILE, S_W), jnp.float32),
                        pltpu.VMEM((S_HEADS, SB_TILE, LANES), jnp.float32),
                        pltpu.VMEM((S_HEADS, SB_TILE, SB_TILE), jnp.float32),
                        pltpu.VMEM((S_HEADS, SB_TILE, SB_TILE), jnp.bfloat16)],
        compiler_params=_params(1),
        name="stickbreak",
    )(proj, proj, proj, proj)


def _outproj_kernel(x_ref, ym_ref, ya_ref, ys_ref, wm_ref, wa_ref, ws_ref, g_ref, gate_ref, o_ref, *, tm, sub):
    f32 = jnp.float32
    gain = gate_ref[...] * g_ref[...]
    for i in range(tm // sub):
        r = slice(i * sub, (i + 1) * sub)
        y = (jnp.dot(ym_ref[0, r, :], wm_ref[...], preferred_element_type=f32)
             + jnp.dot(ya_ref[0, r, :], wa_ref[...], preferred_element_type=f32)
             + jnp.dot(ys_ref[0, r, :], ws_ref[...], preferred_element_type=f32))
        o_ref[0, r, :] = x_ref[0, r, :] + y * lax.rsqrt(jnp.mean(y * y, axis=-1, keepdims=True) + EPS) * gain


def _outproj(x, ym, ya, ys, wm, wa, ws, g_post, mod4, layer):
    bsz, seq, d = x.shape
    tm = OUTPROJ_ROWS
    kern = functools.partial(_outproj_kernel, tm=tm, sub=OUTPROJ_SUB)
    row = lambda w: pl.BlockSpec((1, tm, w), lambda b, s: (b, s, 0))
    const = lambda shape: pl.BlockSpec(shape, lambda b, s: (0,) * len(shape), pipeline_mode=pl.Buffered(1))
    return pl.pallas_call(
        kern,
        grid=(bsz, seq // tm),
        in_specs=[row(d), row(M_WP), row(A_W), row(S_W),
                  const((M_WP, d)), const((A_W, d)), const((S_W, d)), const((1, d)),
                  pl.BlockSpec((None, None, 1, d), lambda b, s: (layer, b, 0, 2))],
        out_specs=row(d),
        out_shape=jax.ShapeDtypeStruct((bsz, seq, d), jnp.float32),
        compiler_params=_params(2),
        name="outproj",
    )(x, ym, ya, ys, wm, wa, ws, g_post, mod4)


def _pad_heads(w, axis):
    shape = list(w.shape)
    shape[axis:axis + 1] = [M_HEADS, M_DH]
    w = w.reshape(shape)
    pad = [(0, 0)] * w.ndim
    pad[axis + 1] = (0, LANES - M_DH)
    w = jnp.pad(w, pad)
    shape[axis:axis + 2] = [M_WP]
    return w.reshape(shape)


def _pair_heads(w, axis):
    shape = list(w.shape)
    shape[axis:axis + 1] = [A_KV, A_G, A_DH]
    w = jnp.swapaxes(w.reshape(shape), axis, axis + 1)
    shape[axis:axis + 3] = [A_W]
    return w.reshape(shape)


def _pack_w_in(w):
    o = np.cumsum([0, M_W, M_W, M_HEADS, M_HEADS, M_W, A_W, A_KVW, A_KVW, A_W, S_W, S_W, S_W, S_W])
    wt = w.astype(jnp.bfloat16).T
    seg = lambda i: wt[int(o[i]):int(o[i + 1])]
    packed_t = jnp.concatenate(
        [_pad_heads(seg(0), 0), _pad_heads(seg(1), 0), _pad_heads(seg(4), 0),
         _pair_heads(seg(5), 0), _pair_heads(seg(8), 0), seg(6), seg(7),
         seg(9), seg(10), seg(11), seg(12)], axis=0)
    gates_t = jnp.concatenate([seg(2), seg(3)], axis=0)
    return packed_t, gates_t


def _pack_w_out(w):
    w = w.astype(jnp.bfloat16)
    return _pad_heads(w[:M_W], 0), _pair_heads(w[M_W:M_W + A_W], 0), w[M_W + A_W:]


def _pad_qk(w):
    return jnp.pad(w, ((0, 0), (0, LANES - M_DH), (0, LANES - M_DH))).astype(jnp.bfloat16)


def kernel(x, c, w_mod, b_mod, g_pre, g_post, w_in, m_conv_w, m_conv_b, m_wq, m_wk, m_b_i, m_b_f,
           m_norm_g, m_skip, a_sinks, w_out):
    bsz = x.shape[0]
    mod4 = _modulation(c, w_mod, b_mod).reshape(DEPTH, bsz, 1, 3 * D_MODEL)
    slopes = jnp.asarray(2.0 ** (-8.0 * np.arange(1, A_HEADS + 1) / A_HEADS), dtype=jnp.float32)
    for l in range(DEPTH):
        w_pack, wg_t = _pack_w_in(w_in[l])
        wm, wa, ws = _pack_w_out(w_out[l])
        gbias = jnp.broadcast_to(jnp.concatenate([m_b_i[l], m_b_f[l]])[:, None], (2 * M_HEADS, M_CHUNK))
        proj, gates, ya = _inproj(x, mod4, l, g_pre[l][None], w_pack, wg_t,
                                  _pad_heads(m_conv_w[l], 1), _pad_heads(m_conv_b[l][None], 1), slopes, a_sinks[l])
        ym = _mlstm(proj, gates, gbias, _pad_qk(m_wq[l]), _pad_qk(jnp.swapaxes(m_wk[l], 1, 2)),
                    _pad_heads(m_norm_g[l][None], 1), _pad_heads(m_skip[l][None], 1))
        ys = _stickbreak(proj)
        x = _outproj(x, ym, ya, ys, wm, wa, ws, g_post[l][None], mod4, l)
    return x
```

```python
import functools

import jax
import jax.numpy as jnp
import numpy as np
from jax import lax
from jax.experimental import pallas as pl
from jax.experimental.pallas import tpu as pltpu

D_MODEL = 1024
DEPTH = 2
M_HEADS = 4
M_DH = 96
M_W = M_HEADS * M_DH
M_CONV = 4
A_HEADS = 6
A_KV = 2
A_G = A_HEADS // A_KV
A_DH = 64
A_W = A_HEADS * A_DH
A_KVW = A_KV * A_DH
WINDOW = 128
S_HEADS = 4
S_DH = 64
S_W = S_HEADS * S_DH
EPS = 1e-6

LANES = 128
SUBLANES = 8
GATE_ROWS = 2 * M_HEADS
M_CHUNK = 128
M_WP = M_HEADS * LANES
SB_TILE = 256
S_PAIRS = S_W // LANES
INPROJ_ROWS, INPROJ_SUB, INPROJ_COLS = 1024, 128, 512
OUTPROJ_ROWS, OUTPROJ_SUB = 2048, 256
SB_DEAD = -93.0

OFF_MU, OFF_MV, OFF_MZ = 0, M_WP, 2 * M_WP
OFF_AQ = 3 * M_WP
OFF_AZ = OFF_AQ + A_W
OFF_AK = OFF_AZ + A_W
OFF_AV = OFF_AK + A_KVW
OFF_SQ = OFF_AV + A_KVW
OFF_SK = OFF_SQ + S_W
OFF_SV = OFF_SK + S_W
OFF_SZ = OFF_SV + S_W
N_PACK = OFF_SZ + S_W

VMEM_LIMIT = 48 * 1024 * 1024

_NT = (((1,), (1,)), ((), ()))
LOG2E = 1.4426950408889634


def _log_sigmoid(x):
    return jnp.minimum(x, 0.0) - jnp.log(1.0 + jnp.exp(-jnp.abs(x)))


def _silu(x):
    u = 0.5 * x
    return u + u * jnp.tanh(u)


def _params(n_axes):
    return pltpu.CompilerParams(dimension_semantics=("arbitrary",) * n_axes, vmem_limit_bytes=VMEM_LIMIT)


def _mod_kernel(c_ref, w_ref, b_ref, o_ref):
    c_act = _silu(c_ref[...]).astype(jnp.bfloat16)
    o_ref[0] = jnp.dot(c_act, w_ref[0].astype(jnp.bfloat16), preferred_element_type=jnp.float32) + b_ref[0]


def _modulation(c, w_mod, b_mod):
    depth, d, n = w_mod.shape
    bsz = c.shape[0]
    tn = 1024
    return pl.pallas_call(
        _mod_kernel,
        grid=(depth, n // tn),
        in_specs=[pl.BlockSpec((bsz, d), lambda l, j: (0, 0)),
                  pl.BlockSpec((1, d, tn), lambda l, j: (l, 0, j)),
                  pl.BlockSpec((1, 1, tn), lambda l, j: (l, 0, j))],
        out_specs=pl.BlockSpec((1, bsz, tn), lambda l, j: (l, 0, j)),
        out_shape=jax.ShapeDtypeStruct((depth, bsz, n), jnp.float32),
        compiler_params=_params(2),
        name="modulation",
    )(c, w_mod, b_mod.reshape(depth, 1, n))


def _inproj_kernel(x_ref, shift_ref, scale_ref, g_ref, wt_ref, wg_ref, cw_ref, cb_ref, slope_ref, sink_ref,
                   proj_ref, gates_ref, ya_ref, conv_ref, w_ref, kv_ref, *, tm, sub, nchunk):
    s = pl.program_id(1)
    f32 = jnp.float32

    @pl.when((pl.program_id(0) == 0) & (s == 0))
    def _():
        for c0 in range(0, N_PACK, 2 * LANES):
            w_ref[:, c0:c0 + 2 * LANES] = wt_ref[c0:c0 + 2 * LANES, :].T

    @pl.when(s == 0)
    def _():
        conv_ref[0:SUBLANES, :] = jnp.zeros((SUBLANES, M_WP), f32)
        kv_ref[...] = jnp.zeros(kv_ref.shape, kv_ref.dtype)

    subs = [slice(i * sub, (i + 1) * sub) for i in range(tm // sub)]
    gain = g_ref[...] * (1.0 + scale_ref[...])
    hs = []
    for r in subs:
        x = x_ref[0, r, :]
        ms = jnp.mean(x * x, axis=-1, keepdims=True)
        hs.append((x * lax.rsqrt(ms + EPS) * gain + shift_ref[...]).astype(jnp.bfloat16))

    swa_consts = _swa_consts()
    for i, (r, h) in enumerate(zip(subs, hs)):
        gates_ref[0, :, r] = lax.dot_general(wg_ref[...], h, _NT, preferred_element_type=f32)

        base = SUBLANES + i * sub
        conv_ref[base:base + sub, :] = jnp.dot(h, w_ref[:, OFF_MU:OFF_MU + M_WP], preferred_element_type=f32)
        acc = cb_ref[...] + cw_ref[M_CONV - 1:M_CONV, :] * conv_ref[base:base + sub, :]
        for j in range(M_CONV - 1):
            lag = M_CONV - 1 - j
            acc = acc + cw_ref[j:j + 1, :] * conv_ref[base - lag:base - lag + sub, :]
        proj_ref[0, r, OFF_MU:OFF_MU + M_WP] = _silu(acc).astype(jnp.bfloat16)

        for c0 in range(OFF_MV, N_PACK, nchunk):
            proj_ref[0, r, c0:c0 + nchunk] = jnp.dot(
                h, w_ref[:, c0:c0 + nchunk], preferred_element_type=f32).astype(jnp.bfloat16)

        grab = lambda rows, off, w: proj_ref[0, rows, off:off + w]
        kp = kv_ref[0] if i == 0 else grab(subs[i - 1], OFF_AK, A_KVW)
        vp = kv_ref[1] if i == 0 else grab(subs[i - 1], OFF_AV, A_KVW)
        ya_ref[0, r, :] = _swa_block(swa_consts, s * (tm // sub) + i, grab(r, OFF_AQ, A_W), grab(r, OFF_AZ, A_W),
                                     grab(r, OFF_AK, A_KVW), kp, grab(r, OFF_AV, A_KVW), vp, slope_ref, sink_ref)

    conv_ref[0:SUBLANES, :] = conv_ref[tm:tm + SUBLANES, :]
    kv_ref[0] = proj_ref[0, subs[-1], OFF_AK:OFF_AK + A_KVW]
    kv_ref[1] = proj_ref[0, subs[-1], OFF_AV:OFF_AV + A_KVW]


def _inproj(x, mod4, layer, g_pre, w_pack, wg_t, conv_w, conv_b, slopes, sinks):
    bsz, seq, d = x.shape
    tm = INPROJ_ROWS
    assert INPROJ_SUB == WINDOW
    kern = functools.partial(_inproj_kernel, tm=tm, sub=INPROJ_SUB, nchunk=INPROJ_COLS)
    const = lambda shape: pl.BlockSpec(shape, lambda b, s: (0,) * len(shape), pipeline_mode=pl.Buffered(1))
    smem = pl.BlockSpec(memory_space=pltpu.SMEM)
    return pl.pallas_call(
        kern,
        grid=(bsz, seq // tm),
        in_specs=[pl.BlockSpec((1, tm, d), lambda b, s: (b, s, 0)),
                  pl.BlockSpec((None, None, 1, d), lambda b, s: (layer, b, 0, 0)),
                  pl.BlockSpec((None, None, 1, d), lambda b, s: (layer, b, 0, 1)),
                  const((1, d)), const((N_PACK, d)), const((GATE_ROWS, d)), const((M_CONV, M_WP)), const((1, M_WP)),
                  smem, smem],
        out_specs=[pl.BlockSpec((1, tm, N_PACK), lambda b, s: (b, s, 0)),
                   pl.BlockSpec((1, GATE_ROWS, tm), lambda b, s: (b, 0, s)),
                   pl.BlockSpec((1, tm, A_W), lambda b, s: (b, s, 0))],
        out_shape=[jax.ShapeDtypeStruct((bsz, seq, N_PACK), jnp.bfloat16),
                   jax.ShapeDtypeStruct((bsz, GATE_ROWS, seq), jnp.float32),
                   jax.ShapeDtypeStruct((bsz, seq, A_W), jnp.bfloat16)],
        scratch_shapes=[pltpu.VMEM((tm + SUBLANES, M_WP), jnp.float32), pltpu.VMEM((d, N_PACK), jnp.bfloat16),
                        pltpu.VMEM((2, WINDOW, A_KVW), jnp.bfloat16)],
        compiler_params=_params(2),
        name="inproj",
    )(x, mod4, mod4, g_pre, w_pack, wg_t, conv_w, conv_b, slopes, sinks)


def _mlstm_kernel(cu_ref, v_ref, z_ref, gates_ref, gbias_ref, wq_ref, wkt_ref, hg_ref, skip_ref,
                  o_ref, st_ref, a_ref, b_ref, mp_ref, ml_ref, *, seq):
    L = M_CHUNK
    nchunks = seq // L
    f32 = jnp.float32
    bf16 = jnp.bfloat16
    lane8 = lax.broadcasted_iota(jnp.int32, (GATE_ROWS, L), 1)
    row8 = lax.broadcasted_iota(jnp.int32, (GATE_ROWS, L), 0)

    amax, gsum = [], []
    for c in range(nchunks):
        g = gates_ref[0, :, c * L:(c + 1) * L] + gbias_ref[...]
        cum = jnp.where(row8 >= M_HEADS, _log_sigmoid(g), 0.0)
        k = 1
        while k < L:
            cum = cum + jnp.where(lane8 >= k, pltpu.roll(cum, k, axis=1), 0.0)
            k *= 2
        b8 = pltpu.roll(cum, M_HEADS, axis=0) * LOG2E
        a8 = g * LOG2E - b8
        a_ref[:, c * L:(c + 1) * L] = a8
        b_ref[:, c * L:(c + 1) * L] = b8
        amax.append(jnp.broadcast_to(jnp.max(a8, axis=1, keepdims=True), (GATE_ROWS, L)))
        gsum.append(jnp.broadcast_to(jnp.max(jnp.where(lane8 == L - 1, b8, -jnp.inf), axis=1, keepdims=True),
                                     (GATE_ROWS, L)))
    m_prev = jnp.zeros((GATE_ROWS, L), f32)
    for c in range(nchunks):
        m_last = jnp.maximum(m_prev, amax[c])
        mp_ref[:, c * L:(c + 1) * L] = m_prev
        ml_ref[:, c * L:(c + 1) * L] = m_last
        m_prev = gsum[c] + m_last

    st_ref[...] = jnp.zeros(st_ref.shape, f32)

    t_idx = lax.broadcasted_iota(jnp.int32, (L, L), 0)
    s_idx = lax.broadcasted_iota(jnp.int32, (L, L), 1)
    tri = s_idx <= t_idx
    eye = s_idx == t_idx
    ones = jnp.ones((L, LANES), bf16)
    scale = M_DH ** -0.5
    heads = range(M_HEADS)
    cols = [slice(h * LANES, (h + 1) * LANES) for h in heads]

    def chunk(c, carry):
        r0 = pl.multiple_of(c * L, L)
        rows = pl.ds(r0, L)
        cu = [cu_ref[0, rows, cols[h]] for h in heads]
        q = [(jnp.dot(cu[h], wq_ref[h], preferred_element_type=f32) * scale).astype(bf16) for h in heads]
        kt = [lax.dot_general(wkt_ref[h], cu[h], _NT, preferred_element_type=f32) for h in heads]
        st = [st_ref[h] for h in heads]
        qk = [jnp.dot(q[h], kt[h].astype(bf16), preferred_element_type=f32) for h in heads]
        qs = [jnp.dot(q[h], st[h].astype(bf16), preferred_element_type=f32) for h in heads]
        smat, mmb, bb = [], [], []
        for h in heads:
            a_mat = jnp.broadcast_to(a_ref[h:h + 1, rows], (L, L))
            m1 = jnp.max(jnp.where(tri, a_mat, -jnp.inf), axis=1, keepdims=True)
            mm = jnp.maximum(mp_ref[h:h + 1, rows], m1)
            p = jnp.where(tri, jnp.exp2(a_mat - mm), 0.0)
            smat.append((qk[h] * p).astype(bf16))
            mmb.append(mm)
            b_mat = jnp.broadcast_to(b_ref[h:h + 1, rows], (L, L))
            bb.append(jnp.broadcast_to(jnp.sum(jnp.where(eye, b_mat, 0.0), axis=1, keepdims=True), (L, LANES)))
        v2 = [jnp.concatenate([v_ref[0, rows, cols[h]], ones], axis=1) for h in heads]
        sv = [jnp.dot(smat[h], v2[h], preferred_element_type=f32) for h in heads]
        upd = []
        for h in heads:
            ws = jnp.exp2(a_ref[h:h + 1, rows] - ml_ref[h:h + 1, rows])
            upd.append(jnp.dot((kt[h] * ws).astype(bf16), v2[h], preferred_element_type=f32))
        for h in heads:
            mp_row = mp_ref[h:h + 1, rows]
            w_inter = jnp.exp2(mp_row - mmb[h])
            num = w_inter * qs[h][:, :LANES] + sv[h][:, :LANES]
            den = w_inter * qs[h][:, LANES:] + sv[h][:, LANES:]
            hh = num / jnp.maximum(jnp.abs(den), jnp.exp2(-(bb[h] + mmb[h])))
            hh = hh * lax.rsqrt(jnp.sum(hh * hh, axis=1, keepdims=True) * (1.0 / M_DH) + EPS)
            y = hh * hg_ref[:, cols[h]] + skip_ref[:, cols[h]] * cu[h].astype(f32)
            y = y * _silu(z_ref[0, rows, cols[h]].astype(f32))
            o_ref[0, rows, cols[h]] = y.astype(o_ref.dtype)
            cs = jnp.exp2(mp_row - ml_ref[h:h + 1, rows])
            st_ref[h] = jnp.concatenate([cs, cs], axis=1) * st[h] + upd[h]
        return carry

    lax.fori_loop(0, nchunks, chunk, 0, unroll=8)


def _mlstm(proj, gates, gbias, wq, wkt, hn_g, skip):
    bsz, seq, _ = proj.shape
    kern = functools.partial(_mlstm_kernel, seq=seq)
    blk = lambda j: pl.BlockSpec((1, seq, M_WP), lambda b: (b, 0, j))
    full = lambda shape: pl.BlockSpec(shape, lambda b: (0,) * len(shape))
    rows = pltpu.VMEM((GATE_ROWS, seq), jnp.float32)
    return pl.pallas_call(
        kern,
        grid=(bsz,),
        in_specs=[blk(OFF_MU // M_WP), blk(OFF_MV // M_WP), blk(OFF_MZ // M_WP),
                  pl.BlockSpec((1, GATE_ROWS, seq), lambda b: (b, 0, 0)),
                  full((GATE_ROWS, M_CHUNK)), full((M_HEADS, LANES, LANES)), full((M_HEADS, LANES, LANES)),
                  full((1, M_WP)), full((1, M_WP))],
        out_specs=pl.BlockSpec((1, seq, M_WP), lambda b: (b, 0, 0)),
        out_shape=jax.ShapeDtypeStruct((bsz, seq, M_WP), jnp.bfloat16),
        scratch_shapes=[pltpu.VMEM((M_HEADS, LANES, 2 * LANES), jnp.float32), rows, rows, rows, rows],
        compiler_params=_params(1),
        name="mlstm",
    )(proj, proj, proj, gates, gbias, wq, wkt, hn_g, skip)


def _swa_consts():
    W = WINDOW
    ti = lax.broadcasted_iota(jnp.int32, (W, W), 0)
    si = lax.broadcasted_iota(jnp.int32, (W, W), 1)
    cur = si <= ti
    neg_rel = -jnp.where(cur, ti - si, W + ti - si).astype(jnp.float32)
    lane = lax.broadcasted_iota(jnp.int32, (W, LANES), 1)
    lane2 = lax.broadcasted_iota(jnp.int32, (2 * W, LANES), 1)
    return cur, neg_rel, (lane < A_DH, lane >= A_DH), (lane2 < A_DH, lane2 >= A_DH)


def _swa_block(consts, n, q_all, z, kc, kp, vc, vp, slope_ref, sink_ref):
    f32 = jnp.float32
    bf16 = jnp.bfloat16
    W = WINDOW
    cur, neg_rel, halves, halves2 = consts
    bias = jnp.where((cur.astype(jnp.int32) + n) > 0, neg_rel, -jnp.inf)
    kk = jnp.concatenate([kc, kp], axis=0)
    vv = jnp.concatenate([vc, vp], axis=0)
    vms = [jnp.where(hm, vv, jnp.zeros_like(vv)) for hm in halves2]
    q_all = q_all * jnp.asarray(A_DH ** -0.5, q_all.dtype)
    pairs = [(g, kv) for g in range(A_G) for kv in range(A_KV)]
    sc = []
    for g, kv in pairs:
        qg = q_all[:, g * LANES:(g + 1) * LANES]
        qm = jnp.where(halves[kv], qg, jnp.zeros_like(qg))
        sc.append(lax.dot_general(qm, kk, _NT, preferred_element_type=f32))
    pc = []
    for i, (g, kv) in enumerate(pairs):
        head = kv * A_G + g
        s2 = jnp.where(cur, sc[i][:, :W], sc[i][:, W:]) + slope_ref[head] * bias
        sink = sink_ref[head]
        mx = jnp.maximum(jnp.max(s2, axis=1, keepdims=True), sink)
        e = jnp.exp(s2 - mx)
        den = jnp.sum(e, axis=1, keepdims=True) + jnp.exp(sink - mx)
        probs = (e * (1.0 / den)).astype(bf16)
        zero = jnp.zeros_like(probs)
        pc.append(jnp.concatenate([jnp.where(cur, probs, zero), jnp.where(cur, zero, probs)], axis=1))
    outs = []
    for g in range(A_G):
        o = (jnp.dot(pc[2 * g], vms[0], preferred_element_type=f32)
             + jnp.dot(pc[2 * g + 1], vms[1], preferred_element_type=f32))
        outs.append((o * _silu(z[:, g * LANES:(g + 1) * LANES].astype(f32))).astype(bf16))
    return jnp.concatenate(outs, axis=1)


def _sb_kernel(q_ref, z_ref, k_ref, v_ref, o_ref, acc_ref, run_ref, zz_ref, a_ref, *, seq):
    f32 = jnp.float32
    bf16 = jnp.bfloat16
    T = SB_TILE
    lane = lax.broadcasted_iota(jnp.int32, (T, LANES), 1)
    halves = (lane < S_DH, lane >= S_DH)
    ti = lax.broadcasted_iota(jnp.int32, (T, T), 0)
    si = lax.broadcasted_iota(jnp.int32, (T, T), 1)
    strict = si < ti
    usum = jnp.where(ti > si, 1.0, 0.0).astype(bf16)
    heads = range(S_HEADS)
    cols = [slice((h // 2) * LANES, (h // 2 + 1) * LANES) for h in heads]

    def qblock(n, carry):
        r0 = pl.multiple_of(n * T, T)
        q = q_ref[0, pl.ds(r0, T), :]
        q = q * jnp.asarray(S_DH ** -0.5, q.dtype)
        qm = [jnp.where(halves[h % 2], q[:, cols[h]], jnp.zeros((T, LANES), q.dtype)) for h in heads]
        acc_ref[...] = jnp.zeros(acc_ref.shape, f32)
        run_ref[...] = jnp.zeros(run_ref.shape, f32)

        def score(j):
            k0 = pl.multiple_of(j * T, T)
            kb = k_ref[0, pl.ds(k0, T), :]
            for h in heads:
                zz_ref[h] = lax.dot_general(qm[h], kb[:, cols[h]], _NT, preferred_element_type=f32)

        def apply(j):
            k0 = pl.multiple_of(j * T, T)
            vb = v_ref[0, pl.ds(k0, T), :]
            for p in range(S_PAIRS):
                pv = None
                for h in (2 * p, 2 * p + 1):
                    vm = jnp.where(halves[h % 2], vb[:, cols[h]], jnp.zeros((T, LANES), vb.dtype))
                    d = jnp.dot(a_ref[h], vm, preferred_element_type=f32)
                    pv = d if pv is None else pv + d
                acc_ref[:, cols[2 * p]] += pv

        def weights(j_next, diag):
            ls, lk = [], []
            for h in heads:
                zz = zz_ref[h]
                soft = jnp.log(1.0 + jnp.exp2(jnp.abs(zz) * (-LOG2E)))
                ls.append(jnp.minimum(zz, 0.0) - soft)
                lkh = ls[h] - zz
                lk.append(jnp.where(strict, lkh, 0.0) if diag else lkh)
            suf = [jnp.dot(lk[h].astype(bf16), usum, preferred_element_type=f32) for h in heads]
            runs = [run_ref[h] for h in heads]
            new_runs = [runs[h] + jnp.sum(lk[h], axis=1, keepdims=True) for h in heads]
            live = jnp.max(functools.reduce(jnp.maximum, new_runs)) > SB_DEAD
            score(j_next)
            for h in heads:
                ah = jnp.exp2((ls[h] + suf[h] + jnp.concatenate([runs[h]] * (T // LANES), axis=1)) * LOG2E)
                a_ref[h] = (jnp.where(strict, ah, 0.0) if diag else ah).astype(bf16)
                run_ref[h] = new_runs[h]
            return live.astype(jnp.int32)

        score(n)
        live0 = weights(max(n - 1, 0), True)

        def cond(c):
            i, live = c
            return jnp.logical_and(i <= n, live > 0)

        def body(c):
            i, _ = c
            apply(n - i + 1)
            return i + 1, weights(jnp.maximum(n - i - 1, 0), False)

        i_end, _ = lax.while_loop(cond, body, (jnp.int32(1), live0))
        apply(n - i_end + 1)
        zg = z_ref[0, pl.ds(r0, T), :].astype(f32)
        o_ref[0, pl.ds(r0, T), :] = (acc_ref[...] * _silu(zg)).astype(o_ref.dtype)
        return carry

    for n in range(seq // T):
        qblock(n, 0)


def _stickbreak(proj):
    bsz, seq, _ = proj.shape
    kern = functools.partial(_sb_kernel, seq=seq)
    full = lambda off: pl.BlockSpec((1, seq, S_W), lambda b: (b, 0, off // S_W))
    return pl.pallas_call(
        kern,
        grid=(bsz,),
        in_specs=[full(OFF_SQ), full(OFF_SZ), full(OFF_SK), full(OFF_SV)],
        out_specs=pl.BlockSpec((1, seq, S_W), lambda b: (b, 0, 0)),
        out_shape=jax.ShapeDtypeStruct((bsz, seq, S_W), jnp.bfloat16),
        scratch_shapes=[pltpu.VMEM((SB_TILE, S_W), jnp.float32),
                        pltpu.VMEM((S_HEADS, SB_TILE, LANES), jnp.float32),
                        pltpu.VMEM((S_HEADS, SB_TILE, SB_TILE), jnp.float32),
                        pltpu.VMEM((S_HEADS, SB_TILE, SB_TILE), jnp.bfloat16)],
        compiler_params=_params(1),
        name="stickbreak",
    )(proj, proj, proj, proj)


def _outproj_kernel(x_ref, ym_ref, ya_ref, ys_ref, wm_ref, wa_ref, ws_ref, g_ref, gate_ref, o_ref, *, tm, sub):
    f32 = jnp.float32
    gain = gate_ref[...] * g_ref[...]
    for i in range(tm // sub):
        r = slice(i * sub, (i + 1) * sub)
        y = (jnp.dot(ym_ref[0, r, :], wm_ref[...], preferred_element_type=f32)
             + jnp.dot(ya_ref[0, r, :], wa_ref[...], preferred_element_type=f32)
             + jnp.dot(ys_ref[0, r, :], ws_ref[...], preferred_element_type=f32))
        o_ref[0, r, :] = x_ref[0, r, :] + y * lax.rsqrt(jnp.mean(y * y, axis=-1, keepdims=True) + EPS) * gain


def _outproj(x, ym, ya, ys, wm, wa, ws, g_post, mod4, layer):
    bsz, seq, d = x.shape
    tm = OUTPROJ_ROWS
    kern = functools.partial(_outproj_kernel, tm=tm, sub=OUTPROJ_SUB)
    row = lambda w: pl.BlockSpec((1, tm, w), lambda b, s: (b, s, 0))
    const = lambda shape: pl.BlockSpec(shape, lambda b, s: (0,) * len(shape), pipeline_mode=pl.Buffered(1))
    return pl.pallas_call(
        kern,
        grid=(bsz, seq // tm),
        in_specs=[row(d), row(M_WP), row(A_W), row(S_W),
                  const((M_WP, d)), const((A_W, d)), const((S_W, d)), const((1, d)),
                  pl.BlockSpec((None, None, 1, d), lambda b, s: (layer, b, 0, 2))],
        out_specs=row(d),
        out_shape=jax.ShapeDtypeStruct((bsz, seq, d), jnp.float32),
        compiler_params=_params(2),
        name="outproj",
    )(x, ym, ya, ys, wm, wa, ws, g_post, mod4)


def _pad_heads(w, axis):
    shape = list(w.shape)
    shape[axis:axis + 1] = [M_HEADS, M_DH]
    w = w.reshape(shape)
    pad = [(0, 0)] * w.ndim
    pad[axis + 1] = (0, LANES - M_DH)
    w = jnp.pad(w, pad)
    shape[axis:axis + 2] = [M_WP]
    return w.reshape(shape)


def _pair_heads(w, axis):
    shape = list(w.shape)
    shape[axis:axis + 1] = [A_KV, A_G, A_DH]
    w = jnp.swapaxes(w.reshape(shape), axis, axis + 1)
    shape[axis:axis + 3] = [A_W]
    return w.reshape(shape)


def _pack_w_in(w):
    o = np.cumsum([0, M_W, M_W, M_HEADS, M_HEADS, M_W, A_W, A_KVW, A_KVW, A_W, S_W, S_W, S_W, S_W])
    wt = w.astype(jnp.bfloat16).T
    seg = lambda i: wt[int(o[i]):int(o[i + 1])]
    packed_t = jnp.concatenate(
        [_pad_heads(seg(0), 0), _pad_heads(seg(1), 0), _pad_heads(seg(4), 0),
         _pair_heads(seg(5), 0), _pair_heads(seg(8), 0), seg(6), seg(7),
         seg(9), seg(10), seg(11), seg(12)], axis=0)
    gates_t = jnp.concatenate([seg(2), seg(3)], axis=0)
    return packed_t, gates_t


def _pack_w_out(w):
    w = w.astype(jnp.bfloat16)
    return _pad_heads(w[:M_W], 0), _pair_heads(w[M_W:M_W + A_W], 0), w[M_W + A_W:]


def _pad_qk(w):
    return jnp.pad(w, ((0, 0), (0, LANES - M_DH), (0, LANES - M_DH))).astype(jnp.bfloat16)


def kernel(x, c, w_mod, b_mod, g_pre, g_post, w_in, m_conv_w, m_conv_b, m_wq, m_wk, m_b_i, m_b_f,
           m_norm_g, m_skip, a_sinks, w_out):
    bsz = x.shape[0]
    mod4 = _modulation(c, w_mod, b_mod).reshape(DEPTH, bsz, 1, 3 * D_MODEL)
    slopes = jnp.asarray(2.0 ** (-8.0 * np.arange(1, A_HEADS + 1) / A_HEADS), dtype=jnp.float32)
    for l in range(DEPTH):
        w_pack, wg_t = _pack_w_in(w_in[l])
        wm, wa, ws = _pack_w_out(w_out[l])
        gbias = jnp.broadcast_to(jnp.concatenate([m_b_i[l], m_b_f[l]])[:, None], (2 * M_HEADS, M_CHUNK))
        proj, gates, ya = _inproj(x, mod4, l, g_pre[l][None], w_pack, wg_t,
                                  _pad_heads(m_conv_w[l], 1), _pad_heads(m_conv_b[l][None], 1), slopes, a_sinks[l])
        ym = _mlstm(proj, gates, gbias, _pad_qk(m_wq[l]), _pad_qk(jnp.swapaxes(m_wk[l], 1, 2)),
                    _pad_heads(m_norm_g[l][None], 1), _pad_heads(m_skip[l][None], 1))
        ys = _stickbreak(proj)
        x = _outproj(x, ym, ya, ys, wm, wa, ws, g_post[l][None], mod4, l)
    return x
```

```python
import functools

import jax
import jax.numpy as jnp
import numpy as np
from jax import lax
from jax.experimental import pallas as pl
from jax.experimental.pallas import tpu as pltpu

D_MODEL = 1024
DEPTH = 2
M_HEADS = 4
M_DH = 96
M_W = M_HEADS * M_DH
M_CONV = 4
A_HEADS = 6
A_KV = 2
A_G = A_HEADS // A_KV
A_DH = 64
A_W = A_HEADS * A_DH
A_KVW = A_KV * A_DH
WINDOW = 128
S_HEADS = 4
S_DH = 64
S_W = S_HEADS * S_DH
EPS = 1e-6

LANES = 128
SUBLANES = 8
GATE_ROWS = 2 * M_HEADS
M_CHUNK = 128
M_WP = M_HEADS * LANES
SB_TILE = 256
S_PAIRS = S_W // LANES
INPROJ_ROWS, INPROJ_SUB, INPROJ_COLS = 1024, 128, 512
OUTPROJ_ROWS, OUTPROJ_SUB = 2048, 256
SB_DEAD = -93.0

OFF_MU, OFF_MV, OFF_MZ = 0, M_WP, 2 * M_WP
OFF_AQ = 3 * M_WP
OFF_AZ = OFF_AQ + A_W
OFF_AK = OFF_AZ + A_W
OFF_AV = OFF_AK + A_KVW
OFF_SQ = OFF_AV + A_KVW
OFF_SK = OFF_SQ + S_W
OFF_SV = OFF_SK + S_W
OFF_SZ = OFF_SV + S_W
N_PACK = OFF_SZ + S_W

VMEM_LIMIT = 48 * 1024 * 1024

_NT = (((1,), (1,)), ((), ()))
LOG2E = 1.4426950408889634


def _log_sigmoid(x):
    return jnp.minimum(x, 0.0) - jnp.log(1.0 + jnp.exp(-jnp.abs(x)))


def _silu(x):
    u = 0.5 * x
    return u + u * jnp.tanh(u)


def _params(n_axes):
    return pltpu.CompilerParams(dimension_semantics=("arbitrary",) * n_axes, vmem_limit_bytes=VMEM_LIMIT)


def _mod_kernel(c_ref, w_ref, b_ref, o_ref):
    c_act = _silu(c_ref[...]).astype(jnp.bfloat16)
    o_ref[0] = jnp.dot(c_act, w_ref[0].astype(jnp.bfloat16), preferred_element_type=jnp.float32) + b_ref[0]


def _modulation(c, w_mod, b_mod):
    depth, d, n = w_mod.shape
    bsz = c.shape[0]
    tn = 1024
    return pl.pallas_call(
        _mod_kernel,
        grid=(depth, n // tn),
        in_specs=[pl.BlockSpec((bsz, d), lambda l, j: (0, 0)),
                  pl.BlockSpec((1, d, tn), lambda l, j: (l, 0, j)),
                  pl.BlockSpec((1, 1, tn), lambda l, j: (l, 0, j))],
        out_specs=pl.BlockSpec((1, bsz, tn), lambda l, j: (l, 0, j)),
        out_shape=jax.ShapeDtypeStruct((depth, bsz, n), jnp.float32),
        compiler_params=_params(2),
        name="modulation",
    )(c, w_mod, b_mod.reshape(depth, 1, n))


def _inproj_kernel(x_ref, shift_ref, scale_ref, g_ref, wt_ref, wg_ref, cw_ref, cb_ref, slope_ref, sink_ref,
                   proj_ref, gates_ref, ya_ref, conv_ref, w_ref, kv_ref, *, tm, sub, nchunk):
    s = pl.program_id(1)
    f32 = jnp.float32

    @pl.when((pl.program_id(0) == 0) & (s == 0))
    def _():
        for c0 in range(0, N_PACK, 2 * LANES):
            w_ref[:, c0:c0 + 2 * LANES] = wt_ref[c0:c0 + 2 * LANES, :].T

    @pl.when(s == 0)
    def _():
        conv_ref[0:SUBLANES, :] = jnp.zeros((SUBLANES, M_WP), f32)
        kv_ref[...] = jnp.zeros(kv_ref.shape, kv_ref.dtype)

    subs = [slice(i * sub, (i + 1) * sub) for i in range(tm // sub)]
    gain = g_ref[...] * (1.0 + scale_ref[...])
    hs = []
    for r in subs:
        x = x_ref[0, r, :]
        ms = jnp.mean(x * x, axis=-1, keepdims=True)
        hs.append((x * lax.rsqrt(ms + EPS) * gain + shift_ref[...]).astype(jnp.bfloat16))

    swa_consts = _swa_consts()
    for i, (r, h) in enumerate(zip(subs, hs)):
        gates_ref[0, :, r] = lax.dot_general(wg_ref[...], h, _NT, preferred_element_type=f32)

        base = SUBLANES + i * sub
        conv_ref[base:base + sub, :] = jnp.dot(h, w_ref[:, OFF_MU:OFF_MU + M_WP], preferred_element_type=f32)
        acc = cb_ref[...] + cw_ref[M_CONV - 1:M_CONV, :] * conv_ref[base:base + sub, :]
        for j in range(M_CONV - 1):
            lag = M_CONV - 1 - j
            acc = acc + cw_ref[j:j + 1, :] * conv_ref[base - lag:base - lag + sub, :]
        proj_ref[0, r, OFF_MU:OFF_MU + M_WP] = _silu(acc).astype(jnp.bfloat16)

        for c0 in range(OFF_MV, N_PACK, nchunk):
            proj_ref[0, r, c0:c0 + nchunk] = jnp.dot(
                h, w_ref[:, c0:c0 + nchunk], preferred_element_type=f32).astype(jnp.bfloat16)

        grab = lambda rows, off, w: proj_ref[0, rows, off:off + w]
        kp = kv_ref[0] if i == 0 else grab(subs[i - 1], OFF_AK, A_KVW)
        vp = kv_ref[1] if i == 0 else grab(subs[i - 1], OFF_AV, A_KVW)
        ya_ref[0, r, :] = _swa_block(swa_consts, s * (tm // sub) + i, grab(r, OFF_AQ, A_W), grab(r, OFF_AZ, A_W),
                                     grab(r, OFF_AK, A_KVW), kp, grab(r, OFF_AV, A_KVW), vp, slope_ref, sink_ref)

    conv_ref[0:SUBLANES, :] = conv_ref[tm:tm + SUBLANES, :]
    kv_ref[0] = proj_ref[0, subs[-1], OFF_AK:OFF_AK + A_KVW]
    kv_ref[1] = proj_ref[0, subs[-1], OFF_AV:OFF_AV + A_KVW]


def _inproj(x, mod4, layer, g_pre, w_pack, wg_t, conv_w, conv_b, slopes, sinks):
    bsz, seq, d = x.shape
    tm = INPROJ_ROWS
    assert INPROJ_SUB == WINDOW
    kern = functools.partial(_inproj_kernel, tm=tm, sub=INPROJ_SUB, nchunk=INPROJ_COLS)
    const = lambda shape: pl.BlockSpec(shape, lambda b, s: (0,) * len(shape), pipeline_mode=pl.Buffered(1))
    smem = pl.BlockSpec(memory_space=pltpu.SMEM)
    return pl.pallas_call(
        kern,
        grid=(bsz, seq // tm),
        in_specs=[pl.BlockSpec((1, tm, d), lambda b, s: (b, s, 0)),
                  pl.BlockSpec((None, None, 1, d), lambda b, s: (layer, b, 0, 0)),
                  pl.BlockSpec((None, None, 1, d), lambda b, s: (layer, b, 0, 1)),
                  const((1, d)), const((N_PACK, d)), const((GATE_ROWS, d)), const((M_CONV, M_WP)), const((1, M_WP)),
                  smem, smem],
        out_specs=[pl.BlockSpec((1, tm, N_PACK), lambda b, s: (b, s, 0)),
                   pl.BlockSpec((1, GATE_ROWS, tm), lambda b, s: (b, 0, s)),
                   pl.BlockSpec((1, tm, A_W), lambda b, s: (b, s, 0))],
        out_shape=[jax.ShapeDtypeStruct((bsz, seq, N_PACK), jnp.bfloat16),
                   jax.ShapeDtypeStruct((bsz, GATE_ROWS, seq), jnp.float32),
                   jax.ShapeDtypeStruct((bsz, seq, A_W), jnp.bfloat16)],
        scratch_shapes=[pltpu.VMEM((tm + SUBLANES, M_WP), jnp.float32), pltpu.VMEM((d, N_PACK), jnp.bfloat16),
                        pltpu.VMEM((2, WINDOW, A_KVW), jnp.bfloat16)],
        compiler_params=_params(2),
        name="inproj",
    )(x, mod4, mod4, g_pre, w_pack, wg_t, conv_w, conv_b, slopes, sinks)


def _mlstm_kernel(cu_ref, v_ref, z_ref, gates_ref, gbias_ref, wq_ref, wkt_ref, hg_ref, skip_ref,
                  o_ref, st_ref, a_ref, b_ref, mp_ref, ml_ref, *, seq):
    L = M_CHUNK
    nchunks = seq // L
    f32 = jnp.float32
    bf16 = jnp.bfloat16
    lane8 = lax.broadcasted_iota(jnp.int32, (GATE_ROWS, L), 1)
    row8 = lax.broadcasted_iota(jnp.int32, (GATE_ROWS, L), 0)

    amax, gsum = [], []
    for c in range(nchunks):
        g = gates_ref[0, :, c * L:(c + 1) * L] + gbias_ref[...]
        cum = jnp.where(row8 >= M_HEADS, _log_sigmoid(g), 0.0)
        k = 1
        while k < L:
            cum = cum + jnp.where(lane8 >= k, pltpu.roll(cum, k, axis=1), 0.0)
            k *= 2
        b8 = pltpu.roll(cum, M_HEADS, axis=0) * LOG2E
        a8 = g * LOG2E - b8
        a_ref[:, c * L:(c + 1) * L] = a8
        b_ref[:, c * L:(c + 1) * L] = b8
        amax.append(jnp.broadcast_to(jnp.max(a8, axis=1, keepdims=True), (GATE_ROWS, L)))
        gsum.append(jnp.broadcast_to(jnp.max(jnp.where(lane8 == L - 1, b8, -jnp.inf), axis=1, keepdims=True),
                                     (GATE_ROWS, L)))
    m_prev = jnp.zeros((GATE_ROWS, L), f32)
    for c in range(nchunks):
        m_last = jnp.maximum(m_prev, amax[c])
        mp_ref[:, c * L:(c + 1) * L] = m_prev
        ml_ref[:, c * L:(c + 1) * L] = m_last
        m_prev = gsum[c] + m_last

    st_ref[...] = jnp.zeros(st_ref.shape, f32)

    t_idx = lax.broadcasted_iota(jnp.int32, (L, L), 0)
    s_idx = lax.broadcasted_iota(jnp.int32, (L, L), 1)
    tri = s_idx <= t_idx
    eye = s_idx == t_idx
    ones = jnp.ones((L, LANES), bf16)
    scale = M_DH ** -0.5
    heads = range(M_HEADS)
    cols = [slice(h * LANES, (h + 1) * LANES) for h in heads]

    def chunk(c, carry):
        r0 = pl.multiple_of(c * L, L)
        rows = pl.ds(r0, L)
        cu = [cu_ref[0, rows, cols[h]] for h in heads]
        q = [(jnp.dot(cu[h], wq_ref[h], preferred_element_type=f32) * scale).astype(bf16) for h in heads]
        kt = [lax.dot_general(wkt_ref[h], cu[h], _NT, preferred_element_type=f32) for h in heads]
        st = [st_ref[h] for h in heads]
        qk = [jnp.dot(q[h], kt[h].astype(bf16), preferred_element_type=f32) for h in heads]
        qs = [jnp.dot(q[h], st[h].astype(bf16), preferred_element_type=f32) for h in heads]
        smat, mmb, bb = [], [], []
        for h in heads:
            a_mat = jnp.broadcast_to(a_ref[h:h + 1, rows], (L, L))
            m1 = jnp.max(jnp.where(tri, a_mat, -jnp.inf), axis=1, keepdims=True)
            mm = jnp.maximum(mp_ref[h:h + 1, rows], m1)
            p = jnp.where(tri, jnp.exp2(a_mat - mm), 0.0)
            smat.append((qk[h] * p).astype(bf16))
            mmb.append(mm)
            b_mat = jnp.broadcast_to(b_ref[h:h + 1, rows], (L, L))
            bb.append(jnp.broadcast_to(jnp.sum(jnp.where(eye, b_mat, 0.0), axis=1, keepdims=True), (L, LANES)))
        v2 = [jnp.concatenate([v_ref[0, rows, cols[h]], ones], axis=1) for h in heads]
        sv = [jnp.dot(smat[h], v2[h], preferred_element_type=f32) for h in heads]
        upd = []
        for h in heads:
            ws = jnp.exp2(a_ref[h:h + 1, rows] - ml_ref[h:h + 1, rows])
            upd.append(jnp.dot((kt[h] * ws).astype(bf16), v2[h], preferred_element_type=f32))
        for h in heads:
            mp_row = mp_ref[h:h + 1, rows]
            w_inter = jnp.exp2(mp_row - mmb[h])
            num = w_inter * qs[h][:, :LANES] + sv[h][:, :LANES]
            den = w_inter * qs[h][:, LANES:] + sv[h][:, LANES:]
            hh = num / jnp.maximum(jnp.abs(den), jnp.exp2(-(bb[h] + mmb[h])))
            hh = hh * lax.rsqrt(jnp.sum(hh * hh, axis=1, keepdims=True) * (1.0 / M_DH) + EPS)
            y = hh * hg_ref[:, cols[h]] + skip_ref[:, cols[h]] * cu[h].astype(f32)
            y = y * _silu(z_ref[0, rows, cols[h]].astype(f32))
            o_ref[0, rows, cols[h]] = y.astype(o_ref.dtype)
            cs = jnp.exp2(mp_row - ml_ref[h:h + 1, rows])
            st_ref[h] = jnp.concatenate([cs, cs], axis=1) * st[h] + upd[h]
        return carry

    lax.fori_loop(0, nchunks, chunk, 0, unroll=8)


def _mlstm(proj, gates, gbias, wq, wkt, hn_g, skip):
    bsz, seq, _ = proj.shape
    kern = functools.partial(_mlstm_kernel, seq=seq)
    blk = lambda j: pl.BlockSpec((1, seq, M_WP), lambda b: (b, 0, j))
    full = lambda shape: pl.BlockSpec(shape, lambda b: (0,) * len(shape))
    rows = pltpu.VMEM((GATE_ROWS, seq), jnp.float32)
    return pl.pallas_call(
        kern,
        grid=(bsz,),
        in_specs=[blk(OFF_MU // M_WP), blk(OFF_MV // M_WP), blk(OFF_MZ // M_WP),
                  pl.BlockSpec((1, GATE_ROWS, seq), lambda b: (b, 0, 0)),
                  full((GATE_ROWS, M_CHUNK)), full((M_HEADS, LANES, LANES)), full((M_HEADS, LANES, LANES)),
                  full((1, M_WP)), full((1, M_WP))],
        out_specs=pl.BlockSpec((1, seq, M_WP), lambda b: (b, 0, 0)),
        out_shape=jax.ShapeDtypeStruct((bsz, seq, M_WP), jnp.bfloat16),
        scratch_shapes=[pltpu.VMEM((M_HEADS, LANES, 2 * LANES), jnp.float32), rows, rows, rows, rows],
        compiler_params=_params(1),
        name="mlstm",
    )(proj, proj, proj, gates, gbias, wq, wkt, hn_g, skip)


def _swa_consts():
    W = WINDOW
    ti = lax.broadcasted_iota(jnp.int32, (W, W), 0)
    si = lax.broadcasted_iota(jnp.int32, (W, W), 1)
    cur = si <= ti
    neg_rel = -jnp.where(cur, ti - si, W + ti - si).astype(jnp.float32)
    lane = lax.broadcasted_iota(jnp.int32, (W, LANES), 1)
    lane2 = lax.broadcasted_iota(jnp.int32, (2 * W, LANES), 1)
    return cur, neg_rel, (lane < A_DH, lane >= A_DH), (lane2 < A_DH, lane2 >= A_DH)


def _swa_block(consts, n, q_all, z, kc, kp, vc, vp, slope_ref, sink_ref):
    f32 = jnp.float32
    bf16 = jnp.bfloat16
    W = WINDOW
    cur, neg_rel, halves, halves2 = consts
    bias = jnp.where((cur.astype(jnp.int32) + n) > 0, neg_rel, -jnp.inf)
    kk = jnp.concatenate([kc, kp], axis=0)
    vv = jnp.concatenate([vc, vp], axis=0)
    vms = [jnp.where(hm, vv, jnp.zeros_like(vv)) for hm in halves2]
    q_all = q_all * jnp.asarray(A_DH ** -0.5, q_all.dtype)
    pairs = [(g, kv) for g in range(A_G) for kv in range(A_KV)]
    sc = []
    for g, kv in pairs:
        qg = q_all[:, g * LANES:(g + 1) * LANES]
        qm = jnp.where(halves[kv], qg, jnp.zeros_like(qg))
        sc.append(lax.dot_general(qm, kk, _NT, preferred_element_type=f32))
    pc = []
    for i, (g, kv) in enumerate(pairs):
        head = kv * A_G + g
        s2 = jnp.where(cur, sc[i][:, :W], sc[i][:, W:]) + slope_ref[head] * bias
        sink = sink_ref[head]
        mx = jnp.maximum(jnp.max(s2, axis=1, keepdims=True), sink)
        e = jnp.exp(s2 - mx)
        den = jnp.sum(e, axis=1, keepdims=True) + jnp.exp(sink - mx)
        probs = (e * (1.0 / den)).astype(bf16)
        zero = jnp.zeros_like(probs)
        pc.append(jnp.concatenate([jnp.where(cur, probs, zero), jnp.where(cur, zero, probs)], axis=1))
    outs = []
    for g in range(A_G):
        o = (jnp.dot(pc[2 * g], vms[0], preferred_element_type=f32)
             + jnp.dot(pc[2 * g + 1], vms[1], preferred_element_type=f32))
        outs.append((o * _silu(z[:, g * LANES:(g + 1) * LANES].astype(f32))).astype(bf16))
    return jnp.concatenate(outs, axis=1)


def _sb_kernel(q_ref, z_ref, k_ref, v_ref, o_ref, acc_ref, run_ref, zz_ref, a_ref, *, seq):
    f32 = jnp.float32
    bf16 = jnp.bfloat16
    T = SB_TILE
    lane = lax.broadcasted_iota(jnp.int32, (T, LANES), 1)
    halves = (lane < S_DH, lane >= S_DH)
    ti = lax.broadcasted_iota(jnp.int32, (T, T), 0)
    si = lax.broadcasted_iota(jnp.int32, (T, T), 1)
    strict = si < ti
    usum = jnp.where(ti > si, 1.0, 0.0).astype(bf16)
    heads = range(S_HEADS)
    cols = [slice((h // 2) * LANES, (h // 2 + 1) * LANES) for h in heads]

    def qblock(n, carry):
        r0 = pl.multiple_of(n * T, T)
        q = q_ref[0, pl.ds(r0, T), :]
        q = q * jnp.asarray(S_DH ** -0.5, q.dtype)
        qm = [jnp.where(halves[h % 2], q[:, cols[h]], jnp.zeros((T, LANES), q.dtype)) for h in heads]
        acc_ref[...] = jnp.zeros(acc_ref.shape, f32)
        run_ref[...] = jnp.zeros(run_ref.shape, f32)

        def score(j):
            k0 = pl.multiple_of(j * T, T)
            kb = k_ref[0, pl.ds(k0, T), :]
            for h in heads:
                zz_ref[h] = lax.dot_general(qm[h], kb[:, cols[h]], _NT, preferred_element_type=f32)

        def apply(j):
            k0 = pl.multiple_of(j * T, T)
            vb = v_ref[0, pl.ds(k0, T), :]
            for p in range(S_PAIRS):
                pv = None
                for h in (2 * p, 2 * p + 1):
                    vm = jnp.where(halves[h % 2], vb[:, cols[h]], jnp.zeros((T, LANES), vb.dtype))
                    d = jnp.dot(a_ref[h], vm, preferred_element_type=f32)
                    pv = d if pv is None else pv + d
                acc_ref[:, cols[2 * p]] += pv

        def weights(j_next, diag):
            ls, lk = [], []
            for h in heads:
                zz = zz_ref[h]
                soft = jnp.log(1.0 + jnp.exp2(jnp.abs(zz) * (-LOG2E)))
                ls.append(jnp.minimum(zz, 0.0) - soft)
                lkh = ls[h] - zz
                lk.append(jnp.where(strict, lkh, 0.0) if diag else lkh)
            suf = [jnp.dot(lk[h].astype(bf16), usum, preferred_element_type=f32) for h in heads]
            runs = [run_ref[h] for h in heads]
            new_runs = [runs[h] + jnp.sum(lk[h], axis=1, keepdims=True) for h in heads]
            live = jnp.max(functools.reduce(jnp.maximum, new_runs)) > SB_DEAD
            score(j_next)
            for h in heads:
                ah = jnp.exp2((ls[h] + suf[h] + jnp.concatenate([runs[h]] * (T // LANES), axis=1)) * LOG2E)
                a_ref[h] = (jnp.where(strict, ah, 0.0) if diag else ah).astype(bf16)
                run_ref[h] = new_runs[h]
            return live.astype(jnp.int32)

        score(n)
        live0 = weights(max(n - 1, 0), True)
        first = 1
        if n >= 1:
            apply(n)
            live0 = weights(max(n - 2, 0), False)
            first = 2

        def cond(c):
            i, live = c
            return jnp.logical_and(i <= n, live > 0)

        def body(c):
            i, _ = c
            apply(n - i + 1)
            return i + 1, weights(jnp.maximum(n - i - 1, 0), False)

        i_end, _ = lax.while_loop(cond, body, (jnp.int32(first), live0))
        apply(n - i_end + 1)
        zg = z_ref[0, pl.ds(r0, T), :].astype(f32)
        o_ref[0, pl.ds(r0, T), :] = (acc_ref[...] * _silu(zg)).astype(o_ref.dtype)
        return carry

    for n in range(seq // T):
        qblock(n, 0)


def _stickbreak(proj):
    bsz, seq, _ = proj.shape
    kern = functools.partial(_sb_kernel, seq=seq)
    full = lambda off: pl.BlockSpec((1, seq, S_W), lambda b: (b, 0, off // S_W))
    return pl.pallas_call(
        kern,
        grid=(bsz,),
        in_specs=[full(OFF_SQ), full(OFF_SZ), full(OFF_SK), full(OFF_SV)],
        out_specs=pl.BlockSpec((1, seq, S_W), lambda b: (b, 0, 0)),
        out_shape=jax.ShapeDtypeStruct((bsz, seq, S_W), jnp.bfloat16),
        scratch_shapes=[pltpu.VMEM((SB_TILE, S_W), jnp.float32),
                        pltpu.VMEM((S_HEADS, SB_TILE, LANES), jnp.float32),
                        pltpu.VMEM((S_HEADS, SB_TILE, SB_TILE), jnp.float32),
                        pltpu.VMEM((S_HEADS, SB_TILE, SB_TILE), jnp.bfloat16)],
        compiler_params=_params(1),
        name="stickbreak",
    )(proj, proj, proj, proj)


def _outproj_kernel(x_ref, ym_ref, ya_ref, ys_ref, wm_ref, wa_ref, ws_ref, g_ref, gate_ref, o_ref, *, tm, sub):
    f32 = jnp.float32
    gain = gate_ref[...] * g_ref[...]
    for i in range(tm // sub):
        r = slice(i * sub, (i + 1) * sub)
        y = (jnp.dot(ym_ref[0, r, :], wm_ref[...], preferred_element_type=f32)
             + jnp.dot(ya_ref[0, r, :], wa_ref[...], preferred_element_type=f32)
             + jnp.dot(ys_ref[0, r, :], ws_ref[...], preferred_element_type=f32))
        o_ref[0, r, :] = x_ref[0, r, :] + y * lax.rsqrt(jnp.mean(y * y, axis=-1, keepdims=True) + EPS) * gain


def _outproj(x, ym, ya, ys, wm, wa, ws, g_post, mod4, layer):
    bsz, seq, d = x.shape
    tm = OUTPROJ_ROWS
    kern = functools.partial(_outproj_kernel, tm=tm, sub=OUTPROJ_SUB)
    row = lambda w: pl.BlockSpec((1, tm, w), lambda b, s: (b, s, 0))
    const = lambda shape: pl.BlockSpec(shape, lambda b, s: (0,) * len(shape), pipeline_mode=pl.Buffered(1))
    return pl.pallas_call(
        kern,
        grid=(bsz, seq // tm),
        in_specs=[row(d), row(M_WP), row(A_W), row(S_W),
                  const((M_WP, d)), const((A_W, d)), const((S_W, d)), const((1, d)),
                  pl.BlockSpec((None, None, 1, d), lambda b, s: (layer, b, 0, 2))],
        out_specs=row(d),
        out_shape=jax.ShapeDtypeStruct((bsz, seq, d), jnp.float32),
        compiler_params=_params(2),
        name="outproj",
    )(x, ym, ya, ys, wm, wa, ws, g_post, mod4)


def _pad_heads(w, axis):
    shape = list(w.shape)
    shape[axis:axis + 1] = [M_HEADS, M_DH]
    w = w.reshape(shape)
    pad = [(0, 0)] * w.ndim
    pad[axis + 1] = (0, LANES - M_DH)
    w = jnp.pad(w, pad)
    shape[axis:axis + 2] = [M_WP]
    return w.reshape(shape)


def _pair_heads(w, axis):
    shape = list(w.shape)
    shape[axis:axis + 1] = [A_KV, A_G, A_DH]
    w = jnp.swapaxes(w.reshape(shape), axis, axis + 1)
    shape[axis:axis + 3] = [A_W]
    return w.reshape(shape)


def _pack_w_in(w):
    o = np.cumsum([0, M_W, M_W, M_HEADS, M_HEADS, M_W, A_W, A_KVW, A_KVW, A_W, S_W, S_W, S_W, S_W])
    wt = w.astype(jnp.bfloat16).T
    seg = lambda i: wt[int(o[i]):int(o[i + 1])]
    packed_t = jnp.concatenate(
        [_pad_heads(seg(0), 0), _pad_heads(seg(1), 0), _pad_heads(seg(4), 0),
         _pair_heads(seg(5), 0), _pair_heads(seg(8), 0), seg(6), seg(7),
         seg(9), seg(10), seg(11), seg(12)], axis=0)
    gates_t = jnp.concatenate([seg(2), seg(3)], axis=0)
    return packed_t, gates_t


def _pack_w_out(w):
    w = w.astype(jnp.bfloat16)
    return _pad_heads(w[:M_W], 0), _pair_heads(w[M_W:M_W + A_W], 0), w[M_W + A_W:]


def _pad_qk(w):
    return jnp.pad(w, ((0, 0), (0, LANES - M_DH), (0, LANES - M_DH))).astype(jnp.bfloat16)


def kernel(x, c, w_mod, b_mod, g_pre, g_post, w_in, m_conv_w, m_conv_b, m_wq, m_wk, m_b_i, m_b_f,
           m_norm_g, m_skip, a_sinks, w_out):
    bsz = x.shape[0]
    mod4 = _modulation(c, w_mod, b_mod).reshape(DEPTH, bsz, 1, 3 * D_MODEL)
    slopes = jnp.asarray(2.0 ** (-8.0 * np.arange(1, A_HEADS + 1) / A_HEADS), dtype=jnp.float32)
    for l in range(DEPTH):
        w_pack, wg_t = _pack_w_in(w_in[l])
        wm, wa, ws = _pack_w_out(w_out[l])
        gbias = jnp.broadcast_to(jnp.concatenate([m_b_i[l], m_b_f[l]])[:, None], (2 * M_HEADS, M_CHUNK))
        proj, gates, ya = _inproj(x, mod4, l, g_pre[l][None], w_pack, wg_t,
                                  _pad_heads(m_conv_w[l], 1), _pad_heads(m_conv_b[l][None], 1), slopes, a_sinks[l])
        ym = _mlstm(proj, gates, gbias, _pad_qk(m_wq[l]), _pad_qk(jnp.swapaxes(m_wk[l], 1, 2)),
                    _pad_heads(m_norm_g[l][None], 1), _pad_heads(m_skip[l][None], 1))
        ys = _stickbreak(proj)
        x = _outproj(x, ym, ya, ys, wm, wa, ws, g_post[l][None], mod4, l)
    return x
```

```python
import functools

import jax
import jax.numpy as jnp
import numpy as np
from jax import lax
from jax.experimental import pallas as pl
from jax.experimental.pallas import tpu as pltpu

D_MODEL = 1024
DEPTH = 2
M_HEADS = 4
M_DH = 96
M_W = M_HEADS * M_DH
M_CONV = 4
A_HEADS = 6
A_KV = 2
A_G = A_HEADS // A_KV
A_DH = 64
A_W = A_HEADS * A_DH
A_KVW = A_KV * A_DH
WINDOW = 128
S_HEADS = 4
S_DH = 64
S_W = S_HEADS * S_DH
EPS = 1e-6

LANES = 128
SUBLANES = 8
GATE_ROWS = 2 * M_HEADS
M_CHUNK = 128
M_WP = M_HEADS * LANES
SB_TILE = 256
S_PAIRS = S_W // LANES
INPROJ_ROWS, INPROJ_SUB, INPROJ_COLS = 1024, 128, 512
OUTPROJ_ROWS, OUTPROJ_SUB = 2048, 256
SB_DEAD = -93.0

OFF_MU, OFF_MV, OFF_MZ = 0, M_WP, 2 * M_WP
OFF_AQ = 3 * M_WP
OFF_AZ = OFF_AQ + A_W
OFF_AK = OFF_AZ + A_W
OFF_AV = OFF_AK + A_KVW
OFF_SQ = OFF_AV + A_KVW
OFF_SK = OFF_SQ + S_W
OFF_SV = OFF_SK + S_W
OFF_SZ = OFF_SV + S_W
N_PACK = OFF_SZ + S_W

VMEM_LIMIT = 48 * 1024 * 1024

_NT = (((1,), (1,)), ((), ()))
LOG2E = 1.4426950408889634


def _log_sigmoid(x):
    return jnp.minimum(x, 0.0) - jnp.log(1.0 + jnp.exp(-jnp.abs(x)))


def _silu(x):
    u = 0.5 * x
    return u + u * jnp.tanh(u)


def _params(n_axes):
    return pltpu.CompilerParams(dimension_semantics=("arbitrary",) * n_axes, vmem_limit_bytes=VMEM_LIMIT)


def _mod_kernel(c_ref, w_ref, b_ref, o_ref):
    c_act = _silu(c_ref[...]).astype(jnp.bfloat16)
    o_ref[0] = jnp.dot(c_act, w_ref[0].astype(jnp.bfloat16), preferred_element_type=jnp.float32) + b_ref[0]


def _modulation(c, w_mod, b_mod):
    depth, d, n = w_mod.shape
    bsz = c.shape[0]
    tn = 1024
    return pl.pallas_call(
        _mod_kernel,
        grid=(depth, n // tn),
        in_specs=[pl.BlockSpec((bsz, d), lambda l, j: (0, 0)),
                  pl.BlockSpec((1, d, tn), lambda l, j: (l, 0, j)),
                  pl.BlockSpec((1, 1, tn), lambda l, j: (l, 0, j))],
        out_specs=pl.BlockSpec((1, bsz, tn), lambda l, j: (l, 0, j)),
        out_shape=jax.ShapeDtypeStruct((depth, bsz, n), jnp.float32),
        compiler_params=_params(2),
        name="modulation",
    )(c, w_mod, b_mod.reshape(depth, 1, n))


def _inproj_kernel(x_ref, shift_ref, scale_ref, g_ref, wt_ref, wg_ref, cw_ref, cb_ref, slope_ref, sink_ref,
                   proj_ref, gates_ref, ya_ref, conv_ref, w_ref, kv_ref, *, tm, sub, nchunk):
    s = pl.program_id(1)
    f32 = jnp.float32

    @pl.when((pl.program_id(0) == 0) & (s == 0))
    def _():
        for c0 in range(0, N_PACK, 2 * LANES):
            w_ref[:, c0:c0 + 2 * LANES] = wt_ref[c0:c0 + 2 * LANES, :].T

    @pl.when(s == 0)
    def _():
        conv_ref[0:SUBLANES, :] = jnp.zeros((SUBLANES, M_WP), f32)
        kv_ref[...] = jnp.zeros(kv_ref.shape, kv_ref.dtype)

    subs = [slice(i * sub, (i + 1) * sub) for i in range(tm // sub)]
    gain = g_ref[...] * (1.0 + scale_ref[...])
    hs = []
    for r in subs:
        x = x_ref[0, r, :]
        ms = jnp.mean(x * x, axis=-1, keepdims=True)
        hs.append((x * lax.rsqrt(ms + EPS) * gain + shift_ref[...]).astype(jnp.bfloat16))

    swa_consts = _swa_consts()
    for i, (r, h) in enumerate(zip(subs, hs)):
        gates_ref[0, :, r] = lax.dot_general(wg_ref[...], h, _NT, preferred_element_type=f32)

        base = SUBLANES + i * sub
        conv_ref[base:base + sub, :] = jnp.dot(h, w_ref[:, OFF_MU:OFF_MU + M_WP], preferred_element_type=f32)
        acc = cb_ref[...] + cw_ref[M_CONV - 1:M_CONV, :] * conv_ref[base:base + sub, :]
        for j in range(M_CONV - 1):
            lag = M_CONV - 1 - j
            acc = acc + cw_ref[j:j + 1, :] * conv_ref[base - lag:base - lag + sub, :]
        proj_ref[0, r, OFF_MU:OFF_MU + M_WP] = _silu(acc).astype(jnp.bfloat16)

        for c0 in range(OFF_MV, N_PACK, nchunk):
            proj_ref[0, r, c0:c0 + nchunk] = jnp.dot(
                h, w_ref[:, c0:c0 + nchunk], preferred_element_type=f32).astype(jnp.bfloat16)

        grab = lambda rows, off, w: proj_ref[0, rows, off:off + w]
        kp = kv_ref[0] if i == 0 else grab(subs[i - 1], OFF_AK, A_KVW)
        vp = kv_ref[1] if i == 0 else grab(subs[i - 1], OFF_AV, A_KVW)
        ya_ref[0, r, :] = _swa_block(swa_consts, s * (tm // sub) + i, grab(r, OFF_AQ, A_W), grab(r, OFF_AZ, A_W),
                                     grab(r, OFF_AK, A_KVW), kp, grab(r, OFF_AV, A_KVW), vp, slope_ref, sink_ref)

    conv_ref[0:SUBLANES, :] = conv_ref[tm:tm + SUBLANES, :]
    kv_ref[0] = proj_ref[0, subs[-1], OFF_AK:OFF_AK + A_KVW]
    kv_ref[1] = proj_ref[0, subs[-1], OFF_AV:OFF_AV + A_KVW]


def _inproj(x, mod4, layer, g_pre, w_pack, wg_t, conv_w, conv_b, slopes, sinks):
    bsz, seq, d = x.shape
    tm = INPROJ_ROWS
    assert INPROJ_SUB == WINDOW
    kern = functools.partial(_inproj_kernel, tm=tm, sub=INPROJ_SUB, nchunk=INPROJ_COLS)
    const = lambda shape: pl.BlockSpec(shape, lambda b, s: (0,) * len(shape), pipeline_mode=pl.Buffered(1))
    smem = pl.BlockSpec(memory_space=pltpu.SMEM)
    return pl.pallas_call(
        kern,
        grid=(bsz, seq // tm),
        in_specs=[pl.BlockSpec((1, tm, d), lambda b, s: (b, s, 0)),
                  pl.BlockSpec((None, None, 1, d), lambda b, s: (layer, b, 0, 0)),
                  pl.BlockSpec((None, None, 1, d), lambda b, s: (layer, b, 0, 1)),
                  const((1, d)), const((N_PACK, d)), const((GATE_ROWS, d)), const((M_CONV, M_WP)), const((1, M_WP)),
                  smem, smem],
        out_specs=[pl.BlockSpec((1, tm, N_PACK), lambda b, s: (b, s, 0)),
                   pl.BlockSpec((1, GATE_ROWS, tm), lambda b, s: (b, 0, s)),
                   pl.BlockSpec((1, tm, A_W), lambda b, s: (b, s, 0))],
        out_shape=[jax.ShapeDtypeStruct((bsz, seq, N_PACK), jnp.bfloat16),
                   jax.ShapeDtypeStruct((bsz, GATE_ROWS, seq), jnp.float32),
                   jax.ShapeDtypeStruct((bsz, seq, A_W), jnp.bfloat16)],
        scratch_shapes=[pltpu.VMEM((tm + SUBLANES, M_WP), jnp.float32), pltpu.VMEM((d, N_PACK), jnp.bfloat16),
                        pltpu.VMEM((2, WINDOW, A_KVW), jnp.bfloat16)],
        compiler_params=_params(2),
        name="inproj",
    )(x, mod4, mod4, g_pre, w_pack, wg_t, conv_w, conv_b, slopes, sinks)


def _mlstm_kernel(cu_ref, v_ref, z_ref, gates_ref, gbias_ref, wq_ref, wkt_ref, hg_ref, skip_ref,
                  o_ref, st_ref, a_ref, b_ref, mp_ref, ml_ref, *, seq):
    L = M_CHUNK
    nchunks = seq // L
    f32 = jnp.float32
    bf16 = jnp.bfloat16
    lane8 = lax.broadcasted_iota(jnp.int32, (GATE_ROWS, L), 1)
    row8 = lax.broadcasted_iota(jnp.int32, (GATE_ROWS, L), 0)

    amax, gsum = [], []
    for c in range(nchunks):
        g = gates_ref[0, :, c * L:(c + 1) * L] + gbias_ref[...]
        cum = jnp.where(row8 >= M_HEADS, _log_sigmoid(g), 0.0)
        k = 1
        while k < L:
            cum = cum + jnp.where(lane8 >= k, pltpu.roll(cum, k, axis=1), 0.0)
            k *= 2
        b8 = pltpu.roll(cum, M_HEADS, axis=0) * LOG2E
        a8 = g * LOG2E - b8
        a_ref[:, c * L:(c + 1) * L] = a8
        b_ref[:, c * L:(c + 1) * L] = b8
        amax.append(jnp.broadcast_to(jnp.max(a8, axis=1, keepdims=True), (GATE_ROWS, L)))
        gsum.append(jnp.broadcast_to(jnp.max(jnp.where(lane8 == L - 1, b8, -jnp.inf), axis=1, keepdims=True),
                                     (GATE_ROWS, L)))
    m_prev = jnp.zeros((GATE_ROWS, L), f32)
    for c in range(nchunks):
        m_last = jnp.maximum(m_prev, amax[c])
        mp_ref[:, c * L:(c + 1) * L] = m_prev
        ml_ref[:, c * L:(c + 1) * L] = m_last
        m_prev = gsum[c] + m_last

    st_ref[...] = jnp.zeros(st_ref.shape, f32)

    t_idx = lax.broadcasted_iota(jnp.int32, (L, L), 0)
    s_idx = lax.broadcasted_iota(jnp.int32, (L, L), 1)
    tri = s_idx <= t_idx
    eye = s_idx == t_idx
    ones = jnp.ones((L, LANES), bf16)
    scale = M_DH ** -0.5
    heads = range(M_HEADS)
    cols = [slice(h * LANES, (h + 1) * LANES) for h in heads]

    def chunk(c, carry):
        r0 = pl.multiple_of(c * L, L)
        rows = pl.ds(r0, L)
        cu = [cu_ref[0, rows, cols[h]] for h in heads]
        q = [(jnp.dot(cu[h], wq_ref[h], preferred_element_type=f32) * scale).astype(bf16) for h in heads]
        kt = [lax.dot_general(wkt_ref[h], cu[h], _NT, preferred_element_type=f32) for h in heads]
        st = [st_ref[h] for h in heads]
        qk = [jnp.dot(q[h], kt[h].astype(bf16), preferred_element_type=f32) for h in heads]
        qs = [jnp.dot(q[h], st[h].astype(bf16), preferred_element_type=f32) for h in heads]
        smat, mmb, bb = [], [], []
        for h in heads:
            a_mat = jnp.broadcast_to(a_ref[h:h + 1, rows], (L, L))
            m1 = jnp.max(jnp.where(tri, a_mat, -jnp.inf), axis=1, keepdims=True)
            mm = jnp.maximum(mp_ref[h:h + 1, rows], m1)
            p = jnp.where(tri, jnp.exp2(a_mat - mm), 0.0)
            smat.append((qk[h] * p).astype(bf16))
            mmb.append(mm)
            b_mat = jnp.broadcast_to(b_ref[h:h + 1, rows], (L, L))
            bb.append(jnp.broadcast_to(jnp.sum(jnp.where(eye, b_mat, 0.0), axis=1, keepdims=True), (L, LANES)))
        v2 = [jnp.concatenate([v_ref[0, rows, cols[h]], ones], axis=1) for h in heads]
        sv = [jnp.dot(smat[h], v2[h], preferred_element_type=f32) for h in heads]
        upd = []
        for h in heads:
            ws = jnp.exp2(a_ref[h:h + 1, rows] - ml_ref[h:h + 1, rows])
            upd.append(jnp.dot((kt[h] * ws).astype(bf16), v2[h], preferred_element_type=f32))
        for h in heads:
            mp_row = mp_ref[h:h + 1, rows]
            w_inter = jnp.exp2(mp_row - mmb[h])
            num = w_inter * qs[h][:, :LANES] + sv[h][:, :LANES]
            den = w_inter * qs[h][:, LANES:] + sv[h][:, LANES:]
            hh = num / jnp.maximum(jnp.abs(den), jnp.exp2(-(bb[h] + mmb[h])))
            hh = hh * lax.rsqrt(jnp.sum(hh * hh, axis=1, keepdims=True) * (1.0 / M_DH) + EPS)
            y = hh * hg_ref[:, cols[h]] + skip_ref[:, cols[h]] * cu[h].astype(f32)
            y = y * _silu(z_ref[0, rows, cols[h]].astype(f32))
            o_ref[0, rows, cols[h]] = y.astype(o_ref.dtype)
            cs = jnp.exp2(mp_row - ml_ref[h:h + 1, rows])
            st_ref[h] = jnp.concatenate([cs, cs], axis=1) * st[h] + upd[h]
        return carry

    lax.fori_loop(0, nchunks, chunk, 0, unroll=8)


def _mlstm(proj, gates, gbias, wq, wkt, hn_g, skip):
    bsz, seq, _ = proj.shape
    kern = functools.partial(_mlstm_kernel, seq=seq)
    blk = lambda j: pl.BlockSpec((1, seq, M_WP), lambda b: (b, 0, j))
    full = lambda shape: pl.BlockSpec(shape, lambda b: (0,) * len(shape))
    rows = pltpu.VMEM((GATE_ROWS, seq), jnp.float32)
    return pl.pallas_call(
        kern,
        grid=(bsz,),
        in_specs=[blk(OFF_MU // M_WP), blk(OFF_MV // M_WP), blk(OFF_MZ // M_WP),
                  pl.BlockSpec((1, GATE_ROWS, seq), lambda b: (b, 0, 0)),
                  full((GATE_ROWS, M_CHUNK)), full((M_HEADS, LANES, LANES)), full((M_HEADS, LANES, LANES)),
                  full((1, M_WP)), full((1, M_WP))],
        out_specs=pl.BlockSpec((1, seq, M_WP), lambda b: (b, 0, 0)),
        out_shape=jax.ShapeDtypeStruct((bsz, seq, M_WP), jnp.bfloat16),
        scratch_shapes=[pltpu.VMEM((M_HEADS, LANES, 2 * LANES), jnp.float32), rows, rows, rows, rows],
        compiler_params=_params(1),
        name="mlstm",
    )(proj, proj, proj, gates, gbias, wq, wkt, hn_g, skip)


def _swa_consts():
    W = WINDOW
    ti = lax.broadcasted_iota(jnp.int32, (W, W), 0)
    si = lax.broadcasted_iota(jnp.int32, (W, W), 1)
    cur = si <= ti
    neg_rel = -jnp.where(cur, ti - si, W + ti - si).astype(jnp.float32)
    lane = lax.broadcasted_iota(jnp.int32, (W, LANES), 1)
    lane2 = lax.broadcasted_iota(jnp.int32, (2 * W, LANES), 1)
    return cur, neg_rel, (lane < A_DH, lane >= A_DH), (lane2 < A_DH, lane2 >= A_DH)


def _swa_block(consts, n, q_all, z, kc, kp, vc, vp, slope_ref, sink_ref):
    f32 = jnp.float32
    bf16 = jnp.bfloat16
    W = WINDOW
    cur, neg_rel, halves, halves2 = consts
    bias = jnp.where((cur.astype(jnp.int32) + n) > 0, neg_rel, -jnp.inf)
    kk = jnp.concatenate([kc, kp], axis=0)
    vv = jnp.concatenate([vc, vp], axis=0)
    vms = [jnp.where(hm, vv, jnp.zeros_like(vv)) for hm in halves2]
    q_all = q_all * jnp.asarray(A_DH ** -0.5, q_all.dtype)
    pairs = [(g, kv) for g in range(A_G) for kv in range(A_KV)]
    sc = []
    for g, kv in pairs:
        qg = q_all[:, g * LANES:(g + 1) * LANES]
        qm = jnp.where(halves[kv], qg, jnp.zeros_like(qg))
        sc.append(lax.dot_general(qm, kk, _NT, preferred_element_type=f32))
    pc = []
    for i, (g, kv) in enumerate(pairs):
        head = kv * A_G + g
        s2 = jnp.where(cur, sc[i][:, :W], sc[i][:, W:]) + slope_ref[head] * bias
        sink = sink_ref[head]
        mx = jnp.maximum(jnp.max(s2, axis=1, keepdims=True), sink)
        e = jnp.exp(s2 - mx)
        den = jnp.sum(e, axis=1, keepdims=True) + jnp.exp(sink - mx)
        probs = (e * (1.0 / den)).astype(bf16)
        zero = jnp.zeros_like(probs)
        pc.append(jnp.concatenate([jnp.where(cur, probs, zero), jnp.where(cur, zero, probs)], axis=1))
    outs = []
    for g in range(A_G):
        o = (jnp.dot(pc[2 * g], vms[0], preferred_element_type=f32)
             + jnp.dot(pc[2 * g + 1], vms[1], preferred_element_type=f32))
        outs.append((o * _silu(z[:, g * LANES:(g + 1) * LANES].astype(f32))).astype(bf16))
    return jnp.concatenate(outs, axis=1)


def _sb_kernel(q_ref, z_ref, k_ref, v_ref, o_ref, acc_ref, run_ref, zz_ref, a_ref, *, seq):
    f32 = jnp.float32
    bf16 = jnp.bfloat16
    T = SB_TILE
    lane = lax.broadcasted_iota(jnp.int32, (T, LANES), 1)
    halves = (lane < S_DH, lane >= S_DH)
    ti = lax.broadcasted_iota(jnp.int32, (T, T), 0)
    si = lax.broadcasted_iota(jnp.int32, (T, T), 1)
    strict = si < ti
    usum = jnp.where(ti > si, 1.0, 0.0).astype(bf16)
    heads = range(S_HEADS)
    cols = [slice((h // 2) * LANES, (h // 2 + 1) * LANES) for h in heads]

    def qblock(n, carry):
        r0 = pl.multiple_of(n * T, T)
        q = q_ref[0, pl.ds(r0, T), :]
        q = q * jnp.asarray(S_DH ** -0.5, q.dtype)
        qm = [jnp.where(halves[h % 2], q[:, cols[h]], jnp.zeros((T, LANES), q.dtype)) for h in heads]
        acc_ref[...] = jnp.zeros(acc_ref.shape, f32)
        run_ref[...] = jnp.zeros(run_ref.shape, f32)

        def score(j):
            k0 = pl.multiple_of(j * T, T)
            kb = k_ref[0, pl.ds(k0, T), :]
            for h in heads:
                zz_ref[h] = lax.dot_general(qm[h], kb[:, cols[h]], _NT, preferred_element_type=f32)

        def apply(j):
            k0 = pl.multiple_of(j * T, T)
            vb = v_ref[0, pl.ds(k0, T), :]
            for p in range(S_PAIRS):
                pv = None
                for h in (2 * p, 2 * p + 1):
                    vm = jnp.where(halves[h % 2], vb[:, cols[h]], jnp.zeros((T, LANES), vb.dtype))
                    d = jnp.dot(a_ref[h], vm, preferred_element_type=f32)
                    pv = d if pv is None else pv + d
                acc_ref[:, cols[2 * p]] += pv

        def weights(j_next, diag):
            ls, lk = [], []
            for h in heads:
                zz = zz_ref[h]
                soft = jnp.log(1.0 + jnp.exp2(jnp.abs(zz) * (-LOG2E)))
                ls.append(jnp.minimum(zz, 0.0) - soft)
                lkh = ls[h] - zz
                lk.append(jnp.where(strict, lkh, 0.0) if diag else lkh)
            suf = [jnp.dot(lk[h].astype(bf16), usum, preferred_element_type=f32) for h in heads]
            runs = [run_ref[h] for h in heads]
            new_runs = [runs[h] + jnp.sum(lk[h], axis=1, keepdims=True) for h in heads]
            live = jnp.max(functools.reduce(jnp.maximum, new_runs)) > SB_DEAD
            if j_next is not None:
                score(j_next)
            for h in heads:
                ah = jnp.exp2((ls[h] + suf[h] + jnp.concatenate([runs[h]] * (T // LANES), axis=1)) * LOG2E)
                a_ref[h] = (jnp.where(strict, ah, 0.0) if diag else ah).astype(bf16)
                run_ref[h] = new_runs[h]
            return live.astype(jnp.int32)

        score(n)
        live0 = weights(n - 1 if n >= 1 else None, True)
        first = 1
        if n >= 1:
            apply(n)
            live0 = weights(None, False)
            first = 2

        def cond(c):
            i, live = c
            return jnp.logical_and(i <= n, live > 0)

        def body(c):
            i, _ = c
            apply(n - i + 1)
            score(n - i)
            return i + 1, weights(None, False)

        i_end, _ = lax.while_loop(cond, body, (jnp.int32(first), live0))
        apply(n - i_end + 1)
        zg = z_ref[0, pl.ds(r0, T), :].astype(f32)
        o_ref[0, pl.ds(r0, T), :] = (acc_ref[...] * _silu(zg)).astype(o_ref.dtype)
        return carry

    for n in range(seq // T):
        qblock(n, 0)


def _stickbreak(proj):
    bsz, seq, _ = proj.shape
    kern = functools.partial(_sb_kernel, seq=seq)
    full = lambda off: pl.BlockSpec((1, seq, S_W), lambda b: (b, 0, off // S_W))
    return pl.pallas_call(
        kern,
        grid=(bsz,),
        in_specs=[full(OFF_SQ), full(OFF_SZ), full(OFF_SK), full(OFF_SV)],
        out_specs=pl.BlockSpec((1, seq, S_W), lambda b: (b, 0, 0)),
        out_shape=jax.ShapeDtypeStruct((bsz, seq, S_W), jnp.bfloat16),
        scratch_shapes=[pltpu.VMEM((SB_TILE, S_W), jnp.float32),
                        pltpu.VMEM((S_HEADS, SB_TILE, LANES), jnp.float32),
                        pltpu.VMEM((S_HEADS, SB_TILE, SB_TILE), jnp.float32),
                        pltpu.VMEM((S_HEADS, SB_TILE, SB_TILE), jnp.bfloat16)],
        compiler_params=_params(1),
        name="stickbreak",
    )(proj, proj, proj, proj)


def _outproj_kernel(x_ref, ym_ref, ya_ref, ys_ref, wm_ref, wa_ref, ws_ref, g_ref, gate_ref, o_ref, *, tm, sub):
    f32 = jnp.float32
    gain = gate_ref[...] * g_ref[...]
    for i in range(tm // sub):
        r = slice(i * sub, (i + 1) * sub)
        y = (jnp.dot(ym_ref[0, r, :], wm_ref[...], preferred_element_type=f32)
             + jnp.dot(ya_ref[0, r, :], wa_ref[...], preferred_element_type=f32)
             + jnp.dot(ys_ref[0, r, :], ws_ref[...], preferred_element_type=f32))
        o_ref[0, r, :] = x_ref[0, r, :] + y * lax.rsqrt(jnp.mean(y * y, axis=-1, keepdims=True) + EPS) * gain


def _outproj(x, ym, ya, ys, wm, wa, ws, g_post, mod4, layer):
    bsz, seq, d = x.shape
    tm = OUTPROJ_ROWS
    kern = functools.partial(_outproj_kernel, tm=tm, sub=OUTPROJ_SUB)
    row = lambda w: pl.BlockSpec((1, tm, w), lambda b, s: (b, s, 0))
    const = lambda shape: pl.BlockSpec(shape, lambda b, s: (0,) * len(shape), pipeline_mode=pl.Buffered(1))
    return pl.pallas_call(
        kern,
        grid=(bsz, seq // tm),
        in_specs=[row(d), row(M_WP), row(A_W), row(S_W),
                  const((M_WP, d)), const((A_W, d)), const((S_W, d)), const((1, d)),
                  pl.BlockSpec((None, None, 1, d), lambda b, s: (layer, b, 0, 2))],
        out_specs=row(d),
        out_shape=jax.ShapeDtypeStruct((bsz, seq, d), jnp.float32),
        compiler_params=_params(2),
        name="outproj",
    )(x, ym, ya, ys, wm, wa, ws, g_post, mod4)


def _pad_heads(w, axis):
    shape = list(w.shape)
    shape[axis:axis + 1] = [M_HEADS, M_DH]
    w = w.reshape(shape)
    pad = [(0, 0)] * w.ndim
    pad[axis + 1] = (0, LANES - M_DH)
    w = jnp.pad(w, pad)
    shape[axis:axis + 2] = [M_WP]
    return w.reshape(shape)


def _pair_heads(w, axis):
    shape = list(w.shape)
    shape[axis:axis + 1] = [A_KV, A_G, A_DH]
    w = jnp.swapaxes(w.reshape(shape), axis, axis + 1)
    shape[axis:axis + 3] = [A_W]
    return w.reshape(shape)


def _pack_w_in(w):
    o = np.cumsum([0, M_W, M_W, M_HEADS, M_HEADS, M_W, A_W, A_KVW, A_KVW, A_W, S_W, S_W, S_W, S_W])
    wt = w.astype(jnp.bfloat16).T
    seg = lambda i: wt[int(o[i]):int(o[i + 1])]
    packed_t = jnp.concatenate(
        [_pad_heads(seg(0), 0), _pad_heads(seg(1), 0), _pad_heads(seg(4), 0),
         _pair_heads(seg(5), 0), _pair_heads(seg(8), 0), seg(6), seg(7),
         seg(9), seg(10), seg(11), seg(12)], axis=0)
    gates_t = jnp.concatenate([seg(2), seg(3)], axis=0)
    return packed_t, gates_t


def _pack_w_out(w):
    w = w.astype(jnp.bfloat16)
    return _pad_heads(w[:M_W], 0), _pair_heads(w[M_W:M_W + A_W], 0), w[M_W + A_W:]


def _pad_qk(w):
    return jnp.pad(w, ((0, 0), (0, LANES - M_DH), (0, LANES - M_DH))).astype(jnp.bfloat16)


def kernel(x, c, w_mod, b_mod, g_pre, g_post, w_in, m_conv_w, m_conv_b, m_wq, m_wk, m_b_i, m_b_f,
           m_norm_g, m_skip, a_sinks, w_out):
    bsz = x.shape[0]
    mod4 = _modulation(c, w_mod, b_mod).reshape(DEPTH, bsz, 1, 3 * D_MODEL)
    slopes = jnp.asarray(2.0 ** (-8.0 * np.arange(1, A_HEADS + 1) / A_HEADS), dtype=jnp.float32)
    for l in range(DEPTH):
        w_pack, wg_t = _pack_w_in(w_in[l])
        wm, wa, ws = _pack_w_out(w_out[l])
        gbias = jnp.broadcast_to(jnp.concatenate([m_b_i[l], m_b_f[l]])[:, None], (2 * M_HEADS, M_CHUNK))
        proj, gates, ya = _inproj(x, mod4, l, g_pre[l][None], w_pack, wg_t,
                                  _pad_heads(m_conv_w[l], 1), _pad_heads(m_conv_b[l][None], 1), slopes, a_sinks[l])
        ym = _mlstm(proj, gates, gbias, _pad_qk(m_wq[l]), _pad_qk(jnp.swapaxes(m_wk[l], 1, 2)),
                    _pad_heads(m_norm_g[l][None], 1), _pad_heads(m_skip[l][None], 1))
        ys = _stickbreak(proj)
        x = _outproj(x, ym, ya, ys, wm, wa, ws, g_post[l][None], mod4, l)
    return x
```
